```python
import math
import jax, jax.numpy as jnp
from jax import lax
import numpy as np

D_MODEL = 2048
BATCH = 8
SEQ = 2048
DEPTH = 1
DEC_BATCH = 1
DEC_SEQ = 16384
PAST_LEN = 128

MIX_WIDTH = D_MODEL
ATTN_WIDTH = MIX_WIDTH // 2
SSM_WIDTH = MIX_WIDTH - ATTN_WIDTH
DIFF_HEAD_DIM = 64
N_DIFF_HEADS = ATTN_WIDTH // (2 * DIFF_HEAD_DIM)
ROT_DIM = DIFF_HEAD_DIM // 4
ROPE_THETA = 500000.0
Q_BLOCK = 128
SSM_GROUP = 16
N_SSM_GROUPS = SSM_WIDTH // SSM_GROUP
SSM_STATE = 64
D_FF = ((8 * D_MODEL + 3 * 256 - 1) // (3 * 256)) * 256
DEEPNORM_ALPHA = (2 * DEPTH) ** 0.25
DEEPNORM_BETA = (8 * DEPTH) ** -0.25
LN_EPS = 1e-5
DT_MIN = 1e-3
DT_MAX = 1e-1

Q_COLS = 2 * N_DIFF_HEADS * DIFF_HEAD_DIM
K_COLS = 2 * N_DIFF_HEADS * DIFF_HEAD_DIM
V_COLS = N_DIFF_HEADS * 2 * DIFF_HEAD_DIM
U_COLS = SSM_WIDTH
IN_COLS = Q_COLS + K_COLS + V_COLS + U_COLS

kernel_name = 'hybrid_diffattn_s5_deepnorm_encoder'

F32 = jnp.float32


def layer_norm(x, g, b):
    xf = x.astype(F32)
    mu = jnp.mean(xf, axis=-1, keepdims=True)
    var = jnp.mean(jnp.square(xf - mu), axis=-1, keepdims=True)
    return ((xf - mu) * lax.rsqrt(var + LN_EPS) * g.astype(F32) + b.astype(F32)).astype(x.dtype)


def rope_tables(L):
    inv = ROPE_THETA ** (-jnp.arange(0, ROT_DIM, 2, dtype=F32) / ROT_DIM)
    ang = jnp.arange(L, dtype=F32)[:, None] * inv[None, :]
    return jnp.cos(ang), jnp.sin(ang)


def rope_partial(t, cos, sin):
    half = ROT_DIM // 2
    c = cos[None, :, None, :]
    s = sin[None, :, None, :]
    t1 = t[..., :half]
    t2 = t[..., half:ROT_DIM]
    return jnp.concatenate([t1 * c - t2 * s, t2 * c + t1 * s, t[..., ROT_DIM:]], axis=-1)


def diff_attention(q, k, v, lam, lam_init, subln_w):
    bsz, L = q.shape[0], q.shape[1]
    nb = L // Q_BLOCK
    scale = DIFF_HEAD_DIM ** -0.5
    qb = (q * scale).reshape(bsz, nb, Q_BLOCK, 2 * N_DIFF_HEADS, DIFF_HEAD_DIM).transpose(1, 0, 2, 3, 4)

    def one_block(qblk):
        s = jnp.einsum('bqnd,bknd->bnqk', qblk, k)
        p = jax.nn.softmax(s, axis=-1).reshape(bsz, N_DIFF_HEADS, 2, Q_BLOCK, L)
        w = p[:, :, 0] - lam * p[:, :, 1]
        return jnp.einsum('bhqk,bkhe->bqhe', w, v)

    o = lax.map(one_block, qb)
    o = o.transpose(1, 0, 2, 3, 4).reshape(bsz, L, N_DIFF_HEADS, 2 * DIFF_HEAD_DIM)
    o = o * lax.rsqrt(jnp.mean(o * o, axis=-1, keepdims=True) + LN_EPS) * subln_w.astype(F32)
    return (o * (1.0 - lam_init)).reshape(bsz, L, ATTN_WIDTH)


def zoh(a_re, a_im, log_dt, b_re, b_im):
    dt = jnp.exp(log_dt)[:, None]
    mag = jnp.exp(a_re * dt)
    ab_re = mag * jnp.cos(a_im * dt)
    ab_im = mag * jnp.sin(a_im * dt)
    den = a_re * a_re + a_im * a_im
    nr = ab_re - 1.0
    f_re = ((nr * a_re + ab_im * a_im) / den)[..., None]
    f_im = ((ab_im * a_re - nr * a_im) / den)[..., None]
    bb_re = f_re * b_re - f_im * b_im
    bb_im = f_re * b_im + f_im * b_re
    return ab_re, ab_im, bb_re, bb_im


def complex_linear_combine(e1, e2):
    a1r, a1i, b1r, b1i = e1
    a2r, a2i, b2r, b2i = e2
    ar = a2r * a1r - a2i * a1i
    ai = a2r * a1i + a2i * a1r
    br = a2r * b1r - a2i * b1i + b2r
    bi = a2r * b1i + a2i * b1r + b2i
    return ar, ai, br, bi


def ssm_direction(u, a_re, a_im, log_dt, b_re, b_im, c_re, c_im, reverse):
    ab_re, ab_im, bb_re, bb_im = zoh(a_re, a_im, log_dt, b_re, b_im)
    bu_re = jnp.einsum('blgc,gpc->blgp', u, bb_re)
    bu_im = jnp.einsum('blgc,gpc->blgp', u, bb_im)
    a_full_re = jnp.broadcast_to(ab_re, bu_re.shape)
    a_full_im = jnp.broadcast_to(ab_im, bu_re.shape)
    _, _, h_re, h_im = lax.associative_scan(
        complex_linear_combine, (a_full_re, a_full_im, bu_re, bu_im), reverse=reverse, axis=1)
    return jnp.einsum('blgp,gcp->blgc', h_re, c_re) - jnp.einsum('blgp,gcp->blgc', h_im, c_im)


def s5_mixer(u, a_re, a_im, log_dt, b_re, b_im, c_re, c_im, d, w_glu):
    bsz, L = u.shape[0], u.shape[1]
    uf = u.astype(F32)
    ug = uf.reshape(bsz, L, N_SSM_GROUPS, SSM_GROUP)
    p = [t.astype(F32) for t in (a_re, a_im, log_dt, b_re, b_im, c_re, c_im)]
    y_fwd = ssm_direction(ug, *[t[0] for t in p], reverse=False)
    y_bwd = ssm_direction(ug, *[t[1] for t in p], reverse=True)
    y = (y_fwd + y_bwd).reshape(bsz, L, SSM_WIDTH) + d.astype(F32) * uf
    z = jax.nn.gelu(y)
    return z * jax.nn.sigmoid(z @ w_glu.astype(F32))


def encoder_layer(x, layer_idx, w_in, lambda_q1, lambda_k1, lambda_q2, lambda_k2, subln_w,
                  ssm_a_re, ssm_a_im, ssm_log_dt, ssm_b_re, ssm_b_im, ssm_c_re, ssm_c_im, ssm_d, w_glu,
                  w_out, ln1_g, ln1_b, w_gate, w_up, w_down, ln2_g, ln2_b):
    bsz, L = x.shape[0], x.shape[1]
    proj = x @ w_in
    q, k, v, u = jnp.split(proj, [Q_COLS, Q_COLS + K_COLS, Q_COLS + K_COLS + V_COLS], axis=-1)
    cos, sin = rope_tables(L)
    q = rope_partial(q.astype(F32).reshape(bsz, L, 2 * N_DIFF_HEADS, DIFF_HEAD_DIM), cos, sin)
    k = rope_partial(k.astype(F32).reshape(bsz, L, 2 * N_DIFF_HEADS, DIFF_HEAD_DIM), cos, sin)
    v = v.astype(F32).reshape(bsz, L, N_DIFF_HEADS, 2 * DIFF_HEAD_DIM)
    lam_init = 0.8 - 0.6 * math.exp(-0.3 * layer_idx)
    lam = (jnp.exp(jnp.sum(lambda_q1.astype(F32) * lambda_k1.astype(F32)))
           - jnp.exp(jnp.sum(lambda_q2.astype(F32) * lambda_k2.astype(F32))) + lam_init)
    attn = diff_attention(q, k, v, lam, lam_init, subln_w)
    ssm = s5_mixer(u, ssm_a_re, ssm_a_im, ssm_log_dt, ssm_b_re, ssm_b_im, ssm_c_re, ssm_c_im,
                   ssm_d, w_glu)
    mixed = jnp.concatenate([attn, ssm], axis=-1).astype(x.dtype) @ w_out
    x = layer_norm(DEEPNORM_ALPHA * x + mixed, ln1_g, ln1_b)
    h = jax.nn.silu(x @ w_gate) * (x @ w_up)
    x = layer_norm(DEEPNORM_ALPHA * x + h @ w_down, ln2_g, ln2_b)
    return x


def setup_inputs(seed: int = 0) -> dict:
    key = jax.random.key(seed)
    ks = jax.random.split(key, 26)

    def nrm(k, shape, std):
        return jax.random.normal(k, shape, F32) * std

    G, P, C = N_SSM_GROUPS, SSM_STATE, SSM_GROUP
    w_in = nrm(ks[2], (DEPTH, D_MODEL, IN_COLS), D_MODEL ** -0.5)
    w_in = w_in.at[..., Q_COLS + K_COLS:Q_COLS + K_COLS + V_COLS].multiply(DEEPNORM_BETA)
    n = jnp.arange(P, dtype=F32)
    return {
        'x_prompt': nrm(ks[0], (BATCH, SEQ, D_MODEL), 1.0),
        'x_sample': nrm(ks[1], (DEC_BATCH, DEC_SEQ, D_MODEL), 1.0),
        'w_in': w_in,
        'lambda_q1': nrm(ks[3], (DEPTH, DIFF_HEAD_DIM), 0.1),
        'lambda_k1': nrm(ks[4], (DEPTH, DIFF_HEAD_DIM), 0.1),
        'lambda_q2': nrm(ks[5], (DEPTH, DIFF_HEAD_DIM), 0.1),
        'lambda_k2': nrm(ks[6], (DEPTH, DIFF_HEAD_DIM), 0.1),
        'subln_w': 1.0 + nrm(ks[7], (DEPTH, 2 * DIFF_HEAD_DIM), 0.02),
        'ssm_a_re': -0.5 + nrm(ks[8], (DEPTH, 2, G, P), 0.02),
        'ssm_a_im': math.pi * n + nrm(ks[9], (DEPTH, 2, G, P), 0.02),
        'ssm_log_dt': jax.random.uniform(ks[10], (DEPTH, 2, G), F32,
                                         minval=math.log(DT_MIN), maxval=math.log(DT_MAX)),
        'ssm_b_re': nrm(ks[11], (DEPTH, 2, G, P, C), (2 * C) ** -0.5),
        'ssm_b_im': nrm(ks[12], (DEPTH, 2, G, P, C), (2 * C) ** -0.5),
        'ssm_c_re': nrm(ks[13], (DEPTH, 2, G, C, P), 0.5),
        'ssm_c_im': nrm(ks[14], (DEPTH, 2, G, C, P), 0.5),
        'ssm_d': nrm(ks[15], (DEPTH, SSM_WIDTH), 1.0),
        'w_glu': nrm(ks[16], (DEPTH, SSM_WIDTH, SSM_WIDTH), SSM_WIDTH ** -0.5),
        'w_out': nrm(ks[17], (DEPTH, MIX_WIDTH, D_MODEL), MIX_WIDTH ** -0.5 * DEEPNORM_BETA),
        'ln1_g': 1.0 + nrm(ks[18], (DEPTH, D_MODEL), 0.02),
        'ln1_b': nrm(ks[19], (DEPTH, D_MODEL), 0.02),
        'w_gate': nrm(ks[20], (DEPTH, D_MODEL, D_FF), D_MODEL ** -0.5),
        'w_up': nrm(ks[21], (DEPTH, D_MODEL, D_FF), D_MODEL ** -0.5),
        'w_down': nrm(ks[22], (DEPTH, D_FF, D_MODEL), D_FF ** -0.5 * DEEPNORM_BETA),
        'ln2_g': 1.0 + nrm(ks[23], (DEPTH, D_MODEL), 0.02),
        'ln2_b': nrm(ks[24], (DEPTH, D_MODEL), 0.02),
    }


def reference(x_prompt, x_sample, w_in, lambda_q1, lambda_k1, lambda_q2, lambda_k2, subln_w,
              ssm_a_re, ssm_a_im, ssm_log_dt, ssm_b_re, ssm_b_im, ssm_c_re, ssm_c_im, ssm_d, w_glu,
              w_out, ln1_g, ln1_b, w_gate, w_up, w_down, ln2_g, ln2_b):
    params = (w_in, lambda_q1, lambda_k1, lambda_q2, lambda_k2, subln_w,
              ssm_a_re, ssm_a_im, ssm_log_dt, ssm_b_re, ssm_b_im, ssm_c_re, ssm_c_im, ssm_d, w_glu,
              w_out, ln1_g, ln1_b, w_gate, w_up, w_down, ln2_g, ln2_b)
    y_prompt = x_prompt
    y_sample = x_sample
    for layer in range(DEPTH):
        layer_params = [p[layer] for p in params]
        y_prompt = encoder_layer(y_prompt, layer, *layer_params)
        y_sample = encoder_layer(y_sample, layer, *layer_params)
    return (y_prompt, y_sample)
```

```python
import functools
import math

import jax
import jax.numpy as jnp
from jax import lax
from jax.experimental import pallas as pl
from jax.experimental.pallas import tpu as pltpu

F32 = jnp.float32
BF16 = jnp.bfloat16

DIFF_HEAD_DIM = 64
ROT_DIM = DIFF_HEAD_DIM // 4
ROPE_THETA = 500000.0
SSM_GROUP = 16
SSM_STATE = 64
LN_EPS = 1e-5

LANES = 128
SUBLANES = 8
VMEM_LIMIT_BYTES = 56 * 1024 * 1024

SSM_CHUNK = 32
SSM_GBLK = SUBLANES
ATT_BK = 512
ATT_BQ = 512
NEG_BIG = -1e30


def _cparams(sem):
    return pltpu.CompilerParams(dimension_semantics=sem, vmem_limit_bytes=VMEM_LIMIT_BYTES)


def _inproj_kernel(x_ref, w_ref, wvt_ref, rope_ref, q_ref, k_ref, vt_ref, u_ref, xb_ref):
    j = pl.program_id(1)

    @pl.when(j == 0)
    def _():
        xb_ref[...] = x_ref[...].astype(BF16)

    def rope_to(out_ref):
        acc = jnp.dot(xb_ref[...], w_ref[...], preferred_element_type=F32)
        c = rope_ref[0, 0]
        sa = rope_ref[0, 1]
        sb = rope_ref[0, 2]
        for cb in range(acc.shape[1] // LANES):
            xs = acc[:, cb * LANES:(cb + 1) * LANES]
            nxt = pltpu.roll(xs, LANES - ROT_DIM // 2, axis=1)
            prv = pltpu.roll(xs, ROT_DIM // 2, axis=1)
            out_ref[:, cb * LANES:(cb + 1) * LANES] = (xs * c + nxt * sa + prv * sb).astype(out_ref.dtype)

    @pl.when(j == 0)
    def _():
        rope_to(q_ref)

    @pl.when(j == 1)
    def _():
        rope_to(k_ref)

    @pl.when(j == 2)
    def _():
        vt = lax.dot_general(wvt_ref[...], xb_ref[...], (((1,), (1,)), ((), ())),
                             preferred_element_type=F32)
        for cc in range(vt_ref.shape[0]):
            vt_ref[cc] = vt[:, cc * ATT_BK:(cc + 1) * ATT_BK].astype(vt_ref.dtype)

    @pl.when(j == 3)
    def _():
        u_ref[...] = jnp.dot(xb_ref[...], w_ref[...], preferred_element_type=F32)


def _inproj(x2d, w_qku, wvt, rope, seq_len, tm=512):
    T, D = x2d.shape
    W = wvt.shape[0]
    nseq_blk = seq_len // tm
    grid = (T // tm, 4)
    wmap = lambda i, j: (0, jnp.where(j >= 2, j - 1, j))
    return pl.pallas_call(
        _inproj_kernel,
        grid=grid,
        in_specs=[
            pl.BlockSpec((tm, D), lambda i, j: (i, 0)),
            pl.BlockSpec((D, W), wmap),
            pl.BlockSpec((W, D), lambda i, j: (0, 0)),
            pl.BlockSpec((1, 3, tm, LANES), lambda i, j: (jnp.minimum(j, 1), 0, i % nseq_blk, 0)),
        ],
        out_specs=[
            pl.BlockSpec((tm, W), lambda i, j: (i, 0)),
            pl.BlockSpec((tm, W), lambda i, j: (i, 0)),
            pl.BlockSpec((tm // ATT_BK, W, ATT_BK), lambda i, j: (i, 0, 0)),
            pl.BlockSpec((tm, W), lambda i, j: (i, 0)),
        ],
        out_shape=[
            jax.ShapeDtypeStruct((T, W), BF16),
            jax.ShapeDtypeStruct((T, W), BF16),
            jax.ShapeDtypeStruct((T // ATT_BK, W, ATT_BK), BF16),
            jax.ShapeDtypeStruct((T, W), F32),
        ],
        scratch_shapes=[pltpu.VMEM((tm, D), BF16)],
        compiler_params=_cparams(("arbitrary", "arbitrary")),
        name="inproj",
    )(x2d, w_qku, wvt, rope)


def _attn_kernel(lam_ref, sub_ref, q_ref, k_ref, vt_ref, o_ref, q1t_ref, q2t_ref, o1_ref, o2_ref,
                 *, lam_init):
    nk = vt_ref.shape[0]
    bq = q_ref.shape[1]
    hd = q_ref.shape[2]

    qt = q_ref[0].astype(F32).T
    row = lax.broadcasted_iota(jnp.int32, qt.shape, 0)
    q1t_ref[...] = jnp.where(row < DIFF_HEAD_DIM, qt, 0.0).astype(BF16)
    q2t_ref[...] = jnp.where(row >= DIFF_HEAD_DIM, qt, 0.0).astype(BF16)
    o1_ref[...] = jnp.zeros_like(o1_ref)
    o2_ref[...] = jnp.zeros_like(o2_ref)

    def branch(kb, vtb, qt_ref, o_acc, m, l):
        s = jnp.dot(kb, qt_ref[...], preferred_element_type=F32)
        m_new = jnp.maximum(m, jnp.max(s, axis=0, keepdims=True))
        alpha = jnp.exp(m - m_new)
        p = jnp.exp(s - m_new)
        l_new = alpha * l + jnp.sum(p, axis=0, keepdims=True)
        pv = jnp.dot(vtb, p.astype(BF16), preferred_element_type=F32)
        o_acc[...] = alpha * o_acc[...] + pv
        return m_new, l_new

    def body(c, carry):
        m1, l1, m2, l2 = carry
        off = pl.multiple_of(c * ATT_BK, ATT_BK)
        kb = k_ref[0, pl.ds(off, ATT_BK), :]
        vtb = vt_ref[c]
        m1, l1 = branch(kb, vtb, q1t_ref, o1_ref, m1, l1)
        m2, l2 = branch(kb, vtb, q2t_ref, o2_ref, m2, l2)
        return m1, l1, m2, l2

    neg = jnp.full((1, bq), NEG_BIG, F32)
    zero = jnp.zeros((1, bq), F32)
    _, l1, _, l2 = lax.fori_loop(0, nk, body, (neg, zero, neg, zero))

    lv = lam_ref[...]
    d1 = jnp.sum(lv[0:1] * lv[1:2], axis=1, keepdims=True)
    d2 = jnp.sum(lv[2:3] * lv[3:4], axis=1, keepdims=True)
    lam = jnp.exp(d1) - jnp.exp(d2) + lam_init

    o = o1_ref[...] / l1 - lam * (o2_ref[...] / l2)
    ms = jnp.mean(o * o, axis=0, keepdims=True)
    o = o * lax.rsqrt(ms + LN_EPS) * sub_ref[...]
    o = o * (1.0 - lam_init)
    o_ref[0] = o.T.astype(o_ref.dtype)


def _attention(q, k, vt, lam_vecs, sub_col, lam_init):
    B, L, W = q.shape
    hd = 2 * DIFF_HEAD_DIM
    H = W // hd
    nk = L // ATT_BK
    bq = min(ATT_BQ, L)
    grid = (B, H, L // bq)
    return pl.pallas_call(
        functools.partial(_attn_kernel, lam_init=lam_init),
        grid=grid,
        in_specs=[
            pl.BlockSpec(lam_vecs.shape, lambda b, h, i: (0, 0)),
            pl.BlockSpec(sub_col.shape, lambda b, h, i: (0, 0)),
            pl.BlockSpec((1, bq, hd), lambda b, h, i: (b, i, h)),
            pl.BlockSpec((1, L, hd), lambda b, h, i: (b, 0, h)),
            pl.BlockSpec((nk, hd, ATT_BK), lambda b, h, i: (b, h, 0)),
        ],
        out_specs=pl.BlockSpec((1, bq, hd), lambda b, h, i: (b, i, h)),
        out_shape=jax.ShapeDtypeStruct((B, L, W), BF16),
        scratch_shapes=[
            pltpu.VMEM((hd, bq), BF16),
            pltpu.VMEM((hd, bq), BF16),
            pltpu.VMEM((hd, bq), F32),
            pltpu.VMEM((hd, bq), F32),
        ],
        compiler_params=_cparams(("arbitrary", "arbitrary", "arbitrary")),
        name="diff_attn",
    )(lam_vecs, sub_col, q, k, vt)


def _ssm_kernel(ut_ref, w_ref, m_ref, v_ref, lre_ref, lim_ref, y_ref,
                sre, sim, hfre, hfim, hbre, hbim):
    gb = ut_ref.shape[1]
    nc = ut_ref.shape[2]
    half = LANES // 2

    for gi in range(gb):
        s = jnp.dot(ut_ref[0, gi], w_ref[gi], preferred_element_type=F32)
        sre[pl.ds(gi, nc, stride=gb), :] = s[:, :LANES]
        sim[pl.ds(gi, nc, stride=gb), :] = s[:, LANES:]

    zero = jnp.zeros((gb, LANES), F32)
    hfre[pl.ds(0, gb), :] = zero
    hfim[pl.ds(0, gb), :] = zero
    hbre[pl.ds(nc * gb, gb), :] = zero
    hbim[pl.ds(nc * gb, gb), :] = zero

    lre = lre_ref[...]
    lim = lim_ref[...]
    fwd_lane = lax.broadcasted_iota(jnp.int32, (gb, LANES), 1) < half

    def body(j, carry):
        hr, hi = carry
        jf = pl.multiple_of(j * gb, gb)
        jb = pl.multiple_of((nc - 1 - j) * gb, gb)
        sr = jnp.where(fwd_lane, sre[pl.ds(jf, gb), :], sre[pl.ds(jb, gb), :])
        si = jnp.where(fwd_lane, sim[pl.ds(jf, gb), :], sim[pl.ds(jb, gb), :])
        nr = lre * hr - lim * hi + sr
        ni = lre * hi + lim * hr + si
        hfre[pl.ds(jf + gb, gb), :] = nr
        hfim[pl.ds(jf + gb, gb), :] = ni
        hbre[pl.ds(jb, gb), :] = nr
        hbim[pl.ds(jb, gb), :] = ni
        return nr, ni

    lax.fori_loop(0, nc, body, (zero, zero))

    fwd_col = lax.broadcasted_iota(jnp.int32, (nc, LANES), 1) < half
    for gi in range(gb):
        hr = jnp.where(fwd_col, hfre[pl.ds(gi, nc, stride=gb), :], hbre[pl.ds(gb + gi, nc, stride=gb), :])
        hi = jnp.where(fwd_col, hfim[pl.ds(gi, nc, stride=gb), :], hbim[pl.ds(gb + gi, nc, stride=gb), :])
        h = jnp.concatenate([hr, hi], axis=1).astype(BF16)
        y = jnp.dot(ut_ref[0, gi], m_ref[gi], preferred_element_type=F32)
        y = y + jnp.dot(h, v_ref[gi], preferred_element_type=F32)
        y_ref[0, gi] = y.astype(y_ref.dtype)


def _ssm(ut, w_t, m_t, v_t, lre, lim):
    B, G, nc, K = ut.shape
    gb = SSM_GBLK
    ncols = w_t.shape[2]
    grid = (G // gb, B)
    rows = (nc + 1) * gb
    return pl.pallas_call(
        _ssm_kernel,
        grid=grid,
        in_specs=[
            pl.BlockSpec((1, gb, nc, K), lambda g, b: (b, g, 0, 0)),
            pl.BlockSpec((gb, K, ncols), lambda g, b: (g, 0, 0)),
            pl.BlockSpec((gb, K, K), lambda g, b: (g, 0, 0)),
            pl.BlockSpec((gb, ncols, K), lambda g, b: (g, 0, 0)),
            pl.BlockSpec((gb, LANES), lambda g, b: (g, 0)),
            pl.BlockSpec((gb, LANES), lambda g, b: (g, 0)),
        ],
        out_specs=pl.BlockSpec((1, gb, nc, K), lambda g, b: (b, g, 0, 0)),
        out_shape=jax.ShapeDtypeStruct((B, G, nc, K), F32),
        scratch_shapes=[
            pltpu.VMEM((nc * gb, LANES), F32),
            pltpu.VMEM((nc * gb, LANES), F32),
            pltpu.VMEM((rows, LANES), F32),
            pltpu.VMEM((rows, LANES), F32),
            pltpu.VMEM((rows, LANES), F32),
            pltpu.VMEM((rows, LANES), F32),
        ],
        compiler_params=_cparams(("arbitrary", "arbitrary")),
        name="s5_chunked",
    )(ut, w_t, m_t, v_t, lre, lim)


def _ssm_tables(a_re, a_im, log_dt, b_re, b_im, c_re, c_im):
    hp = lax.Precision.HIGHEST
    T = SSM_CHUNK
    G, P = a_re.shape[1], a_re.shape[2]
    C = b_re.shape[3]
    dt = jnp.exp(log_dt)[..., None]
    zr = a_re * dt
    zi = a_im * dt
    tau = jnp.arange(T + 1, dtype=F32)[:, None, None, None]
    mag = jnp.exp(zr[None] * tau)
    pr = mag * jnp.cos(zi[None] * tau)
    pi = mag * jnp.sin(zi[None] * tau)
    ab_re, ab_im = pr[1], pi[1]
    den = a_re * a_re + a_im * a_im
    nr = ab_re - 1.0
    f_re = ((nr * a_re + ab_im * a_im) / den)[..., None]
    f_im = ((ab_im * a_re - nr * a_im) / den)[..., None]
    bb_re = f_re * b_re - f_im * b_im
    bb_im = f_re * b_im + f_im * b_re

    cp_re = c_re[None] * pr[:, :, :, None, :] - c_im[None] * pi[:, :, :, None, :]
    cp_im = c_re[None] * pi[:, :, :, None, :] + c_im[None] * pr[:, :, :, None, :]
    pb_re = pr[..., None] * bb_re[None] - pi[..., None] * bb_im[None]
    pb_im = pr[..., None] * bb_im[None] + pi[..., None] * bb_re[None]

    cb = (jnp.einsum('tdgcp,dgpk->tdgck', cp_re[:T], bb_re, precision=hp)
          - jnp.einsum('tdgcp,dgpk->tdgck', cp_im[:T], bb_im, precision=hp))

    s_idx = jnp.arange(T)[:, None]
    t_idx = jnp.arange(T)[None, :]
    lag = t_idx - s_idx
    mf = jnp.where((lag >= 0)[:, :, None, None, None], cb[jnp.clip(lag, 0, T - 1), 0], 0.0)
    mb = jnp.where((lag <= 0)[:, :, None, None, None], cb[jnp.clip(-lag, 0, T - 1), 1], 0.0)
    m_t = (mf + mb).transpose(2, 0, 4, 1, 3).reshape(G, T * C, T * C)

    rev = jnp.arange(T - 1, -1, -1)
    w_cols = [pb_re[rev, 0], pb_re[:T, 1], pb_im[rev, 0], pb_im[:T, 1]]
    w_t = jnp.concatenate([w.transpose(1, 0, 3, 2) for w in w_cols], axis=3)
    w_t = w_t.reshape(G, T * C, 4 * P)

    fwd_pow = jnp.arange(1, T + 1)
    bwd_pow = jnp.arange(T, 0, -1)
    v_rows = [cp_re[fwd_pow, 0], cp_re[bwd_pow, 1], -cp_im[fwd_pow, 0], -cp_im[bwd_pow, 1]]
    v_t = jnp.concatenate([v.transpose(1, 3, 0, 2) for v in v_rows], axis=1)
    v_t = v_t.reshape(G, 4 * P, T * C)

    lre = jnp.concatenate([pr[T, 0], pr[T, 1]], axis=1)
    lim = jnp.concatenate([pi[T, 0], pi[T, 1]], axis=1)
    return w_t.astype(BF16), m_t.astype(BF16), v_t.astype(BF16), lre, lim


def _layer_norm(r, g, b):
    mu = jnp.mean(r, axis=-1, keepdims=True)
    d = r - mu
    var = jnp.mean(d * d, axis=-1, keepdims=True)
    return d * lax.rsqrt(var + LN_EPS) * g + b


def _mix_kernel(x_ref, a_ref, y_ref, u_ref, d_ref, wglu_ref, wout_ref, g_ref, b_ref, o_ref, ob_ref,
                *, alpha):
    wa = a_ref.shape[1]
    yy = y_ref[...] + d_ref[...] * u_ref[...]
    z = jax.nn.gelu(yy)
    gate = jax.nn.sigmoid(jnp.dot(z.astype(BF16), wglu_ref[...], preferred_element_type=F32))
    s = (z * gate).astype(BF16)
    mixed = jnp.dot(a_ref[...], wout_ref[pl.ds(0, wa), :], preferred_element_type=F32)
    mixed = mixed + jnp.dot(s, wout_ref[pl.ds(wa, s.shape[1]), :], preferred_element_type=F32)
    r = alpha * x_ref[...] + mixed
    o = _layer_norm(r, g_ref[...], b_ref[...])
    o_ref[...] = o
    ob_ref[...] = o.astype(BF16)


def _mix(x2d, attn, y, u, d, wglu, wout, g, b, alpha, tm=256):
    T, D = x2d.shape
    W = attn.shape[1]
    row = lambda i: (i, 0)
    fixed = lambda i: (0, 0)
    return pl.pallas_call(
        functools.partial(_mix_kernel, alpha=alpha),
        grid=(T // tm,),
        in_specs=[
            pl.BlockSpec((tm, D), row),
            pl.BlockSpec((tm, W), row),
            pl.BlockSpec((tm, W), row),
            pl.BlockSpec((tm, W), row),
            pl.BlockSpec((1, W), fixed),
            pl.BlockSpec(wglu.shape, fixed),
            pl.BlockSpec(wout.shape, fixed),
            pl.BlockSpec((1, D), fixed),
            pl.BlockSpec((1, D), fixed),
        ],
        out_specs=[pl.BlockSpec((tm, D), row), pl.BlockSpec((tm, D), row)],
        out_shape=[jax.ShapeDtypeStruct((T, D), F32), jax.ShapeDtypeStruct((T, D), BF16)],
        compiler_params=_cparams(("arbitrary",)),
        name="glu_outproj_ln",
    )(x2d, attn, y, u, d, wglu, wout, g, b)


def _ffn_up_kernel(x_ref, wg_ref, wu_ref, h_ref):
    x = x_ref[...]
    g = jnp.dot(x, wg_ref[...], preferred_element_type=F32)
    up = jnp.dot(x, wu_ref[...], preferred_element_type=F32)
    h_ref[...] = (jax.nn.silu(g) * up).astype(h_ref.dtype)


def _ffn_up(xb, wg, wu, tm=1024, tf=512):
    T, D = xb.shape
    F = wg.shape[1]
    return pl.pallas_call(
        _ffn_up_kernel,
        grid=(T // tm, F // tf),
        in_specs=[
            pl.BlockSpec((tm, D), lambda i, j: (i, 0)),
            pl.BlockSpec((D, tf), lambda i, j: (0, j)),
            pl.BlockSpec((D, tf), lambda i, j: (0, j)),
        ],
        out_specs=pl.BlockSpec((tm, tf), lambda i, j: (i, j)),
        out_shape=jax.ShapeDtypeStruct((T, F), BF16),
        compiler_params=_cparams(("arbitrary", "arbitrary")),
        name="ffn_up",
    )(xb, wg, wu)


def _ffn_down_kernel(h_ref, wd_ref, x_ref, g_ref, b_ref, o_ref, acc_ref, *, alpha):
    k = pl.program_id(1)
    part = jnp.dot(h_ref[...], wd_ref[...], preferred_element_type=F32)

    @pl.when(k == 0)
    def _():
        acc_ref[...] = part

    @pl.when(k > 0)
    def _():
        acc_ref[...] = acc_ref[...] + part

    @pl.when(k == pl.num_programs(1) - 1)
    def _():
        r = alpha * x_ref[...] + acc_ref[...]
        o_ref[...] = _layer_norm(r, g_ref[...], b_ref[...])


def _ffn_down(h, wd, x1, g, b, alpha, tm=512, nk=4):
    T, F = h.shape
    D = wd.shape[1]
    tk = F // nk
    return pl.pallas_call(
        functools.partial(_ffn_down_kernel, alpha=alpha),
        grid=(T // tm, nk),
        in_specs=[
            pl.BlockSpec((tm, tk), lambda i, k: (i, k)),
            pl.BlockSpec((tk, D), lambda i, k: (k, 0)),
            pl.BlockSpec((tm, D), lambda i, k: (i, 0)),
            pl.BlockSpec((1, D), lambda i, k: (0, 0)),
            pl.BlockSpec((1, D), lambda i, k: (0, 0)),
        ],
        out_specs=pl.BlockSpec((tm, D), lambda i, k: (i, 0)),
        out_shape=jax.ShapeDtypeStruct((T, D), F32),
        scratch_shapes=[pltpu.VMEM((tm, D), F32)],
        compiler_params=_cparams(("arbitrary", "arbitrary")),
        name="ffn_down_ln",
    )(h, wd, x1, g, b)


def _rope_tables(L):
    half = ROT_DIM // 2
    inv = ROPE_THETA ** (-jnp.arange(0, ROT_DIM, 2, dtype=F32) / ROT_DIM)
    ang = jnp.arange(L, dtype=F32)[:, None] * inv[None, :]
    cos, sin = jnp.cos(ang), jnp.sin(ang)
    ones = jnp.ones((L, DIFF_HEAD_DIM - ROT_DIM), F32)
    zeros_h = jnp.zeros((L, half), F32)
    zeros_r = jnp.zeros((L, DIFF_HEAD_DIM - ROT_DIM), F32)
    c = jnp.concatenate([cos, cos, ones], axis=1)
    sa = jnp.concatenate([-sin, zeros_h, zeros_r], axis=1)
    sb = jnp.concatenate([zeros_h, sin, zeros_r], axis=1)
    tab = jnp.stack([jnp.tile(t, (1, LANES // DIFF_HEAD_DIM)) for t in (c, sa, sb)])
    return jnp.stack([tab * (DIFF_HEAD_DIM ** -0.5), tab])


def _encoder_layer(x, layer_idx, depth, p):
    (w_in, lq1, lk1, lq2, lk2, subln_w, a_re, a_im, log_dt, b_re, b_im, c_re, c_im, ssm_d, w_glu,
     w_out, ln1_g, ln1_b, w_gate, w_up, w_down, ln2_g, ln2_b) = p
    B, L, D = x.shape
    T = B * L
    W = w_glu.shape[0]
    G = a_re.shape[1]
    alpha = (2 * depth) ** 0.25
    lam_init = 0.8 - 0.6 * math.exp(-0.3 * layer_idx)

    w_qku = jnp.concatenate([w_in[:, :2 * W], w_in[:, 3 * W:]], axis=1).astype(BF16)
    wvt = w_in[:, 2 * W:3 * W].T.astype(BF16)
    rope = _rope_tables(L)
    lam_vecs = jnp.stack([lq1, lk1, lq2, lk2]).astype(F32)
    sub_col = subln_w.astype(F32).reshape(-1, 1)
    w_t, m_t, v_t, lre, lim = _ssm_tables(a_re, a_im, log_dt, b_re, b_im, c_re, c_im)

    x2d = x.reshape(T, D)
    q, k, vt, u = _inproj(x2d, w_qku, wvt, rope, L)

    attn = _attention(q.reshape(B, L, W), k.reshape(B, L, W), vt, lam_vecs, sub_col, lam_init)

    nc = L // SSM_CHUNK
    ut = (u.astype(BF16).reshape(B, nc, SSM_CHUNK, G, SSM_GROUP)
          .transpose(0, 3, 1, 2, 4).reshape(B, G, nc, SSM_CHUNK * SSM_GROUP))
    yt = _ssm(ut, w_t, m_t, v_t, lre, lim)
    y = (yt.reshape(B, G, nc, SSM_CHUNK, SSM_GROUP).transpose(0, 2, 3, 1, 4).reshape(T, W))

    x1, x1b = _mix(x2d, attn.reshape(T, W), y, u, ssm_d.reshape(1, W).astype(F32),
                   w_glu.astype(BF16), w_out.astype(BF16),
                   ln1_g.reshape(1, D).astype(F32), ln1_b.reshape(1, D).astype(F32), alpha)
    h = _ffn_up(x1b, w_gate.astype(BF16), w_up.astype(BF16))
    out = _ffn_down(h, w_down.astype(BF16), x1, ln2_g.reshape(1, D).astype(F32),
                    ln2_b.reshape(1, D).astype(F32), alpha)
    return out.reshape(B, L, D)


def kernel(x_prompt, x_sample, w_in, lambda_q1, lambda_k1, lambda_q2, lambda_k2, subln_w, ssm_a_re, ssm_a_im, ssm_log_dt, ssm_b_re, ssm_b_im, ssm_c_re, ssm_c_im, ssm_d, w_glu, w_out, ln1_g, ln1_b, w_gate, w_up, w_down, ln2_g, ln2_b):
    params = (w_in, lambda_q1, lambda_k1, lambda_q2, lambda_k2, subln_w,
              ssm_a_re, ssm_a_im, ssm_log_dt, ssm_b_re, ssm_b_im, ssm_c_re, ssm_c_im, ssm_d, w_glu,
              w_out, ln1_g, ln1_b, w_gate, w_up, w_down, ln2_g, ln2_b)
    depth = w_in.shape[0]
    y_prompt, y_sample = x_prompt, x_sample
    for layer in range(depth):
        lp = tuple(t[layer] for t in params)
        y_prompt = _encoder_layer(y_prompt, layer, depth, lp)
        y_sample = _encoder_layer(y_sample, layer, depth, lp)
    return (y_prompt, y_sample)
```

```python
import functools
import math

import jax
import jax.numpy as jnp
from jax import lax
from jax.experimental import pallas as pl
from jax.experimental.pallas import tpu as pltpu

F32 = jnp.float32
BF16 = jnp.bfloat16

DIFF_HEAD_DIM = 64
ROT_DIM = DIFF_HEAD_DIM // 4
ROPE_THETA = 500000.0
SSM_GROUP = 16
SSM_STATE = 64
LN_EPS = 1e-5

LANES = 128
SUBLANES = 8
VMEM_LIMIT_BYTES = 56 * 1024 * 1024

SSM_CHUNK = 32
SSM_GBLK = SUBLANES
ATT_BK = 512
ATT_BQ = 512
ATT_ONES_ROWS = 16
NEG_BIG = -1e30


def _cparams(sem):
    return pltpu.CompilerParams(dimension_semantics=sem, vmem_limit_bytes=VMEM_LIMIT_BYTES)


def _inproj_kernel(x_ref, w_ref, wvt_ref, rope_ref, q_ref, k_ref, vt_ref, u_ref, xb_ref):
    j = pl.program_id(1)

    @pl.when(j == 0)
    def _():
        xb_ref[...] = x_ref[...].astype(BF16)

    def rope_to(out_ref):
        acc = jnp.dot(xb_ref[...], w_ref[...], preferred_element_type=F32)
        c = rope_ref[0, 0]
        sa = rope_ref[0, 1]
        sb = rope_ref[0, 2]
        for cb in range(acc.shape[1] // LANES):
            xs = acc[:, cb * LANES:(cb + 1) * LANES]
            nxt = pltpu.roll(xs, LANES - ROT_DIM // 2, axis=1)
            prv = pltpu.roll(xs, ROT_DIM // 2, axis=1)
            out_ref[:, cb * LANES:(cb + 1) * LANES] = (xs * c + nxt * sa + prv * sb).astype(out_ref.dtype)

    @pl.when(j == 0)
    def _():
        rope_to(q_ref)

    @pl.when(j == 1)
    def _():
        rope_to(k_ref)

    @pl.when(j == 2)
    def _():
        vt = lax.dot_general(wvt_ref[...], xb_ref[...], (((1,), (1,)), ((), ())),
                             preferred_element_type=F32)
        hd = 2 * DIFF_HEAD_DIM
        ones = jnp.ones((vt_ref.shape[2] - hd, ATT_BK), vt_ref.dtype)
        for cc in range(vt_ref.shape[0]):
            for h in range(vt_ref.shape[1]):
                vt_ref[cc, h, pl.ds(0, hd), :] = (
                    vt[h * hd:(h + 1) * hd, cc * ATT_BK:(cc + 1) * ATT_BK].astype(vt_ref.dtype))
                vt_ref[cc, h, pl.ds(hd, ones.shape[0]), :] = ones

    @pl.when(j == 3)
    def _():
        u_ref[...] = jnp.dot(xb_ref[...], w_ref[...], preferred_element_type=F32)


def _inproj(x2d, w_qku, wvt, rope, seq_len, tm=512):
    T, D = x2d.shape
    W = wvt.shape[0]
    hd = 2 * DIFF_HEAD_DIM
    H = W // hd
    hda = hd + ATT_ONES_ROWS
    nseq_blk = seq_len // tm
    grid = (T // tm, 4)
    wmap = lambda i, j: (0, jnp.where(j >= 2, j - 1, j))
    return pl.pallas_call(
        _inproj_kernel,
        grid=grid,
        in_specs=[
            pl.BlockSpec((tm, D), lambda i, j: (i, 0)),
            pl.BlockSpec((D, W), wmap),
            pl.BlockSpec((W, D), lambda i, j: (0, 0)),
            pl.BlockSpec((1, 3, tm, LANES), lambda i, j: (jnp.minimum(j, 1), 0, i % nseq_blk, 0)),
        ],
        out_specs=[
            pl.BlockSpec((tm, W), lambda i, j: (i, 0)),
            pl.BlockSpec((tm, W), lambda i, j: (i, 0)),
            pl.BlockSpec((tm // ATT_BK, H, hda, ATT_BK), lambda i, j: (i, 0, 0, 0)),
            pl.BlockSpec((tm, W), lambda i, j: (i, 0)),
        ],
        out_shape=[
            jax.ShapeDtypeStruct((T, W), BF16),
            jax.ShapeDtypeStruct((T, W), BF16),
            jax.ShapeDtypeStruct((T // ATT_BK, H, hda, ATT_BK), BF16),
            jax.ShapeDtypeStruct((T, W), F32),
        ],
        scratch_shapes=[pltpu.VMEM((tm, D), BF16)],
        compiler_params=_cparams(("arbitrary", "arbitrary")),
        name="inproj",
    )(x2d, w_qku, wvt, rope)


def _attn_kernel(lam_ref, sub_ref, q_ref, k_ref, vt_ref, o_ref, q1t_ref, q2t_ref, o1_ref, o2_ref, s_ref,
                 *, lam_init):
    nk = vt_ref.shape[0]
    bq = q_ref.shape[1]
    hd = q_ref.shape[2]

    qt = q_ref[0].astype(F32).T
    row = lax.broadcasted_iota(jnp.int32, qt.shape, 0)
    q1t_ref[...] = jnp.where(row < DIFF_HEAD_DIM, qt, 0.0).astype(BF16)
    q2t_ref[...] = jnp.where(row >= DIFF_HEAD_DIM, qt, 0.0).astype(BF16)
    o1_ref[...] = jnp.zeros_like(o1_ref)
    o2_ref[...] = jnp.zeros_like(o2_ref)

    def scores(c, slot):
        off = pl.multiple_of(c * ATT_BK, ATT_BK)
        kb = k_ref[0, pl.ds(off, ATT_BK), :]
        tops = []
        for br, qt_ref in enumerate((q1t_ref, q2t_ref)):
            s = jnp.dot(kb, qt_ref[...], preferred_element_type=F32)
            s_ref[slot, br] = s
            tops.append(jnp.max(s, axis=0, keepdims=True))
        return tuple(tops)

    def update(c, slot, tops, ms):
        vtb = vt_ref[c]
        new_ms = []
        for br, o_acc in enumerate((o1_ref, o2_ref)):
            m_new = jnp.maximum(ms[br], tops[br])
            alpha = jnp.exp2(ms[br] - m_new)
            p = jnp.exp2(s_ref[slot, br] - m_new).astype(BF16)
            pv = jnp.dot(vtb, p, preferred_element_type=F32)
            o_acc[...] = alpha * o_acc[...] + pv
            new_ms.append(m_new)
        return tuple(new_ms)

    def body(t, carry):
        ms, tops0 = carry
        c0 = 2 * t
        tops1 = scores(c0 + 1, 1)
        ms = update(c0, 0, tops0, ms)
        tops0 = scores(c0 + 2, 0)
        ms = update(c0 + 1, 1, tops1, ms)
        return ms, tops0

    neg = jnp.full((1, bq), NEG_BIG, F32)
    tops0 = scores(0, 0)
    ms, tops0 = lax.fori_loop(0, nk // 2 - 1, body, ((neg, neg), tops0))
    tops1 = scores(nk - 1, 1)
    ms = update(nk - 2, 0, tops0, ms)
    update(nk - 1, 1, tops1, ms)

    lv = lam_ref[...]
    d1 = jnp.sum(lv[0:1] * lv[1:2], axis=1, keepdims=True)
    d2 = jnp.sum(lv[2:3] * lv[3:4], axis=1, keepdims=True)
    lam = jnp.exp(d1) - jnp.exp(d2) + lam_init

    l1 = o1_ref[pl.ds(hd, 1), :]
    l2 = o2_ref[pl.ds(hd, 1), :]
    o = o1_ref[pl.ds(0, hd), :] / l1 - lam * (o2_ref[pl.ds(0, hd), :] / l2)
    msq = jnp.mean(o * o, axis=0, keepdims=True)
    o = o * lax.rsqrt(msq + LN_EPS) * sub_ref[...]
    o = o * (1.0 - lam_init)
    o_ref[0] = o.T.astype(o_ref.dtype)


def _attention(q, k, vt, lam_vecs, sub_col, lam_init):
    B, L, W = q.shape
    hd = 2 * DIFF_HEAD_DIM
    hda = vt.shape[2]
    H = W // hd
    nk = L // ATT_BK
    assert nk % 2 == 0 and nk >= 2
    bq = min(ATT_BQ, L)
    grid = (B, H, L // bq)
    return pl.pallas_call(
        functools.partial(_attn_kernel, lam_init=lam_init),
        grid=grid,
        in_specs=[
            pl.BlockSpec(lam_vecs.shape, lambda b, h, i: (0, 0)),
            pl.BlockSpec(sub_col.shape, lambda b, h, i: (0, 0)),
            pl.BlockSpec((1, bq, hd), lambda b, h, i: (b, i, h)),
            pl.BlockSpec((1, L, hd), lambda b, h, i: (b, 0, h)),
            pl.BlockSpec((nk, None, hda, ATT_BK), lambda b, h, i: (b, h, 0, 0)),
        ],
        out_specs=pl.BlockSpec((1, bq, hd), lambda b, h, i: (b, i, h)),
        out_shape=jax.ShapeDtypeStruct((B, L, W), BF16),
        scratch_shapes=[
            pltpu.VMEM((hd, bq), BF16),
            pltpu.VMEM((hd, bq), BF16),
            pltpu.VMEM((hda, bq), F32),
            pltpu.VMEM((hda, bq), F32),
            pltpu.VMEM((2, 2, ATT_BK, bq), F32),
        ],
        compiler_params=_cparams(("arbitrary", "arbitrary", "arbitrary")),
        name="diff_attn",
    )(lam_vecs, sub_col, q, k, vt)


def _ssm_kernel(ut_ref, w_ref, m_ref, v_ref, lre_ref, lim_ref, y_ref,
                sre, sim, hfre, hfim, hbre, hbim):
    gb = ut_ref.shape[1]
    nc = ut_ref.shape[2]
    half = LANES // 2

    for gi in range(gb):
        s = jnp.dot(ut_ref[0, gi], w_ref[gi], preferred_element_type=F32)
        sre[pl.ds(gi, nc, stride=gb), :] = s[:, :LANES]
        sim[pl.ds(gi, nc, stride=gb), :] = s[:, LANES:]

    zero = jnp.zeros((gb, LANES), F32)
    hfre[pl.ds(0, gb), :] = zero
    hfim[pl.ds(0, gb), :] = zero
    hbre[pl.ds(nc * gb, gb), :] = zero
    hbim[pl.ds(nc * gb, gb), :] = zero

    lre = lre_ref[...]
    lim = lim_ref[...]
    fwd_lane = lax.broadcasted_iota(jnp.int32, (gb, LANES), 1) < half

    def body(j, carry):
        hr, hi = carry
        jf = pl.multiple_of(j * gb, gb)
        jb = pl.multiple_of((nc - 1 - j) * gb, gb)
        sr = jnp.where(fwd_lane, sre[pl.ds(jf, gb), :], sre[pl.ds(jb, gb), :])
        si = jnp.where(fwd_lane, sim[pl.ds(jf, gb), :], sim[pl.ds(jb, gb), :])
        nr = lre * hr - lim * hi + sr
        ni = lre * hi + lim * hr + si
        hfre[pl.ds(jf + gb, gb), :] = nr
        hfim[pl.ds(jf + gb, gb), :] = ni
        hbre[pl.ds(jb, gb), :] = nr
        hbim[pl.ds(jb, gb), :] = ni
        return nr, ni

    lax.fori_loop(0, nc, body, (zero, zero))

    fwd_col = lax.broadcasted_iota(jnp.int32, (nc, LANES), 1) < half
    for gi in range(gb):
        hr = jnp.where(fwd_col, hfre[pl.ds(gi, nc, stride=gb), :], hbre[pl.ds(gb + gi, nc, stride=gb), :])
        hi = jnp.where(fwd_col, hfim[pl.ds(gi, nc, stride=gb), :], hbim[pl.ds(gb + gi, nc, stride=gb), :])
        h = jnp.concatenate([hr, hi], axis=1).astype(BF16)
        y = jnp.dot(ut_ref[0, gi], m_ref[gi], preferred_element_type=F32)
        y = y + jnp.dot(h, v_ref[gi], preferred_element_type=F32)
        y_ref[0, gi] = y.astype(y_ref.dtype)


def _ssm(ut, w_t, m_t, v_t, lre, lim):
    B, G, nc, K = ut.shape
    gb = SSM_GBLK
    ncols = w_t.shape[2]
    grid = (G // gb, B)
    rows = (nc + 1) * gb
    return pl.pallas_call(
        _ssm_kernel,
        grid=grid,
        in_specs=[
            pl.BlockSpec((1, gb, nc, K), lambda g, b: (b, g, 0, 0)),
            pl.BlockSpec((gb, K, ncols), lambda g, b: (g, 0, 0)),
            pl.BlockSpec((gb, K, K), lambda g, b: (g, 0, 0)),
            pl.BlockSpec((gb, ncols, K), lambda g, b: (g, 0, 0)),
            pl.BlockSpec((gb, LANES), lambda g, b: (g, 0)),
            pl.BlockSpec((gb, LANES), lambda g, b: (g, 0)),
        ],
        out_specs=pl.BlockSpec((1, gb, nc, K), lambda g, b: (b, g, 0, 0)),
        out_shape=jax.ShapeDtypeStruct((B, G, nc, K), F32),
        scratch_shapes=[
            pltpu.VMEM((nc * gb, LANES), F32),
            pltpu.VMEM((nc * gb, LANES), F32),
            pltpu.VMEM((rows, LANES), F32),
            pltpu.VMEM((rows, LANES), F32),
            pltpu.VMEM((rows, LANES), F32),
            pltpu.VMEM((rows, LANES), F32),
        ],
        compiler_params=_cparams(("arbitrary", "arbitrary")),
        name="s5_chunked",
    )(ut, w_t, m_t, v_t, lre, lim)


def _ssm_tables(a_re, a_im, log_dt, b_re, b_im, c_re, c_im):
    hp = lax.Precision.HIGHEST
    T = SSM_CHUNK
    G, P = a_re.shape[1], a_re.shape[2]
    C = b_re.shape[3]
    dt = jnp.exp(log_dt)[..., None]
    zr = a_re * dt
    zi = a_im * dt
    tau = jnp.arange(T + 1, dtype=F32)[:, None, None, None]
    mag = jnp.exp(zr[None] * tau)
    pr = mag * jnp.cos(zi[None] * tau)
    pi = mag * jnp.sin(zi[None] * tau)
    ab_re, ab_im = pr[1], pi[1]
    den = a_re * a_re + a_im * a_im
    nr = ab_re - 1.0
    f_re = ((nr * a_re + ab_im * a_im) / den)[..., None]
    f_im = ((ab_im * a_re - nr * a_im) / den)[..., None]
    bb_re = f_re * b_re - f_im * b_im
    bb_im = f_re * b_im + f_im * b_re

    cp_re = c_re[None] * pr[:, :, :, None, :] - c_im[None] * pi[:, :, :, None, :]
    cp_im = c_re[None] * pi[:, :, :, None, :] + c_im[None] * pr[:, :, :, None, :]
    pb_re = pr[..., None] * bb_re[None] - pi[..., None] * bb_im[None]
    pb_im = pr[..., None] * bb_im[None] + pi[..., None] * bb_re[None]

    cb = (jnp.einsum('tdgcp,dgpk->tdgck', cp_re[:T], bb_re, precision=hp)
          - jnp.einsum('tdgcp,dgpk->tdgck', cp_im[:T], bb_im, precision=hp))

    s_idx = jnp.arange(T)[:, None]
    t_idx = jnp.arange(T)[None, :]
    lag = t_idx - s_idx
    mf = jnp.where((lag >= 0)[:, :, None, None, None], cb[jnp.clip(lag, 0, T - 1), 0], 0.0)
    mb = jnp.where((lag <= 0)[:, :, None, None, None], cb[jnp.clip(-lag, 0, T - 1), 1], 0.0)
    m_t = (mf + mb).transpose(2, 0, 4, 1, 3).reshape(G, T * C, T * C)

    rev = jnp.arange(T - 1, -1, -1)
    w_cols = [pb_re[rev, 0], pb_re[:T, 1], pb_im[rev, 0], pb_im[:T, 1]]
    w_t = jnp.concatenate([w.transpose(1, 0, 3, 2) for w in w_cols], axis=3)
    w_t = w_t.reshape(G, T * C, 4 * P)

    fwd_pow = jnp.arange(1, T + 1)
    bwd_pow = jnp.arange(T, 0, -1)
    v_rows = [cp_re[fwd_pow, 0], cp_re[bwd_pow, 1], -cp_im[fwd_pow, 0], -cp_im[bwd_pow, 1]]
    v_t = jnp.concatenate([v.transpose(1, 3, 0, 2) for v in v_rows], axis=1)
    v_t = v_t.reshape(G, 4 * P, T * C)

    lre = jnp.concatenate([pr[T, 0], pr[T, 1]], axis=1)
    lim = jnp.concatenate([pi[T, 0], pi[T, 1]], axis=1)
    return w_t.astype(BF16), m_t.astype(BF16), v_t.astype(BF16), lre, lim


def _layer_norm(r, g, b):
    mu = jnp.mean(r, axis=-1, keepdims=True)
    d = r - mu
    var = jnp.mean(d * d, axis=-1, keepdims=True)
    return d * lax.rsqrt(var + LN_EPS) * g + b


def _mix_kernel(x_ref, a_ref, y_ref, u_ref, d_ref, wglu_ref, wout_ref, g_ref, b_ref, o_ref, ob_ref,
                *, alpha):
    wa = a_ref.shape[1]
    yy = y_ref[...] + d_ref[...] * u_ref[...]
    z = jax.nn.gelu(yy)
    gate = jax.nn.sigmoid(jnp.dot(z.astype(BF16), wglu_ref[...], preferred_element_type=F32))
    s = (z * gate).astype(BF16)
    mixed = jnp.dot(a_ref[...], wout_ref[pl.ds(0, wa), :], preferred_element_type=F32)
    mixed = mixed + jnp.dot(s, wout_ref[pl.ds(wa, s.shape[1]), :], preferred_element_type=F32)
    r = alpha * x_ref[...] + mixed
    o = _layer_norm(r, g_ref[...], b_ref[...])
    o_ref[...] = o
    ob_ref[...] = o.astype(BF16)


def _mix(x2d, attn, y, u, d, wglu, wout, g, b, alpha, tm=256):
    T, D = x2d.shape
    W = attn.shape[1]
    row = lambda i: (i, 0)
    fixed = lambda i: (0, 0)
    return pl.pallas_call(
        functools.partial(_mix_kernel, alpha=alpha),
        grid=(T // tm,),
        in_specs=[
            pl.BlockSpec((tm, D), row),
            pl.BlockSpec((tm, W), row),
            pl.BlockSpec((tm, W), row),
            pl.BlockSpec((tm, W), row),
            pl.BlockSpec((1, W), fixed),
            pl.BlockSpec(wglu.shape, fixed),
            pl.BlockSpec(wout.shape, fixed),
            pl.BlockSpec((1, D), fixed),
            pl.BlockSpec((1, D), fixed),
        ],
        out_specs=[pl.BlockSpec((tm, D), row), pl.BlockSpec((tm, D), row)],
        out_shape=[jax.ShapeDtypeStruct((T, D), F32), jax.ShapeDtypeStruct((T, D), BF16)],
        compiler_params=_cparams(("arbitrary",)),
        name="glu_outproj_ln",
    )(x2d, attn, y, u, d, wglu, wout, g, b)


def _ffn_up_kernel(x_ref, wg_ref, wu_ref, h_ref):
    x = x_ref[...]
    g = jnp.dot(x, wg_ref[...], preferred_element_type=F32)
    up = jnp.dot(x, wu_ref[...], preferred_element_type=F32)
    h_ref[...] = (jax.nn.silu(g) * up).astype(h_ref.dtype)


def _ffn_up(xb, wg, wu, tm=1024, tf=512):
    T, D = xb.shape
    F = wg.shape[1]
    return pl.pallas_call(
        _ffn_up_kernel,
        grid=(T // tm, F // tf),
        in_specs=[
            pl.BlockSpec((tm, D), lambda i, j: (i, 0)),
            pl.BlockSpec((D, tf), lambda i, j: (0, j)),
            pl.BlockSpec((D, tf), lambda i, j: (0, j)),
        ],
        out_specs=pl.BlockSpec((tm, tf), lambda i, j: (i, j)),
        out_shape=jax.ShapeDtypeStruct((T, F), BF16),
        compiler_params=_cparams(("arbitrary", "arbitrary")),
        name="ffn_up",
    )(xb, wg, wu)


def _ffn_down_kernel(h_ref, wd_ref, x_ref, g_ref, b_ref, o_ref, acc_ref, *, alpha):
    k = pl.program_id(1)
    part = jnp.dot(h_ref[...], wd_ref[...], preferred_element_type=F32)

    @pl.when(k == 0)
    def _():
        acc_ref[...] = part

    @pl.when(k > 0)
    def _():
        acc_ref[...] = acc_ref[...] + part

    @pl.when(k == pl.num_programs(1) - 1)
    def _():
        r = alpha * x_ref[...] + acc_ref[...]
        o_ref[...] = _layer_norm(r, g_ref[...], b_ref[...])


def _ffn_down(h, wd, x1, g, b, alpha, tm=512, nk=4):
    T, F = h.shape
    D = wd.shape[1]
    tk = F // nk
    return pl.pallas_call(
        functools.partial(_ffn_down_kernel, alpha=alpha),
        grid=(T // tm, nk),
        in_specs=[
            pl.BlockSpec((tm, tk), lambda i, k: (i, k)),
            pl.BlockSpec((tk, D), lambda i, k: (k, 0)),
            pl.BlockSpec((tm, D), lambda i, k: (i, 0)),
            pl.BlockSpec((1, D), lambda i, k: (0, 0)),
            pl.BlockSpec((1, D), lambda i, k: (0, 0)),
        ],
        out_specs=pl.BlockSpec((tm, D), lambda i, k: (i, 0)),
        out_shape=jax.ShapeDtypeStruct((T, D), F32),
        scratch_shapes=[pltpu.VMEM((tm, D), F32)],
        compiler_params=_cparams(("arbitrary", "arbitrary")),
        name="ffn_down_ln",
    )(h, wd, x1, g, b)


def _rope_tables(L):
    half = ROT_DIM // 2
    inv = ROPE_THETA ** (-jnp.arange(0, ROT_DIM, 2, dtype=F32) / ROT_DIM)
    ang = jnp.arange(L, dtype=F32)[:, None] * inv[None, :]
    cos, sin = jnp.cos(ang), jnp.sin(ang)
    ones = jnp.ones((L, DIFF_HEAD_DIM - ROT_DIM), F32)
    zeros_h = jnp.zeros((L, half), F32)
    zeros_r = jnp.zeros((L, DIFF_HEAD_DIM - ROT_DIM), F32)
    c = jnp.concatenate([cos, cos, ones], axis=1)
    sa = jnp.concatenate([-sin, zeros_h, zeros_r], axis=1)
    sb = jnp.concatenate([zeros_h, sin, zeros_r], axis=1)
    tab = jnp.stack([jnp.tile(t, (1, LANES // DIFF_HEAD_DIM)) for t in (c, sa, sb)])
    return jnp.stack([tab * (DIFF_HEAD_DIM ** -0.5 * math.log2(math.e)), tab])


def _encoder_layer(x, layer_idx, depth, p):
    (w_in, lq1, lk1, lq2, lk2, subln_w, a_re, a_im, log_dt, b_re, b_im, c_re, c_im, ssm_d, w_glu,
     w_out, ln1_g, ln1_b, w_gate, w_up, w_down, ln2_g, ln2_b) = p
    B, L, D = x.shape
    T = B * L
    W = w_glu.shape[0]
    G = a_re.shape[1]
    alpha = (2 * depth) ** 0.25
    lam_init = 0.8 - 0.6 * math.exp(-0.3 * layer_idx)

    w_qku = jnp.concatenate([w_in[:, :2 * W], w_in[:, 3 * W:]], axis=1).astype(BF16)
    wvt = w_in[:, 2 * W:3 * W].T.astype(BF16)
    rope = _rope_tables(L)
    lam_vecs = jnp.stack([lq1, lk1, lq2, lk2]).astype(F32)
    sub_col = subln_w.astype(F32).reshape(-1, 1)
    w_t, m_t, v_t, lre, lim = _ssm_tables(a_re, a_im, log_dt, b_re, b_im, c_re, c_im)

    x2d = x.reshape(T, D)
    q, k, vt, u = _inproj(x2d, w_qku, wvt, rope, L)

    attn = _attention(q.reshape(B, L, W), k.reshape(B, L, W), vt, lam_vecs, sub_col, lam_init)

    nc = L // SSM_CHUNK
    ut = (u.astype(BF16).reshape(B, nc, SSM_CHUNK, G, SSM_GROUP)
          .transpose(0, 3, 1, 2, 4).reshape(B, G, nc, SSM_CHUNK * SSM_GROUP))
    yt = _ssm(ut, w_t, m_t, v_t, lre, lim)
    y = (yt.reshape(B, G, nc, SSM_CHUNK, SSM_GROUP).transpose(0, 2, 3, 1, 4).reshape(T, W))

    x1, x1b = _mix(x2d, attn.reshape(T, W), y, u, ssm_d.reshape(1, W).astype(F32),
                   w_glu.astype(BF16), w_out.astype(BF16),
                   ln1_g.reshape(1, D).astype(F32), ln1_b.reshape(1, D).astype(F32), alpha)
    h = _ffn_up(x1b, w_gate.astype(BF16), w_up.astype(BF16))
    out = _ffn_down(h, w_down.astype(BF16), x1, ln2_g.reshape(1, D).astype(F32),
                    ln2_b.reshape(1, D).astype(F32), alpha)
    return out.reshape(B, L, D)


def kernel(x_prompt, x_sample, w_in, lambda_q1, lambda_k1, lambda_q2, lambda_k2, subln_w, ssm_a_re, ssm_a_im, ssm_log_dt, ssm_b_re, ssm_b_im, ssm_c_re, ssm_c_im, ssm_d, w_glu, w_out, ln1_g, ln1_b, w_gate, w_up, w_down, ln2_g, ln2_b):
    params = (w_in, lambda_q1, lambda_k1, lambda_q2, lambda_k2, subln_w,
              ssm_a_re, ssm_a_im, ssm_log_dt, ssm_b_re, ssm_b_im, ssm_c_re, ssm_c_im, ssm_d, w_glu,
              w_out, ln1_g, ln1_b, w_gate, w_up, w_down, ln2_g, ln2_b)
    depth = w_in.shape[0]
    y_prompt, y_sample = x_prompt, x_sample
    for layer in range(depth):
        lp = tuple(t[layer] for t in params)
        y_prompt = _encoder_layer(y_prompt, layer, depth, lp)
        y_sample = _encoder_layer(y_sample, layer, depth, lp)
    return (y_prompt, y_sample)
```

```python
import functools
import math

import jax
import jax.numpy as jnp
from jax import lax
from jax.experimental import pallas as pl
from jax.experimental.pallas import tpu as pltpu

F32 = jnp.float32
BF16 = jnp.bfloat16

DIFF_HEAD_DIM = 64
ROT_DIM = DIFF_HEAD_DIM // 4
ROPE_THETA = 500000.0
SSM_GROUP = 16
SSM_STATE = 64
LN_EPS = 1e-5

LANES = 128
SUBLANES = 8
VMEM_LIMIT_BYTES = 56 * 1024 * 1024

SSM_CHUNK = 32
SSM_GBLK = SUBLANES
ATT_BK = 512
ATT_BQ = 512
ATT_ONES_ROWS = 16
NEG_BIG = -1e30


def _cparams(sem):
    return pltpu.CompilerParams(dimension_semantics=sem, vmem_limit_bytes=VMEM_LIMIT_BYTES)


def _inproj_kernel(x_ref, w_ref, wvt_ref, rope_ref, q_ref, k_ref, vt_ref, u_ref, ub_ref, xb_ref):
    j = pl.program_id(1)

    @pl.when(j == 0)
    def _():
        xb_ref[...] = x_ref[...].astype(BF16)

    def rope_to(out_ref):
        acc = jnp.dot(xb_ref[...], w_ref[...], preferred_element_type=F32)
        c = rope_ref[0, 0]
        sa = rope_ref[0, 1]
        sb = rope_ref[0, 2]
        for cb in range(acc.shape[1] // LANES):
            xs = acc[:, cb * LANES:(cb + 1) * LANES]
            nxt = pltpu.roll(xs, LANES - ROT_DIM // 2, axis=1)
            prv = pltpu.roll(xs, ROT_DIM // 2, axis=1)
            out_ref[:, cb * LANES:(cb + 1) * LANES] = (xs * c + nxt * sa + prv * sb).astype(out_ref.dtype)

    @pl.when(j == 0)
    def _():
        rope_to(q_ref)

    @pl.when(j == 1)
    def _():
        rope_to(k_ref)

    @pl.when(j == 2)
    def _():
        vt = lax.dot_general(wvt_ref[...], xb_ref[...], (((1,), (1,)), ((), ())),
                             preferred_element_type=F32)
        hd = 2 * DIFF_HEAD_DIM
        ones = jnp.ones((vt_ref.shape[2] - hd, ATT_BK), vt_ref.dtype)
        for cc in range(vt_ref.shape[0]):
            for h in range(vt_ref.shape[1]):
                vt_ref[cc, h, pl.ds(0, hd), :] = (
                    vt[h * hd:(h + 1) * hd, cc * ATT_BK:(cc + 1) * ATT_BK].astype(vt_ref.dtype))
                vt_ref[cc, h, pl.ds(hd, ones.shape[0]), :] = ones

    @pl.when(j == 3)
    def _():
        u = jnp.dot(xb_ref[...], w_ref[...], preferred_element_type=F32)
        u_ref[...] = u
        ub_ref[...] = u.astype(ub_ref.dtype)


def _inproj(x2d, w_qku, wvt, rope, seq_len, tm=512):
    T, D = x2d.shape
    W = wvt.shape[0]
    hd = 2 * DIFF_HEAD_DIM
    H = W // hd
    hda = hd + ATT_ONES_ROWS
    nseq_blk = seq_len // tm
    grid = (T // tm, 4)
    wmap = lambda i, j: (0, jnp.where(j >= 2, j - 1, j))
    return pl.pallas_call(
        _inproj_kernel,
        grid=grid,
        in_specs=[
            pl.BlockSpec((tm, D), lambda i, j: (i, 0)),
            pl.BlockSpec((D, W), wmap),
            pl.BlockSpec((W, D), lambda i, j: (0, 0)),
            pl.BlockSpec((1, 3, tm, LANES), lambda i, j: (jnp.minimum(j, 1), 0, i % nseq_blk, 0)),
        ],
        out_specs=[
            pl.BlockSpec((tm, W), lambda i, j: (i, 0)),
            pl.BlockSpec((tm, W), lambda i, j: (i, 0)),
            pl.BlockSpec((tm // ATT_BK, H, hda, ATT_BK), lambda i, j: (i, 0, 0, 0)),
            pl.BlockSpec((tm, W), lambda i, j: (i, 0)),
            pl.BlockSpec((tm, W), lambda i, j: (i, 0)),
        ],
        out_shape=[
            jax.ShapeDtypeStruct((T, W), BF16),
            jax.ShapeDtypeStruct((T, W), BF16),
            jax.ShapeDtypeStruct((T // ATT_BK, H, hda, ATT_BK), BF16),
            jax.ShapeDtypeStruct((T, W), F32),
            jax.ShapeDtypeStruct((T, W), BF16),
        ],
        scratch_shapes=[pltpu.VMEM((tm, D), BF16)],
        compiler_params=_cparams(("arbitrary", "arbitrary")),
        name="inproj",
    )(x2d, w_qku, wvt, rope)


def _attn_kernel(lam_ref, sub_ref, q_ref, k_ref, vt_ref, o_ref, q1t_ref, q2t_ref, o1_ref, o2_ref, s_ref,
                 *, lam_init):
    nk = vt_ref.shape[0]
    bq = q_ref.shape[1]
    hd = q_ref.shape[2]

    qt = q_ref[0].astype(F32).T
    row = lax.broadcasted_iota(jnp.int32, qt.shape, 0)
    q1t_ref[...] = jnp.where(row < DIFF_HEAD_DIM, qt, 0.0).astype(BF16)
    q2t_ref[...] = jnp.where(row >= DIFF_HEAD_DIM, qt, 0.0).astype(BF16)
    o1_ref[...] = jnp.zeros_like(o1_ref)
    o2_ref[...] = jnp.zeros_like(o2_ref)

    def scores(c, slot):
        off = pl.multiple_of(c * ATT_BK, ATT_BK)
        kb = k_ref[0, pl.ds(off, ATT_BK), :]
        tops = []
        for br, qt_ref in enumerate((q1t_ref, q2t_ref)):
            s = jnp.dot(kb, qt_ref[...], preferred_element_type=F32)
            s_ref[slot, br] = s
            tops.append(jnp.max(s, axis=0, keepdims=True))
        return tuple(tops)

    def update(c, slot, tops, ms):
        vtb = vt_ref[c]
        new_ms = []
        for br, o_acc in enumerate((o1_ref, o2_ref)):
            m_new = jnp.maximum(ms[br], tops[br])
            alpha = jnp.exp2(ms[br] - m_new)
            p = jnp.exp2(s_ref[slot, br] - m_new).astype(BF16)
            pv = jnp.dot(vtb, p, preferred_element_type=F32)
            o_acc[...] = alpha * o_acc[...] + pv
            new_ms.append(m_new)
        return tuple(new_ms)

    def body(t, carry):
        ms, tops0 = carry
        c0 = 2 * t
        tops1 = scores(c0 + 1, 1)
        ms = update(c0, 0, tops0, ms)
        tops0 = scores(c0 + 2, 0)
        ms = update(c0 + 1, 1, tops1, ms)
        return ms, tops0

    neg = jnp.full((1, bq), NEG_BIG, F32)
    tops0 = scores(0, 0)
    ms, tops0 = lax.fori_loop(0, nk // 2 - 1, body, ((neg, neg), tops0))
    tops1 = scores(nk - 1, 1)
    ms = update(nk - 2, 0, tops0, ms)
    update(nk - 1, 1, tops1, ms)

    lv = lam_ref[...]
    d1 = jnp.sum(lv[0:1] * lv[1:2], axis=1, keepdims=True)
    d2 = jnp.sum(lv[2:3] * lv[3:4], axis=1, keepdims=True)
    lam = jnp.exp(d1) - jnp.exp(d2) + lam_init

    l1 = o1_ref[pl.ds(hd, 1), :]
    l2 = o2_ref[pl.ds(hd, 1), :]
    o = o1_ref[pl.ds(0, hd), :] / l1 - lam * (o2_ref[pl.ds(0, hd), :] / l2)
    msq = jnp.mean(o * o, axis=0, keepdims=True)
    o = o * lax.rsqrt(msq + LN_EPS) * sub_ref[...]
    o = o * (1.0 - lam_init)
    o_ref[0] = o.T.astype(o_ref.dtype)


def _attention(q, k, vt, lam_vecs, sub_col, lam_init):
    B, L, W = q.shape
    hd = 2 * DIFF_HEAD_DIM
    hda = vt.shape[2]
    H = W // hd
    nk = L // ATT_BK
    assert nk % 2 == 0 and nk >= 2
    bq = min(ATT_BQ, L)
    grid = (B, H, L // bq)
    return pl.pallas_call(
        functools.partial(_attn_kernel, lam_init=lam_init),
        grid=grid,
        in_specs=[
            pl.BlockSpec(lam_vecs.shape, lambda b, h, i: (0, 0)),
            pl.BlockSpec(sub_col.shape, lambda b, h, i: (0, 0)),
            pl.BlockSpec((1, bq, hd), lambda b, h, i: (b, i, h)),
            pl.BlockSpec((1, L, hd), lambda b, h, i: (b, 0, h)),
            pl.BlockSpec((nk, None, hda, ATT_BK), lambda b, h, i: (b, h, 0, 0)),
        ],
        out_specs=pl.BlockSpec((1, bq, hd), lambda b, h, i: (b, i, h)),
        out_shape=jax.ShapeDtypeStruct((B, L, W), BF16),
        scratch_shapes=[
            pltpu.VMEM((hd, bq), BF16),
            pltpu.VMEM((hd, bq), BF16),
            pltpu.VMEM((hda, bq), F32),
            pltpu.VMEM((hda, bq), F32),
            pltpu.VMEM((2, 2, ATT_BK, bq), F32),
        ],
        compiler_params=_cparams(("arbitrary", "arbitrary", "arbitrary")),
        name="diff_attn",
    )(lam_vecs, sub_col, q, k, vt)


def _ssm_kernel(*refs, cps):
    T = SSM_CHUNK
    u_refs = refs[:T]
    p_ref, pt_ref, w_ref, m_ref, v_ref, lre_ref, lim_ref, y_ref = refs[T:T + 8]
    ug_ref, yg_ref, sre, sim, hfre, hfim, hbre, hbim = refs[T + 8:]
    o = pl.program_id(1)

    @pl.when(o == 0)
    def _():
        _ssm_compute(u_refs, p_ref, w_ref, m_ref, v_ref, lre_ref, lim_ref,
                     ug_ref, yg_ref, sre, sim, hfre, hfim, hbre, hbim, cps)

    yo = yg_ref[o]
    hi = yo.astype(BF16)
    lo = (yo - hi.astype(F32)).astype(BF16)
    res = (jnp.dot(hi, pt_ref[...], preferred_element_type=F32)
           + jnp.dot(lo, pt_ref[...], preferred_element_type=F32))
    for t8 in range(SUBLANES):
        y_ref[:, t8, :] = res[:, t8 * LANES:(t8 + 1) * LANES]


def _ssm_compute(u_refs, p_ref, w_ref, m_ref, v_ref, lre_ref, lim_ref,
                 ug_ref, yg_ref, sre, sim, hfre, hfim, hbre, hbim, cps):
    gb = ug_ref.shape[0]
    nc = ug_ref.shape[1]
    half = LANES // 2
    n_oct = len(u_refs) // SUBLANES

    for oc in range(n_oct):
        z = jnp.concatenate([u_refs[oc * SUBLANES + t8][...] for t8 in range(SUBLANES)], axis=1)
        zg = jnp.dot(z, p_ref[...], preferred_element_type=F32).astype(BF16)
        for gi in range(gb):
            ug_ref[gi, :, pl.ds(oc * LANES, LANES)] = zg[:, gi * LANES:(gi + 1) * LANES]

    for gi in range(gb):
        s = jnp.dot(ug_ref[gi], w_ref[gi], preferred_element_type=F32)
        sre[pl.ds(gi, nc, stride=gb), :] = s[:, :LANES]
        sim[pl.ds(gi, nc, stride=gb), :] = s[:, LANES:]

    zero = jnp.zeros((gb, LANES), F32)
    lre = lre_ref[...]
    lim = lim_ref[...]
    fwd_lane = lax.broadcasted_iota(jnp.int32, (gb, LANES), 1) < half

    def body(j, carry):
        hr, hi = carry
        fresh = (j % cps) == 0
        hr = jnp.where(fresh, 0.0, hr)
        hi = jnp.where(fresh, 0.0, hi)
        jf = pl.multiple_of(j * gb, gb)
        jb = pl.multiple_of((nc - 1 - j) * gb, gb)
        hfre[pl.ds(jf, gb), :] = hr
        hfim[pl.ds(jf, gb), :] = hi
        hbre[pl.ds(jb, gb), :] = hr
        hbim[pl.ds(jb, gb), :] = hi
        sr = jnp.where(fwd_lane, sre[pl.ds(jf, gb), :], sre[pl.ds(jb, gb), :])
        si = jnp.where(fwd_lane, sim[pl.ds(jf, gb), :], sim[pl.ds(jb, gb), :])
        return lre * hr - lim * hi + sr, lre * hi + lim * hr + si

    lax.fori_loop(0, nc, body, (zero, zero))

    fwd_col = lax.broadcasted_iota(jnp.int32, (nc, LANES), 1) < half
    for gi in range(gb):
        hr = jnp.where(fwd_col, hfre[pl.ds(gi, nc, stride=gb), :], hbre[pl.ds(gi, nc, stride=gb), :])
        hi = jnp.where(fwd_col, hfim[pl.ds(gi, nc, stride=gb), :], hbim[pl.ds(gi, nc, stride=gb), :])
        h = jnp.concatenate([hr, hi], axis=1).astype(BF16)
        y = jnp.dot(ug_ref[gi], m_ref[gi], preferred_element_type=F32)
        y = y + jnp.dot(h, v_ref[gi], preferred_element_type=F32)
        for oc in range(n_oct):
            yg_ref[oc, :, pl.ds(gi * LANES, LANES)] = y[:, oc * LANES:(oc + 1) * LANES]


def _lane_regroup_matrix():
    n = SUBLANES * LANES
    i = jnp.arange(n)
    t8, gl, c = i // LANES, (i % LANES) // SSM_GROUP, i % SSM_GROUP
    dest = gl * LANES + t8 * SSM_GROUP + c
    return (dest[:, None] == i[None, :]).astype(BF16)


def _ssm(ub, w_t, m_t, v_t, lre, lim, seq_len):
    Ttok, Wd = ub.shape
    T = SSM_CHUNK
    nc = Ttok // T
    cps = seq_len // T
    gb = SSM_GBLK
    assert gb * SSM_GROUP == LANES and T % SUBLANES == 0 and nc % cps == 0
    n_slab = Wd // LANES
    n_oct = T // SUBLANES
    K = T * SSM_GROUP
    ncols = w_t.shape[2]
    pmat = _lane_regroup_matrix()
    u2 = ub.reshape(nc, T * Wd)
    once = pl.Buffered(1)
    fixed2 = lambda g, o: (0, 0)
    per_slab = lambda g, o: (g, 0, 0)

    def u_spec(t):
        return pl.BlockSpec((nc, LANES), lambda g, o: (0, t * n_slab + g))

    y = pl.pallas_call(
        functools.partial(_ssm_kernel, cps=cps),
        grid=(n_slab, n_oct),
        in_specs=[u_spec(t) for t in range(T)] + [
            pl.BlockSpec(pmat.shape, fixed2, pipeline_mode=once),
            pl.BlockSpec(pmat.shape, fixed2, pipeline_mode=once),
            pl.BlockSpec((gb, K, ncols), per_slab, pipeline_mode=once),
            pl.BlockSpec((gb, K, K), per_slab, pipeline_mode=once),
            pl.BlockSpec((gb, ncols, K), per_slab, pipeline_mode=once),
            pl.BlockSpec((gb, LANES), lambda g, o: (g, 0)),
            pl.BlockSpec((gb, LANES), lambda g, o: (g, 0)),
        ],
        out_specs=pl.BlockSpec((nc, SUBLANES, LANES), lambda g, o: (0, o, g)),
        out_shape=jax.ShapeDtypeStruct((nc, T, Wd), F32),
        scratch_shapes=[
            pltpu.VMEM((gb, nc, K), BF16),
            pltpu.VMEM((n_oct, nc, SUBLANES * LANES), F32),
        ] + [pltpu.VMEM((nc * gb, LANES), F32)] * 6,
        compiler_params=_cparams(("arbitrary", "arbitrary")),
        name="s5_chunked",
    )(*([u2] * T), pmat, pmat.T, w_t, m_t, v_t, lre, lim)
    return y.reshape(Ttok, Wd)


def _ssm_tables(a_re, a_im, log_dt, b_re, b_im, c_re, c_im):
    hp = lax.Precision.HIGHEST
    T = SSM_CHUNK
    G, P = a_re.shape[1], a_re.shape[2]
    C = b_re.shape[3]
    dt = jnp.exp(log_dt)[..., None]
    zr = a_re * dt
    zi = a_im * dt
    tau = jnp.arange(T + 1, dtype=F32)[:, None, None, None]
    mag = jnp.exp(zr[None] * tau)
    pr = mag * jnp.cos(zi[None] * tau)
    pi = mag * jnp.sin(zi[None] * tau)
    ab_re, ab_im = pr[1], pi[1]
    den = a_re * a_re + a_im * a_im
    nr = ab_re - 1.0
    f_re = ((nr * a_re + ab_im * a_im) / den)[..., None]
    f_im = ((ab_im * a_re - nr * a_im) / den)[..., None]
    bb_re = f_re * b_re - f_im * b_im
    bb_im = f_re * b_im + f_im * b_re

    cp_re = c_re[None] * pr[:, :, :, None, :] - c_im[None] * pi[:, :, :, None, :]
    cp_im = c_re[None] * pi[:, :, :, None, :] + c_im[None] * pr[:, :, :, None, :]
    pb_re = pr[..., None] * bb_re[None] - pi[..., None] * bb_im[None]
    pb_im = pr[..., None] * bb_im[None] + pi[..., None] * bb_re[None]

    cb = (jnp.einsum('tdgcp,dgpk->tdgck', cp_re[:T], bb_re, precision=hp)
          - jnp.einsum('tdgcp,dgpk->tdgck', cp_im[:T], bb_im, precision=hp))

    kern = jnp.concatenate([cb[T - 1:0:-1, 1], (cb[0, 0] + cb[0, 1])[None], cb[1:, 0],
                            jnp.zeros_like(cb[:1, 0])]).astype(BF16)
    skew = jnp.broadcast_to(kern[None], (T,) + kern.shape).reshape((2 * T * T,) + kern.shape[1:])
    skew = skew[:T * (2 * T - 1)].reshape((T, 2 * T - 1) + kern.shape[1:])[:, T - 1:]
    m_t = skew.transpose(2, 0, 4, 1, 3).reshape(G, T * C, T * C)

    rev = jnp.arange(T - 1, -1, -1)
    w_cols = [pb_re[rev, 0], pb_re[:T, 1], pb_im[rev, 0], pb_im[:T, 1]]
    w_t = jnp.concatenate([w.transpose(1, 0, 3, 2) for w in w_cols], axis=3)
    w_t = w_t.reshape(G, T * C, 4 * P)

    fwd_pow = jnp.arange(1, T + 1)
    bwd_pow = jnp.arange(T, 0, -1)
    v_rows = [cp_re[fwd_pow, 0], cp_re[bwd_pow, 1], -cp_im[fwd_pow, 0], -cp_im[bwd_pow, 1]]
    v_t = jnp.concatenate([v.transpose(1, 3, 0, 2) for v in v_rows], axis=1)
    v_t = v_t.reshape(G, 4 * P, T * C)

    lre = jnp.concatenate([pr[T, 0], pr[T, 1]], axis=1)
    lim = jnp.concatenate([pi[T, 0], pi[T, 1]], axis=1)
    return w_t.astype(BF16), m_t.astype(BF16), v_t.astype(BF16), lre, lim


def _layer_norm(r, g, b):
    mu = jnp.mean(r, axis=-1, keepdims=True)
    d = r - mu
    var = jnp.mean(d * d, axis=-1, keepdims=True)
    return d * lax.rsqrt(var + LN_EPS) * g + b


def _mix_kernel(x_ref, a_ref, y_ref, u_ref, d_ref, wglu_ref, wout_ref, g_ref, b_ref, o_ref, ob_ref,
                *, alpha):
    wa = a_ref.shape[1]
    yy = y_ref[...] + d_ref[...] * u_ref[...]
    z = jax.nn.gelu(yy)
    gate = jax.nn.sigmoid(jnp.dot(z.astype(BF16), wglu_ref[...], preferred_element_type=F32))
    s = (z * gate).astype(BF16)
    mixed = jnp.dot(a_ref[...], wout_ref[pl.ds(0, wa), :], preferred_element_type=F32)
    mixed = mixed + jnp.dot(s, wout_ref[pl.ds(wa, s.shape[1]), :], preferred_element_type=F32)
    r = alpha * x_ref[...] + mixed
    o = _layer_norm(r, g_ref[...], b_ref[...])
    o_ref[...] = o
    ob_ref[...] = o.astype(BF16)


def _mix(x2d, attn, y, u, d, wglu, wout, g, b, alpha, tm=256):
    T, D = x2d.shape
    W = attn.shape[1]
    row = lambda i: (i, 0)
    fixed = lambda i: (0, 0)
    return pl.pallas_call(
        functools.partial(_mix_kernel, alpha=alpha),
        grid=(T // tm,),
        in_specs=[
            pl.BlockSpec((tm, D), row),
            pl.BlockSpec((tm, W), row),
            pl.BlockSpec((tm, W), row),
            pl.BlockSpec((tm, W), row),
            pl.BlockSpec((1, W), fixed),
            pl.BlockSpec(wglu.shape, fixed),
            pl.BlockSpec(wout.shape, fixed),
            pl.BlockSpec((1, D), fixed),
            pl.BlockSpec((1, D), fixed),
        ],
        out_specs=[pl.BlockSpec((tm, D), row), pl.BlockSpec((tm, D), row)],
        out_shape=[jax.ShapeDtypeStruct((T, D), F32), jax.ShapeDtypeStruct((T, D), BF16)],
        compiler_params=_cparams(("arbitrary",)),
        name="glu_outproj_ln",
    )(x2d, attn, y, u, d, wglu, wout, g, b)


def _ffn_up_kernel(x_ref, wg_ref, wu_ref, h_ref):
    x = x_ref[...]
    g = jnp.dot(x, wg_ref[...], preferred_element_type=F32)
    up = jnp.dot(x, wu_ref[...], preferred_element_type=F32)
    h_ref[...] = (jax.nn.silu(g) * up).astype(h_ref.dtype)


def _ffn_up(xb, wg, wu, tm=1024, tf=512):
    T, D = xb.shape
    F = wg.shape[1]
    return pl.pallas_call(
        _ffn_up_kernel,
        grid=(T // tm, F // tf),
        in_specs=[
            pl.BlockSpec((tm, D), lambda i, j: (i, 0)),
            pl.BlockSpec((D, tf), lambda i, j: (0, j)),
            pl.BlockSpec((D, tf), lambda i, j: (0, j)),
        ],
        out_specs=pl.BlockSpec((tm, tf), lambda i, j: (i, j)),
        out_shape=jax.ShapeDtypeStruct((T, F), BF16),
        compiler_params=_cparams(("arbitrary", "arbitrary")),
        name="ffn_up",
    )(xb, wg, wu)


def _ffn_down_kernel(h_ref, wd_ref, x_ref, g_ref, b_ref, o_ref, acc_ref, *, alpha):
    k = pl.program_id(1)
    part = jnp.dot(h_ref[...], wd_ref[...], preferred_element_type=F32)

    @pl.when(k == 0)
    def _():
        acc_ref[...] = part

    @pl.when(k > 0)
    def _():
        acc_ref[...] = acc_ref[...] + part

    @pl.when(k == pl.num_programs(1) - 1)
    def _():
        r = alpha * x_ref[...] + acc_ref[...]
        o_ref[...] = _layer_norm(r, g_ref[...], b_ref[...])


def _ffn_down(h, wd, x1, g, b, alpha, tm=512, nk=4):
    T, F = h.shape
    D = wd.shape[1]
    tk = F // nk
    return pl.pallas_call(
        functools.partial(_ffn_down_kernel, alpha=alpha),
        grid=(T // tm, nk),
        in_specs=[
            pl.BlockSpec((tm, tk), lambda i, k: (i, k)),
            pl.BlockSpec((tk, D), lambda i, k: (k, 0)),
            pl.BlockSpec((tm, D), lambda i, k: (i, 0)),
            pl.BlockSpec((1, D), lambda i, k: (0, 0)),
            pl.BlockSpec((1, D), lambda i, k: (0, 0)),
        ],
        out_specs=pl.BlockSpec((tm, D), lambda i, k: (i, 0)),
        out_shape=jax.ShapeDtypeStruct((T, D), F32),
        scratch_shapes=[pltpu.VMEM((tm, D), F32)],
        compiler_params=_cparams(("arbitrary", "arbitrary")),
        name="ffn_down_ln",
    )(h, wd, x1, g, b)


def _rope_tables(L):
    half = ROT_DIM // 2
    inv = ROPE_THETA ** (-jnp.arange(0, ROT_DIM, 2, dtype=F32) / ROT_DIM)
    ang = jnp.arange(L, dtype=F32)[:, None] * inv[None, :]
    cos, sin = jnp.cos(ang), jnp.sin(ang)
    ones = jnp.ones((L, DIFF_HEAD_DIM - ROT_DIM), F32)
    zeros_h = jnp.zeros((L, half), F32)
    zeros_r = jnp.zeros((L, DIFF_HEAD_DIM - ROT_DIM), F32)
    c = jnp.concatenate([cos, cos, ones], axis=1)
    sa = jnp.concatenate([-sin, zeros_h, zeros_r], axis=1)
    sb = jnp.concatenate([zeros_h, sin, zeros_r], axis=1)
    tab = jnp.stack([jnp.tile(t, (1, LANES // DIFF_HEAD_DIM)) for t in (c, sa, sb)])
    return jnp.stack([tab * (DIFF_HEAD_DIM ** -0.5 * math.log2(math.e)), tab])


def _prepare_layer(p):
    (w_in, lq1, lk1, lq2, lk2, subln_w, a_re, a_im, log_dt, b_re, b_im, c_re, c_im, ssm_d, w_glu,
     w_out, ln1_g, ln1_b, w_gate, w_up, w_down, ln2_g, ln2_b) = p
    W = w_glu.shape[0]
    D = w_in.shape[0]
    row = lambda t, n: t.reshape(1, n).astype(F32)
    return dict(
        w_qku=jnp.concatenate([w_in[:, :2 * W], w_in[:, 3 * W:]], axis=1).astype(BF16),
        wvt=w_in[:, 2 * W:3 * W].T.astype(BF16),
        lam_vecs=jnp.stack([lq1, lk1, lq2, lk2]).astype(F32),
        sub_col=subln_w.astype(F32).reshape(-1, 1),
        ssm=_ssm_tables(a_re, a_im, log_dt, b_re, b_im, c_re, c_im),
        ssm_d=row(ssm_d, W), w_glu=w_glu.astype(BF16), w_out=w_out.astype(BF16),
        ln1_g=row(ln1_g, D), ln1_b=row(ln1_b, D),
        w_gate=w_gate.astype(BF16), w_up=w_up.astype(BF16), w_down=w_down.astype(BF16),
        ln2_g=row(ln2_g, D), ln2_b=row(ln2_b, D),
    )


def _encoder_layer(x, layer_idx, depth, pp):
    B, L, D = x.shape
    T = B * L
    W = pp["w_glu"].shape[0]
    alpha = (2 * depth) ** 0.25
    lam_init = 0.8 - 0.6 * math.exp(-0.3 * layer_idx)

    x2d = x.reshape(T, D)
    q, k, vt, u, ub = _inproj(x2d, pp["w_qku"], pp["wvt"], _rope_tables(L), L)
    attn = _attention(q.reshape(B, L, W), k.reshape(B, L, W), vt, pp["lam_vecs"], pp["sub_col"], lam_init)
    y = _ssm(ub, *pp["ssm"], L)
    x1, x1b = _mix(x2d, attn.reshape(T, W), y, u, pp["ssm_d"], pp["w_glu"], pp["w_out"],
                   pp["ln1_g"], pp["ln1_b"], alpha)
    h = _ffn_up(x1b, pp["w_gate"], pp["w_up"])
    out = _ffn_down(h, pp["w_down"], x1, pp["ln2_g"], pp["ln2_b"], alpha)
    return out.reshape(B, L, D)


def kernel(x_prompt, x_sample, w_in, lambda_q1, lambda_k1, lambda_q2, lambda_k2, subln_w, ssm_a_re, ssm_a_im, ssm_log_dt, ssm_b_re, ssm_b_im, ssm_c_re, ssm_c_im, ssm_d, w_glu, w_out, ln1_g, ln1_b, w_gate, w_up, w_down, ln2_g, ln2_b):
    params = (w_in, lambda_q1, lambda_k1, lambda_q2, lambda_k2, subln_w,
              ssm_a_re, ssm_a_im, ssm_log_dt, ssm_b_re, ssm_b_im, ssm_c_re, ssm_c_im, ssm_d, w_glu,
              w_out, ln1_g, ln1_b, w_gate, w_up, w_down, ln2_g, ln2_b)
    depth = w_in.shape[0]
    y_prompt, y_sample = x_prompt, x_sample
    for layer in range(depth):
        lp = _prepare_layer(tuple(t[layer] for t in params))
        y_prompt = _encoder_layer(y_prompt, layer, depth, lp)
        y_sample = _encoder_layer(y_sample, layer, depth, lp)
    return (y_prompt, y_sample)
```

```python
import functools
import math

import jax
import jax.numpy as jnp
from jax import lax
from jax.experimental import pallas as pl
from jax.experimental.pallas import tpu as pltpu

F32 = jnp.float32
BF16 = jnp.bfloat16

DIFF_HEAD_DIM = 64
ROT_DIM = DIFF_HEAD_DIM // 4
ROPE_THETA = 500000.0
SSM_GROUP = 16
SSM_STATE = 64
LN_EPS = 1e-5

LANES = 128
SUBLANES = 8
VMEM_LIMIT_BYTES = 56 * 1024 * 1024

SSM_CHUNK = 32
SSM_GBLK = SUBLANES
ATT_BK = 512
ATT_BQ = 512
ATT_ONES_ROWS = 16
NEG_BIG = -1e30


def _cparams(sem):
    return pltpu.CompilerParams(dimension_semantics=sem, vmem_limit_bytes=VMEM_LIMIT_BYTES)


def _inproj_kernel(xb_ref, w_ref, wvt_ref, rope_ref, q_ref, k_ref, vt_ref, u_ref, ub_ref):
    j = pl.program_id(1)

    def rope_to(out_ref):
        acc = jnp.dot(xb_ref[...], w_ref[...], preferred_element_type=F32)
        c = rope_ref[0, 0]
        sa = rope_ref[0, 1]
        sb = rope_ref[0, 2]
        for cb in range(acc.shape[1] // LANES):
            xs = acc[:, cb * LANES:(cb + 1) * LANES]
            nxt = pltpu.roll(xs, LANES - ROT_DIM // 2, axis=1)
            prv = pltpu.roll(xs, ROT_DIM // 2, axis=1)
            out_ref[:, cb * LANES:(cb + 1) * LANES] = (xs * c + nxt * sa + prv * sb).astype(out_ref.dtype)

    @pl.when(j == 0)
    def _():
        rope_to(q_ref)

    @pl.when(j == 1)
    def _():
        rope_to(k_ref)

    @pl.when(j == 2)
    def _():
        vt = lax.dot_general(wvt_ref[...], xb_ref[...], (((1,), (1,)), ((), ())),
                             preferred_element_type=F32)
        hd = 2 * DIFF_HEAD_DIM
        ones = jnp.ones((vt_ref.shape[2] - hd, ATT_BK), vt_ref.dtype)
        for cc in range(vt_ref.shape[0]):
            for h in range(vt_ref.shape[1]):
                vt_ref[cc, h, pl.ds(0, hd), :] = (
                    vt[h * hd:(h + 1) * hd, cc * ATT_BK:(cc + 1) * ATT_BK].astype(vt_ref.dtype))
                vt_ref[cc, h, pl.ds(hd, ones.shape[0]), :] = ones

    @pl.when(j == 3)
    def _():
        u = jnp.dot(xb_ref[...], w_ref[...], preferred_element_type=F32)
        u_ref[...] = u
        ub_ref[...] = u.astype(ub_ref.dtype)


def _inproj(xb2d, w_qku, wvt, rope, seq_len, tm=1024):
    T, D = xb2d.shape
    W = wvt.shape[0]
    hd = 2 * DIFF_HEAD_DIM
    H = W // hd
    hda = hd + ATT_ONES_ROWS
    nseq_blk = seq_len // tm
    grid = (T // tm, 4)
    wmap = lambda i, j: (0, jnp.where(j >= 2, j - 1, j))
    return pl.pallas_call(
        _inproj_kernel,
        grid=grid,
        in_specs=[
            pl.BlockSpec((tm, D), lambda i, j: (i, 0)),
            pl.BlockSpec((D, W), wmap),
            pl.BlockSpec((W, D), lambda i, j: (0, 0), pipeline_mode=pl.Buffered(1)),
            pl.BlockSpec((1, 3, tm, LANES), lambda i, j: (jnp.minimum(j, 1), 0, i % nseq_blk, 0)),
        ],
        out_specs=[
            pl.BlockSpec((tm, W), lambda i, j: (i, 0)),
            pl.BlockSpec((tm, W), lambda i, j: (i, 0)),
            pl.BlockSpec((tm // ATT_BK, H, hda, ATT_BK), lambda i, j: (i, 0, 0, 0)),
            pl.BlockSpec((tm, W), lambda i, j: (i, 0)),
            pl.BlockSpec((tm, W), lambda i, j: (i, 0)),
        ],
        out_shape=[
            jax.ShapeDtypeStruct((T, W), BF16),
            jax.ShapeDtypeStruct((T, W), BF16),
            jax.ShapeDtypeStruct((T // ATT_BK, H, hda, ATT_BK), BF16),
            jax.ShapeDtypeStruct((T, W), F32),
            jax.ShapeDtypeStruct((T, W), BF16),
        ],
        compiler_params=_cparams(("arbitrary", "arbitrary")),
        name="inproj",
    )(xb2d, w_qku, wvt, rope)


def _attn_kernel(lam_ref, sub_ref, q_ref, k_ref, vt_ref, o_ref, q1t_ref, q2t_ref, o1_ref, o2_ref, s_ref,
                 *, lam_init):
    nk = vt_ref.shape[0]
    bq = q_ref.shape[1]
    hd = q_ref.shape[2]

    qt = q_ref[0].astype(F32).T
    row = lax.broadcasted_iota(jnp.int32, qt.shape, 0)
    q1t_ref[...] = jnp.where(row < DIFF_HEAD_DIM, qt, 0.0).astype(BF16)
    q2t_ref[...] = jnp.where(row >= DIFF_HEAD_DIM, qt, 0.0).astype(BF16)
    o1_ref[...] = jnp.zeros_like(o1_ref)
    o2_ref[...] = jnp.zeros_like(o2_ref)

    def scores(c, slot):
        off = pl.multiple_of(c * ATT_BK, ATT_BK)
        kb = k_ref[0, pl.ds(off, ATT_BK), :]
        tops = []
        for br, qt_ref in enumerate((q1t_ref, q2t_ref)):
            s = jnp.dot(kb, qt_ref[...], preferred_element_type=F32)
            s_ref[slot, br] = s
            tops.append(jnp.max(s, axis=0, keepdims=True))
        return tuple(tops)

    def update(c, slot, tops, ms):
        vtb = vt_ref[c]
        new_ms = []
        for br, o_acc in enumerate((o1_ref, o2_ref)):
            m_new = jnp.maximum(ms[br], tops[br])
            alpha = jnp.exp2(ms[br] - m_new)
            p = jnp.exp2(s_ref[slot, br] - m_new).astype(BF16)
            pv = jnp.dot(vtb, p, preferred_element_type=F32)
            o_acc[...] = alpha * o_acc[...] + pv
            new_ms.append(m_new)
        return tuple(new_ms)

    def body(t, carry):
        ms, tops0 = carry
        c0 = 2 * t
        tops1 = scores(c0 + 1, 1)
        ms = update(c0, 0, tops0, ms)
        tops0 = scores(c0 + 2, 0)
        ms = update(c0 + 1, 1, tops1, ms)
        return ms, tops0

    neg = jnp.full((1, bq), NEG_BIG, F32)
    tops0 = scores(0, 0)
    ms, tops0 = lax.fori_loop(0, nk // 2 - 1, body, ((neg, neg), tops0))
    tops1 = scores(nk - 1, 1)
    ms = update(nk - 2, 0, tops0, ms)
    update(nk - 1, 1, tops1, ms)

    lv = lam_ref[...]
    d1 = jnp.sum(lv[0:1] * lv[1:2], axis=1, keepdims=True)
    d2 = jnp.sum(lv[2:3] * lv[3:4], axis=1, keepdims=True)
    lam = jnp.exp(d1) - jnp.exp(d2) + lam_init

    l1 = o1_ref[pl.ds(hd, 1), :]
    l2 = o2_ref[pl.ds(hd, 1), :]
    o = o1_ref[pl.ds(0, hd), :] / l1 - lam * (o2_ref[pl.ds(0, hd), :] / l2)
    msq = jnp.mean(o * o, axis=0, keepdims=True)
    o = o * lax.rsqrt(msq + LN_EPS) * sub_ref[...]
    o = o * (1.0 - lam_init)
    o_ref[0] = o.T.astype(o_ref.dtype)


def _attention(q, k, vt, lam_vecs, sub_col, lam_init):
    B, L, W = q.shape
    hd = 2 * DIFF_HEAD_DIM
    hda = vt.shape[2]
    H = W // hd
    nk = L // ATT_BK
    assert nk % 2 == 0 and nk >= 2
    bq = min(ATT_BQ, L)
    grid = (B, H, L // bq)
    return pl.pallas_call(
        functools.partial(_attn_kernel, lam_init=lam_init),
        grid=grid,
        in_specs=[
            pl.BlockSpec(lam_vecs.shape, lambda b, h, i: (0, 0)),
            pl.BlockSpec(sub_col.shape, lambda b, h, i: (0, 0)),
            pl.BlockSpec((1, bq, hd), lambda b, h, i: (b, i, h)),
            pl.BlockSpec((1, L, hd), lambda b, h, i: (b, 0, h)),
            pl.BlockSpec((nk, None, hda, ATT_BK), lambda b, h, i: (b, h, 0, 0)),
        ],
        out_specs=pl.BlockSpec((1, bq, hd), lambda b, h, i: (b, i, h)),
        out_shape=jax.ShapeDtypeStruct((B, L, W), BF16),
        scratch_shapes=[
            pltpu.VMEM((hd, bq), BF16),
            pltpu.VMEM((hd, bq), BF16),
            pltpu.VMEM((hda, bq), F32),
            pltpu.VMEM((hda, bq), F32),
            pltpu.VMEM((2, 2, ATT_BK, bq), F32),
        ],
        compiler_params=_cparams(("arbitrary", "arbitrary", "arbitrary")),
        name="diff_attn",
    )(lam_vecs, sub_col, q, k, vt)


def _ssm_kernel(*refs, cps):
    T = SSM_CHUNK
    u_refs = refs[:T]
    p_ref, pt_ref, w_ref, r_ref, v_ref, lre_ref, lim_ref, y_ref = refs[T:T + 8]
    ug_ref, yg_ref, m_ref, sre, sim, hfre, hfim, hbre, hbim = refs[T + 8:]
    o = pl.program_id(1)

    @pl.when(o == 0)
    def _():
        _ssm_compute(u_refs, p_ref, w_ref, r_ref, v_ref, lre_ref, lim_ref,
                     ug_ref, yg_ref, m_ref, sre, sim, hfre, hfim, hbre, hbim, cps)

    yo = yg_ref[o]
    hi = yo.astype(BF16)
    lo = (yo - hi.astype(F32)).astype(BF16)
    res = (jnp.dot(hi, pt_ref[...], preferred_element_type=F32)
           + jnp.dot(lo, pt_ref[...], preferred_element_type=F32))
    for t8 in range(SUBLANES):
        y_ref[:, t8, :] = res[:, t8 * LANES:(t8 + 1) * LANES]


def _ssm_compute(u_refs, p_ref, w_ref, r_ref, v_ref, lre_ref, lim_ref,
                 ug_ref, yg_ref, m_ref, sre, sim, hfre, hfim, hbre, hbim, cps):
    gb = ug_ref.shape[0]
    nc = ug_ref.shape[1]
    half = LANES // 2
    n_oct = len(u_refs) // SUBLANES

    for oc in range(n_oct):
        z = jnp.concatenate([u_refs[oc * SUBLANES + t8][...] for t8 in range(SUBLANES)], axis=1)
        zg = jnp.dot(z, p_ref[...], preferred_element_type=F32).astype(BF16)
        for gi in range(gb):
            ug_ref[gi, :, pl.ds(oc * LANES, LANES)] = zg[:, gi * LANES:(gi + 1) * LANES]

    T = len(u_refs)
    C = SSM_GROUP
    for gi in range(gb):
        r = r_ref[gi]
        for s_ in range(T):
            off = (T - 1 - s_) * C
            m_ref[gi, pl.ds(s_ * C, C), :] = r[:, off:off + T * C].astype(BF16)

    for gi in range(gb):
        s = jnp.dot(ug_ref[gi], w_ref[gi], preferred_element_type=F32)
        sre[pl.ds(gi, nc, stride=gb), :] = s[:, :LANES]
        sim[pl.ds(gi, nc, stride=gb), :] = s[:, LANES:]

    zero = jnp.zeros((gb, LANES), F32)
    lre = lre_ref[...]
    lim = lim_ref[...]
    fwd_lane = lax.broadcasted_iota(jnp.int32, (gb, LANES), 1) < half

    def body(j, carry):
        hr, hi = carry
        fresh = (j % cps) == 0
        hr = jnp.where(fresh, 0.0, hr)
        hi = jnp.where(fresh, 0.0, hi)
        jf = pl.multiple_of(j * gb, gb)
        jb = pl.multiple_of((nc - 1 - j) * gb, gb)
        hfre[pl.ds(jf, gb), :] = hr
        hfim[pl.ds(jf, gb), :] = hi
        hbre[pl.ds(jb, gb), :] = hr
        hbim[pl.ds(jb, gb), :] = hi
        sr = jnp.where(fwd_lane, sre[pl.ds(jf, gb), :], sre[pl.ds(jb, gb), :])
        si = jnp.where(fwd_lane, sim[pl.ds(jf, gb), :], sim[pl.ds(jb, gb), :])
        return lre * hr - lim * hi + sr, lre * hi + lim * hr + si

    lax.fori_loop(0, nc, body, (zero, zero))

    fwd_col = lax.broadcasted_iota(jnp.int32, (nc, LANES), 1) < half
    for gi in range(gb):
        hr = jnp.where(fwd_col, hfre[pl.ds(gi, nc, stride=gb), :], hbre[pl.ds(gi, nc, stride=gb), :])
        hi = jnp.where(fwd_col, hfim[pl.ds(gi, nc, stride=gb), :], hbim[pl.ds(gi, nc, stride=gb), :])
        h = jnp.concatenate([hr, hi], axis=1).astype(BF16)
        y = jnp.dot(ug_ref[gi], m_ref[gi], preferred_element_type=F32)
        y = y + jnp.dot(h, v_ref[gi], preferred_element_type=F32)
        for oc in range(n_oct):
            yg_ref[oc, :, pl.ds(gi * LANES, LANES)] = y[:, oc * LANES:(oc + 1) * LANES]


def _lane_regroup_matrix():
    n = SUBLANES * LANES
    i = jnp.arange(n)
    t8, gl, c = i // LANES, (i % LANES) // SSM_GROUP, i % SSM_GROUP
    dest = gl * LANES + t8 * SSM_GROUP + c
    return (dest[:, None] == i[None, :]).astype(BF16)


def _ssm(ub, w_t, r_t, v_t, lre, lim, seq_len):
    Ttok, Wd = ub.shape
    T = SSM_CHUNK
    nc = Ttok // T
    cps = seq_len // T
    gb = SSM_GBLK
    assert gb * SSM_GROUP == LANES and T % SUBLANES == 0 and nc % cps == 0
    n_slab = Wd // LANES
    n_oct = T // SUBLANES
    K = T * SSM_GROUP
    ncols = w_t.shape[2]
    pmat = _lane_regroup_matrix()
    u2 = ub.reshape(nc, T * Wd)
    once = pl.Buffered(1)
    fixed2 = lambda g, o: (0, 0)
    per_slab = lambda g, o: (g, 0, 0)

    def u_spec(t):
        return pl.BlockSpec((nc, LANES), lambda g, o: (0, t * n_slab + g))

    y = pl.pallas_call(
        functools.partial(_ssm_kernel, cps=cps),
        grid=(n_slab, n_oct),
        in_specs=[u_spec(t) for t in range(T)] + [
            pl.BlockSpec(pmat.shape, fixed2, pipeline_mode=once),
            pl.BlockSpec(pmat.shape, fixed2, pipeline_mode=once),
            pl.BlockSpec((gb, K, ncols), per_slab, pipeline_mode=once),
            pl.BlockSpec((gb,) + r_t.shape[1:], per_slab, pipeline_mode=once),
            pl.BlockSpec((gb, ncols, K), per_slab, pipeline_mode=once),
            pl.BlockSpec((gb, LANES), lambda g, o: (g, 0)),
            pl.BlockSpec((gb, LANES), lambda g, o: (g, 0)),
        ],
        out_specs=pl.BlockSpec((nc, SUBLANES, LANES), lambda g, o: (0, o, g)),
        out_shape=jax.ShapeDtypeStruct((nc, T, Wd), F32),
        scratch_shapes=[
            pltpu.VMEM((gb, nc, K), BF16),
            pltpu.VMEM((n_oct, nc, SUBLANES * LANES), F32),
            pltpu.VMEM((gb, K, K), BF16),
        ] + [pltpu.VMEM((nc * gb, LANES), F32)] * 6,
        compiler_params=_cparams(("arbitrary", "arbitrary")),
        name="s5_chunked",
    )(*([u2] * T), pmat, pmat.T, w_t, r_t, v_t, lre, lim)
    return y.reshape(Ttok, Wd)


def _ssm_tables(a_re, a_im, log_dt, b_re, b_im, c_re, c_im):
    hp = lax.Precision.HIGHEST
    T = SSM_CHUNK
    G, P = a_re.shape[1], a_re.shape[2]
    C = b_re.shape[3]
    dt = jnp.exp(log_dt)[..., None]
    zr = a_re * dt
    zi = a_im * dt
    tau = jnp.arange(T + 1, dtype=F32)[:, None, None, None]
    mag = jnp.exp(zr[None] * tau)
    pr = mag * jnp.cos(zi[None] * tau)
    pi = mag * jnp.sin(zi[None] * tau)
    ab_re, ab_im = pr[1], pi[1]
    den = a_re * a_re + a_im * a_im
    nr = ab_re - 1.0
    f_re = ((nr * a_re + ab_im * a_im) / den)[..., None]
    f_im = ((ab_im * a_re - nr * a_im) / den)[..., None]
    bb_re = f_re * b_re - f_im * b_im
    bb_im = f_re * b_im + f_im * b_re

    cp_re = c_re[None] * pr[:, :, :, None, :] - c_im[None] * pi[:, :, :, None, :]
    cp_im = c_re[None] * pi[:, :, :, None, :] + c_im[None] * pr[:, :, :, None, :]
    pb_re = pr[..., None] * bb_re[None] - pi[..., None] * bb_im[None]
    pb_im = pr[..., None] * bb_im[None] + pi[..., None] * bb_re[None]

    cb = (jnp.einsum('tdgcp,dgpk->tdgck', cp_re[:T], bb_re, precision=hp)
          - jnp.einsum('tdgcp,dgpk->tdgck', cp_im[:T], bb_im, precision=hp))

    kern = jnp.concatenate([cb[T - 1:0:-1, 1], (cb[0, 0] + cb[0, 1])[None], cb[1:, 0],
                            jnp.zeros_like(cb[:1, 0])])
    r_t = kern.transpose(1, 3, 0, 2).reshape(G, C, 2 * T * C)

    rev = jnp.arange(T - 1, -1, -1)
    w_cols = [pb_re[rev, 0], pb_re[:T, 1], pb_im[rev, 0], pb_im[:T, 1]]
    w_t = jnp.concatenate([w.transpose(1, 0, 3, 2) for w in w_cols], axis=3)
    w_t = w_t.reshape(G, T * C, 4 * P)

    fwd_pow = jnp.arange(1, T + 1)
    bwd_pow = jnp.arange(T, 0, -1)
    v_rows = [cp_re[fwd_pow, 0], cp_re[bwd_pow, 1], -cp_im[fwd_pow, 0], -cp_im[bwd_pow, 1]]
    v_t = jnp.concatenate([v.transpose(1, 3, 0, 2) for v in v_rows], axis=1)
    v_t = v_t.reshape(G, 4 * P, T * C)

    lre = jnp.concatenate([pr[T, 0], pr[T, 1]], axis=1)
    lim = jnp.concatenate([pi[T, 0], pi[T, 1]], axis=1)
    return w_t.astype(BF16), r_t, v_t.astype(BF16), lre, lim


def _layer_norm(r, g, b):
    mu = jnp.mean(r, axis=-1, keepdims=True)
    d = r - mu
    var = jnp.mean(d * d, axis=-1, keepdims=True)
    return d * lax.rsqrt(var + LN_EPS) * g + b


def _mix_kernel(x_ref, a_ref, y_ref, u_ref, d_ref, wglu_ref, wout_ref, g_ref, b_ref, o_ref, ob_ref,
                *, alpha):
    wa = a_ref.shape[1]
    yy = y_ref[...] + d_ref[...] * u_ref[...]
    z = jax.nn.gelu(yy)
    gate = jax.nn.sigmoid(jnp.dot(z.astype(BF16), wglu_ref[...], preferred_element_type=F32))
    s = (z * gate).astype(BF16)
    mixed = jnp.dot(a_ref[...], wout_ref[pl.ds(0, wa), :], preferred_element_type=F32)
    mixed = mixed + jnp.dot(s, wout_ref[pl.ds(wa, s.shape[1]), :], preferred_element_type=F32)
    r = alpha * x_ref[...] + mixed
    o = _layer_norm(r, g_ref[...], b_ref[...])
    o_ref[...] = o
    ob_ref[...] = o.astype(BF16)


def _mix(x2d, attn, y, u, d, wglu, wout, g, b, alpha, tm=256):
    T, D = x2d.shape
    W = attn.shape[1]
    row = lambda i: (i, 0)
    fixed = lambda i: (0, 0)
    return pl.pallas_call(
        functools.partial(_mix_kernel, alpha=alpha),
        grid=(T // tm,),
        in_specs=[
            pl.BlockSpec((tm, D), row),
            pl.BlockSpec((tm, W), row),
            pl.BlockSpec((tm, W), row),
            pl.BlockSpec((tm, W), row),
            pl.BlockSpec((1, W), fixed),
            pl.BlockSpec(wglu.shape, fixed),
            pl.BlockSpec(wout.shape, fixed),
            pl.BlockSpec((1, D), fixed),
            pl.BlockSpec((1, D), fixed),
        ],
        out_specs=[pl.BlockSpec((tm, D), row), pl.BlockSpec((tm, D), row)],
        out_shape=[jax.ShapeDtypeStruct((T, D), F32), jax.ShapeDtypeStruct((T, D), BF16)],
        compiler_params=_cparams(("arbitrary",)),
        name="glu_outproj_ln",
    )(x2d, attn, y, u, d, wglu, wout, g, b)


def _ffn_up_kernel(x_ref, wg_ref, wu_ref, h_ref):
    x = x_ref[...]
    g = jnp.dot(x, wg_ref[...], preferred_element_type=F32)
    up = jnp.dot(x, wu_ref[...], preferred_element_type=F32)
    h_ref[...] = (jax.nn.silu(g) * up).astype(h_ref.dtype)


def _ffn_up(xb, wg, wu, tm=1024, tf=512):
    T, D = xb.shape
    F = wg.shape[1]
    return pl.pallas_call(
        _ffn_up_kernel,
        grid=(T // tm, F // tf),
        in_specs=[
            pl.BlockSpec((tm, D), lambda i, j: (i, 0)),
            pl.BlockSpec((D, tf), lambda i, j: (0, j)),
            pl.BlockSpec((D, tf), lambda i, j: (0, j)),
        ],
        out_specs=pl.BlockSpec((tm, tf), lambda i, j: (i, j)),
        out_shape=jax.ShapeDtypeStruct((T, F), BF16),
        compiler_params=_cparams(("arbitrary", "arbitrary")),
        name="ffn_up",
    )(xb, wg, wu)


def _ffn_down_kernel(h_ref, wd_ref, x_ref, g_ref, b_ref, o_ref, *, alpha):
    down = jnp.dot(h_ref[...], wd_ref[...], preferred_element_type=F32)
    o_ref[...] = _layer_norm(alpha * x_ref[...] + down, g_ref[...], b_ref[...])


def _ffn_down(h, wd, x1, g, b, alpha, tm=256):
    T, F = h.shape
    D = wd.shape[1]
    return pl.pallas_call(
        functools.partial(_ffn_down_kernel, alpha=alpha),
        grid=(T // tm,),
        in_specs=[
            pl.BlockSpec((tm, F), lambda i: (i, 0)),
            pl.BlockSpec((F, D), lambda i: (0, 0), pipeline_mode=pl.Buffered(1)),
            pl.BlockSpec((tm, D), lambda i: (i, 0)),
            pl.BlockSpec((1, D), lambda i: (0, 0)),
            pl.BlockSpec((1, D), lambda i: (0, 0)),
        ],
        out_specs=pl.BlockSpec((tm, D), lambda i: (i, 0)),
        out_shape=jax.ShapeDtypeStruct((T, D), F32),
        compiler_params=_cparams(("arbitrary",)),
        name="ffn_down_ln",
    )(h, wd, x1, g, b)


def _rope_tables(L):
    half = ROT_DIM // 2
    inv = ROPE_THETA ** (-jnp.arange(0, ROT_DIM, 2, dtype=F32) / ROT_DIM)
    ang = jnp.arange(L, dtype=F32)[:, None] * inv[None, :]
    cos, sin = jnp.cos(ang), jnp.sin(ang)
    ones = jnp.ones((L, DIFF_HEAD_DIM - ROT_DIM), F32)
    zeros_h = jnp.zeros((L, half), F32)
    zeros_r = jnp.zeros((L, DIFF_HEAD_DIM - ROT_DIM), F32)
    c = jnp.concatenate([cos, cos, ones], axis=1)
    sa = jnp.concatenate([-sin, zeros_h, zeros_r], axis=1)
    sb = jnp.concatenate([zeros_h, sin, zeros_r], axis=1)
    tab = jnp.stack([jnp.tile(t, (1, LANES // DIFF_HEAD_DIM)) for t in (c, sa, sb)])
    return jnp.stack([tab * (DIFF_HEAD_DIM ** -0.5 * math.log2(math.e)), tab])


def _prepare_layer(p):
    (w_in, lq1, lk1, lq2, lk2, subln_w, a_re, a_im, log_dt, b_re, b_im, c_re, c_im, ssm_d, w_glu,
     w_out, ln1_g, ln1_b, w_gate, w_up, w_down, ln2_g, ln2_b) = p
    W = w_glu.shape[0]
    D = w_in.shape[0]
    row = lambda t, n: t.reshape(1, n).astype(F32)
    return dict(
        w_qku=jnp.concatenate([w_in[:, :2 * W], w_in[:, 3 * W:]], axis=1).astype(BF16),
        wvt=w_in[:, 2 * W:3 * W].T.astype(BF16),
        lam_vecs=jnp.stack([lq1, lk1, lq2, lk2]).astype(F32),
        sub_col=subln_w.astype(F32).reshape(-1, 1),
        ssm=_ssm_tables(a_re, a_im, log_dt, b_re, b_im, c_re, c_im),
        ssm_d=row(ssm_d, W), w_glu=w_glu.astype(BF16), w_out=w_out.astype(BF16),
        ln1_g=row(ln1_g, D), ln1_b=row(ln1_b, D),
        w_gate=w_gate.astype(BF16), w_up=w_up.astype(BF16), w_down=w_down.astype(BF16),
        ln2_g=row(ln2_g, D), ln2_b=row(ln2_b, D),
    )


def _encoder_layer(x, layer_idx, depth, pp):
    B, L, D = x.shape
    T = B * L
    W = pp["w_glu"].shape[0]
    alpha = (2 * depth) ** 0.25
    lam_init = 0.8 - 0.6 * math.exp(-0.3 * layer_idx)

    x2d = x.reshape(T, D)
    q, k, vt, u, ub = _inproj(x2d.astype(BF16), pp["w_qku"], pp["wvt"], _rope_tables(L), L)
    attn = _attention(q.reshape(B, L, W), k.reshape(B, L, W), vt, pp["lam_vecs"], pp["sub_col"], lam_init)
    y = _ssm(ub, *pp["ssm"], L)
    x1, x1b = _mix(x2d, attn.reshape(T, W), y, u, pp["ssm_d"], pp["w_glu"], pp["w_out"],
                   pp["ln1_g"], pp["ln1_b"], alpha)
    h = _ffn_up(x1b, pp["w_gate"], pp["w_up"])
    out = _ffn_down(h, pp["w_down"], x1, pp["ln2_g"], pp["ln2_b"], alpha)
    return out.reshape(B, L, D)


def kernel(x_prompt, x_sample, w_in, lambda_q1, lambda_k1, lambda_q2, lambda_k2, subln_w, ssm_a_re, ssm_a_im, ssm_log_dt, ssm_b_re, ssm_b_im, ssm_c_re, ssm_c_im, ssm_d, w_glu, w_out, ln1_g, ln1_b, w_gate, w_up, w_down, ln2_g, ln2_b):
    params = (w_in, lambda_q1, lambda_k1, lambda_q2, lambda_k2, subln_w,
              ssm_a_re, ssm_a_im, ssm_log_dt, ssm_b_re, ssm_b_im, ssm_c_re, ssm_c_im, ssm_d, w_glu,
              w_out, ln1_g, ln1_b, w_gate, w_up, w_down, ln2_g, ln2_b)
    depth = w_in.shape[0]
    y_prompt, y_sample = x_prompt, x_sample
    for layer in range(depth):
        lp = _prepare_layer(tuple(t[layer] for t in params))
        y_prompt = _encoder_layer(y_prompt, layer, depth, lp)
        y_sample = _encoder_layer(y_sample, layer, depth, lp)
    return (y_prompt, y_sample)
```

```python
import functools
import math

import jax
import jax.numpy as jnp
from jax import lax
from jax.experimental import pallas as pl
from jax.experimental.pallas import tpu as pltpu

F32 = jnp.float32
BF16 = jnp.bfloat16

DIFF_HEAD_DIM = 64
ROT_DIM = DIFF_HEAD_DIM // 4
ROPE_THETA = 500000.0
SSM_GROUP = 16
SSM_STATE = 64
LN_EPS = 1e-5

LANES = 128
SUBLANES = 8
VMEM_LIMIT_BYTES = 56 * 1024 * 1024

SSM_CHUNK = 32
SSM_GBLK = SUBLANES
ATT_BK = 512
ATT_BQ = 512
ATT_ONES_ROWS = 16
NEG_BIG = -1e30


def _cparams(sem):
    return pltpu.CompilerParams(dimension_semantics=sem, vmem_limit_bytes=VMEM_LIMIT_BYTES)


def _inproj_kernel(xb_ref, w_ref, wvt_ref, rope_ref, q_ref, k_ref, vt_ref, u_ref, ub_ref):
    j = pl.program_id(1)

    def rope_to(out_ref):
        acc = jnp.dot(xb_ref[...], w_ref[...], preferred_element_type=F32)
        c = rope_ref[0, 0]
        sa = rope_ref[0, 1]
        sb = rope_ref[0, 2]
        for cb in range(acc.shape[1] // LANES):
            xs = acc[:, cb * LANES:(cb + 1) * LANES]
            nxt = pltpu.roll(xs, LANES - ROT_DIM // 2, axis=1)
            prv = pltpu.roll(xs, ROT_DIM // 2, axis=1)
            out_ref[cb] = (xs * c + nxt * sa + prv * sb).astype(out_ref.dtype)

    @pl.when(j == 0)
    def _():
        rope_to(q_ref)

    @pl.when(j == 1)
    def _():
        rope_to(k_ref)

    @pl.when(j == 2)
    def _():
        vt = lax.dot_general(wvt_ref[...], xb_ref[...], (((1,), (1,)), ((), ())),
                             preferred_element_type=F32)
        hd = 2 * DIFF_HEAD_DIM
        ones = jnp.ones((vt_ref.shape[2] - hd, ATT_BK), vt_ref.dtype)
        for cc in range(vt_ref.shape[0]):
            for h in range(vt_ref.shape[1]):
                vt_ref[cc, h, pl.ds(0, hd), :] = (
                    vt[h * hd:(h + 1) * hd, cc * ATT_BK:(cc + 1) * ATT_BK].astype(vt_ref.dtype))
                vt_ref[cc, h, pl.ds(hd, ones.shape[0]), :] = ones

    @pl.when(j == 3)
    def _():
        u = jnp.dot(xb_ref[...], w_ref[...], preferred_element_type=F32)
        u_ref[...] = u
        ub_ref[...] = u.astype(ub_ref.dtype)


def _inproj(xb2d, w_qku, wvt, rope, seq_len, tm=1024):
    T, D = xb2d.shape
    W = wvt.shape[0]
    hd = 2 * DIFF_HEAD_DIM
    H = W // hd
    hda = hd + ATT_ONES_ROWS
    nseq_blk = seq_len // tm
    grid = (T // tm, 4)
    wmap = lambda i, j: (0, jnp.where(j >= 2, j - 1, j))
    return pl.pallas_call(
        _inproj_kernel,
        grid=grid,
        in_specs=[
            pl.BlockSpec((tm, D), lambda i, j: (i, 0)),
            pl.BlockSpec((D, W), wmap),
            pl.BlockSpec((W, D), lambda i, j: (0, 0), pipeline_mode=pl.Buffered(1)),
            pl.BlockSpec((1, 3, tm, LANES), lambda i, j: (jnp.minimum(j, 1), 0, i % nseq_blk, 0)),
        ],
        out_specs=[
            pl.BlockSpec((H, tm, hd), lambda i, j: (0, i, 0)),
            pl.BlockSpec((H, tm, hd), lambda i, j: (0, i, 0)),
            pl.BlockSpec((tm // ATT_BK, H, hda, ATT_BK), lambda i, j: (i, 0, 0, 0)),
            pl.BlockSpec((tm, W), lambda i, j: (i, 0)),
            pl.BlockSpec((tm, W), lambda i, j: (i, 0)),
        ],
        out_shape=[
            jax.ShapeDtypeStruct((H, T, hd), BF16),
            jax.ShapeDtypeStruct((H, T, hd), BF16),
            jax.ShapeDtypeStruct((T // ATT_BK, H, hda, ATT_BK), BF16),
            jax.ShapeDtypeStruct((T, W), F32),
            jax.ShapeDtypeStruct((T, W), BF16),
        ],
        compiler_params=_cparams(("arbitrary", "arbitrary")),
        name="inproj",
    )(xb2d, w_qku, wvt, rope)


def _attn_kernel(lam_ref, sub_ref, q_ref, k_ref, vt_ref, o_ref, q1t_ref, q2t_ref, o1_ref, o2_ref, s_ref,
                 *, lam_init):
    nk = vt_ref.shape[0]
    bq = q_ref.shape[0]
    hd = q_ref.shape[1]

    qt = q_ref[...].astype(F32).T
    row = lax.broadcasted_iota(jnp.int32, qt.shape, 0)
    q1t_ref[...] = jnp.where(row < DIFF_HEAD_DIM, qt, 0.0).astype(BF16)
    q2t_ref[...] = jnp.where(row >= DIFF_HEAD_DIM, qt, 0.0).astype(BF16)
    o1_ref[...] = jnp.zeros_like(o1_ref)
    o2_ref[...] = jnp.zeros_like(o2_ref)

    def scores(c, slot):
        off = pl.multiple_of(c * ATT_BK, ATT_BK)
        kb = k_ref[pl.ds(off, ATT_BK), :]
        tops = []
        for br, qt_ref in enumerate((q1t_ref, q2t_ref)):
            s = jnp.dot(kb, qt_ref[...], preferred_element_type=F32)
            s_ref[slot, br] = s
            tops.append(jnp.max(s, axis=0, keepdims=True))
        return tuple(tops)

    def update(c, slot, tops, ms):
        vtb = vt_ref[c]
        new_ms = []
        for br, o_acc in enumerate((o1_ref, o2_ref)):
            m_new = jnp.maximum(ms[br], tops[br])
            alpha = jnp.exp2(ms[br] - m_new)
            p = jnp.exp2(s_ref[slot, br] - m_new).astype(BF16)
            pv = jnp.dot(vtb, p, preferred_element_type=F32)
            o_acc[...] = alpha * o_acc[...] + pv
            new_ms.append(m_new)
        return tuple(new_ms)

    def body(t, carry):
        ms, tops0 = carry
        c0 = 2 * t
        tops1 = scores(c0 + 1, 1)
        ms = update(c0, 0, tops0, ms)
        tops0 = scores(c0 + 2, 0)
        ms = update(c0 + 1, 1, tops1, ms)
        return ms, tops0

    neg = jnp.full((1, bq), NEG_BIG, F32)
    tops0 = scores(0, 0)
    ms, tops0 = lax.fori_loop(0, nk // 2 - 1, body, ((neg, neg), tops0))
    tops1 = scores(nk - 1, 1)
    ms = update(nk - 2, 0, tops0, ms)
    update(nk - 1, 1, tops1, ms)

    lv = lam_ref[...]
    d1 = jnp.sum(lv[0:1] * lv[1:2], axis=1, keepdims=True)
    d2 = jnp.sum(lv[2:3] * lv[3:4], axis=1, keepdims=True)
    lam = jnp.exp(d1) - jnp.exp(d2) + lam_init

    l1 = o1_ref[pl.ds(hd, 1), :]
    l2 = o2_ref[pl.ds(hd, 1), :]
    o = o1_ref[pl.ds(0, hd), :] / l1 - lam * (o2_ref[pl.ds(0, hd), :] / l2)
    msq = jnp.mean(o * o, axis=0, keepdims=True)
    o = o * lax.rsqrt(msq + LN_EPS) * sub_ref[...]
    o = o * (1.0 - lam_init)
    o_ref[...] = o.T.astype(o_ref.dtype)


def _attention(q, k, vt, lam_vecs, sub_col, lam_init, seq_len):
    H, T, hd = q.shape
    L = seq_len
    B = T // L
    hda = vt.shape[2]
    nk = L // ATT_BK
    assert nk % 2 == 0 and nk >= 2
    bq = min(ATT_BQ, L)
    nq = L // bq
    grid = (B, H, nq)
    return pl.pallas_call(
        functools.partial(_attn_kernel, lam_init=lam_init),
        grid=grid,
        in_specs=[
            pl.BlockSpec(lam_vecs.shape, lambda b, h, i: (0, 0)),
            pl.BlockSpec(sub_col.shape, lambda b, h, i: (0, 0)),
            pl.BlockSpec((None, bq, hd), lambda b, h, i: (h, b * nq + i, 0)),
            pl.BlockSpec((None, L, hd), lambda b, h, i: (h, b, 0)),
            pl.BlockSpec((nk, None, hda, ATT_BK), lambda b, h, i: (b, h, 0, 0)),
        ],
        out_specs=pl.BlockSpec((None, bq, hd), lambda b, h, i: (h, b * nq + i, 0)),
        out_shape=jax.ShapeDtypeStruct((H, T, hd), BF16),
        scratch_shapes=[
            pltpu.VMEM((hd, bq), BF16),
            pltpu.VMEM((hd, bq), BF16),
            pltpu.VMEM((hda, bq), F32),
            pltpu.VMEM((hda, bq), F32),
            pltpu.VMEM((2, 2, ATT_BK, bq), F32),
        ],
        compiler_params=_cparams(("arbitrary", "arbitrary", "arbitrary")),
        name="diff_attn",
    )(lam_vecs, sub_col, q, k, vt)


def _ssm_kernel(*refs, cps):
    T = SSM_CHUNK
    u_refs = refs[:T]
    p_ref, pt_ref, w_ref, r_ref, v_ref, lre_ref, lim_ref, y_ref = refs[T:T + 8]
    ug_ref, yg_ref, m_ref, sre, sim, hfre, hfim, hbre, hbim = refs[T + 8:]
    o = pl.program_id(1)

    @pl.when(o == 0)
    def _():
        _ssm_compute(u_refs, p_ref, w_ref, r_ref, v_ref, lre_ref, lim_ref,
                     ug_ref, yg_ref, m_ref, sre, sim, hfre, hfim, hbre, hbim, cps)

    yo = yg_ref[o]
    hi = yo.astype(BF16)
    lo = (yo - hi.astype(F32)).astype(BF16)
    res = (jnp.dot(hi, pt_ref[...], preferred_element_type=F32)
           + jnp.dot(lo, pt_ref[...], preferred_element_type=F32))
    for t8 in range(SUBLANES):
        y_ref[:, t8, :] = res[:, t8 * LANES:(t8 + 1) * LANES]


def _ssm_compute(u_refs, p_ref, w_ref, r_ref, v_ref, lre_ref, lim_ref,
                 ug_ref, yg_ref, m_ref, sre, sim, hfre, hfim, hbre, hbim, cps):
    gb = ug_ref.shape[0]
    nc = ug_ref.shape[1]
    half = LANES // 2
    n_oct = len(u_refs) // SUBLANES

    for oc in range(n_oct):
        z = jnp.concatenate([u_refs[oc * SUBLANES + t8][...] for t8 in range(SUBLANES)], axis=1)
        zg = jnp.dot(z, p_ref[...], preferred_element_type=F32).astype(BF16)
        for gi in range(gb):
            ug_ref[gi, :, pl.ds(oc * LANES, LANES)] = zg[:, gi * LANES:(gi + 1) * LANES]

    T = len(u_refs)
    C = SSM_GROUP
    for gi in range(gb):
        r = r_ref[gi]
        for s_ in range(T):
            off = (T - 1 - s_) * C
            m_ref[gi, pl.ds(s_ * C, C), :] = r[:, off:off + T * C].astype(BF16)

    for gi in range(gb):
        s = jnp.dot(ug_ref[gi], w_ref[gi], preferred_element_type=F32)
        sre[pl.ds(gi, nc, stride=gb), :] = s[:, :LANES]
        sim[pl.ds(gi, nc, stride=gb), :] = s[:, LANES:]

    zero = jnp.zeros((gb, LANES), F32)
    lre = lre_ref[...]
    lim = lim_ref[...]
    fwd_lane = lax.broadcasted_iota(jnp.int32, (gb, LANES), 1) < half

    def body(j, carry):
        hr, hi = carry
        fresh = (j % cps) == 0
        hr = jnp.where(fresh, 0.0, hr)
        hi = jnp.where(fresh, 0.0, hi)
        jf = pl.multiple_of(j * gb, gb)
        jb = pl.multiple_of((nc - 1 - j) * gb, gb)
        hfre[pl.ds(jf, gb), :] = hr
        hfim[pl.ds(jf, gb), :] = hi
        hbre[pl.ds(jb, gb), :] = hr
        hbim[pl.ds(jb, gb), :] = hi
        sr = jnp.where(fwd_lane, sre[pl.ds(jf, gb), :], sre[pl.ds(jb, gb), :])
        si = jnp.where(fwd_lane, sim[pl.ds(jf, gb), :], sim[pl.ds(jb, gb), :])
        return lre * hr - lim * hi + sr, lre * hi + lim * hr + si

    lax.fori_loop(0, nc, body, (zero, zero))

    fwd_col = lax.broadcasted_iota(jnp.int32, (nc, LANES), 1) < half
    for gi in range(gb):
        hr = jnp.where(fwd_col, hfre[pl.ds(gi, nc, stride=gb), :], hbre[pl.ds(gi, nc, stride=gb), :])
        hi = jnp.where(fwd_col, hfim[pl.ds(gi, nc, stride=gb), :], hbim[pl.ds(gi, nc, stride=gb), :])
        h = jnp.concatenate([hr, hi], axis=1).astype(BF16)
        y = jnp.dot(ug_ref[gi], m_ref[gi], preferred_element_type=F32)
        y = y + jnp.dot(h, v_ref[gi], preferred_element_type=F32)
        for oc in range(n_oct):
            yg_ref[oc, :, pl.ds(gi * LANES, LANES)] = y[:, oc * LANES:(oc + 1) * LANES]


def _lane_regroup_matrix():
    n = SUBLANES * LANES
    i = jnp.arange(n)
    t8, gl, c = i // LANES, (i % LANES) // SSM_GROUP, i % SSM_GROUP
    dest = gl * LANES + t8 * SSM_GROUP + c
    return (dest[:, None] == i[None, :]).astype(BF16)


def _ssm(ub, w_t, r_t, v_t, lre, lim, seq_len):
    Ttok, Wd = ub.shape
    T = SSM_CHUNK
    nc = Ttok // T
    cps = seq_len // T
    gb = SSM_GBLK
    assert gb * SSM_GROUP == LANES and T % SUBLANES == 0 and nc % cps == 0
    n_slab = Wd // LANES
    n_oct = T // SUBLANES
    K = T * SSM_GROUP
    ncols = w_t.shape[2]
    pmat = _lane_regroup_matrix()
    u2 = ub.reshape(nc, T * Wd)
    once = pl.Buffered(1)
    fixed2 = lambda g, o: (0, 0)
    per_slab = lambda g, o: (g, 0, 0)

    def u_spec(t):
        return pl.BlockSpec((nc, LANES), lambda g, o: (0, t * n_slab + g))

    y = pl.pallas_call(
        functools.partial(_ssm_kernel, cps=cps),
        grid=(n_slab, n_oct),
        in_specs=[u_spec(t) for t in range(T)] + [
            pl.BlockSpec(pmat.shape, fixed2, pipeline_mode=once),
            pl.BlockSpec(pmat.shape, fixed2, pipeline_mode=once),
            pl.BlockSpec((gb, K, ncols), per_slab, pipeline_mode=once),
            pl.BlockSpec((gb,) + r_t.shape[1:], per_slab, pipeline_mode=once),
            pl.BlockSpec((gb, ncols, K), per_slab, pipeline_mode=once),
            pl.BlockSpec((gb, LANES), lambda g, o: (g, 0)),
            pl.BlockSpec((gb, LANES), lambda g, o: (g, 0)),
        ],
        out_specs=pl.BlockSpec((nc, SUBLANES, LANES), lambda g, o: (0, o, g)),
        out_shape=jax.ShapeDtypeStruct((nc, T, Wd), F32),
        scratch_shapes=[
            pltpu.VMEM((gb, nc, K), BF16),
            pltpu.VMEM((n_oct, nc, SUBLANES * LANES), F32),
            pltpu.VMEM((gb, K, K), BF16),
        ] + [pltpu.VMEM((nc * gb, LANES), F32)] * 6,
        compiler_params=_cparams(("arbitrary", "arbitrary")),
        name="s5_chunked",
    )(*([u2] * T), pmat, pmat.T, w_t, r_t, v_t, lre, lim)
    return y.reshape(Ttok, Wd)


def _ssm_tables(a_re, a_im, log_dt, b_re, b_im, c_re, c_im):
    hp = lax.Precision.HIGHEST
    T = SSM_CHUNK
    G, P = a_re.shape[1], a_re.shape[2]
    C = b_re.shape[3]
    dt = jnp.exp(log_dt)[..., None]
    zr = a_re * dt
    zi = a_im * dt
    tau = jnp.arange(T + 1, dtype=F32)[:, None, None, None]
    mag = jnp.exp(zr[None] * tau)
    pr = mag * jnp.cos(zi[None] * tau)
    pi = mag * jnp.sin(zi[None] * tau)
    ab_re, ab_im = pr[1], pi[1]
    den = a_re * a_re + a_im * a_im
    nr = ab_re - 1.0
    f_re = ((nr * a_re + ab_im * a_im) / den)[..., None]
    f_im = ((ab_im * a_re - nr * a_im) / den)[..., None]
    bb_re = f_re * b_re - f_im * b_im
    bb_im = f_re * b_im + f_im * b_re

    cp_re = c_re[None] * pr[:, :, :, None, :] - c_im[None] * pi[:, :, :, None, :]
    cp_im = c_re[None] * pi[:, :, :, None, :] + c_im[None] * pr[:, :, :, None, :]
    pb_re = pr[..., None] * bb_re[None] - pi[..., None] * bb_im[None]
    pb_im = pr[..., None] * bb_im[None] + pi[..., None] * bb_re[None]

    cb = jnp.einsum('tdgcq,dgqk->tdgck', jnp.concatenate([cp_re[:T], -cp_im[:T]], axis=-1),
                    jnp.concatenate([bb_re, bb_im], axis=2), precision=lax.Precision.HIGH)

    kern = jnp.concatenate([cb[T - 1:0:-1, 1], (cb[0, 0] + cb[0, 1])[None], cb[1:, 0],
                            jnp.zeros_like(cb[:1, 0])])
    r_t = kern.transpose(1, 3, 0, 2).reshape(G, C, 2 * T * C)

    rev = jnp.arange(T - 1, -1, -1)
    w_cols = [pb_re[rev, 0], pb_re[:T, 1], pb_im[rev, 0], pb_im[:T, 1]]
    w_t = jnp.concatenate([w.transpose(1, 0, 3, 2) for w in w_cols], axis=3)
    w_t = w_t.reshape(G, T * C, 4 * P)

    fwd_pow = jnp.arange(1, T + 1)
    bwd_pow = jnp.arange(T, 0, -1)
    v_rows = [cp_re[fwd_pow, 0], cp_re[bwd_pow, 1], -cp_im[fwd_pow, 0], -cp_im[bwd_pow, 1]]
    v_t = jnp.concatenate([v.transpose(1, 3, 0, 2) for v in v_rows], axis=1)
    v_t = v_t.reshape(G, 4 * P, T * C)

    lre = jnp.concatenate([pr[T, 0], pr[T, 1]], axis=1)
    lim = jnp.concatenate([pi[T, 0], pi[T, 1]], axis=1)
    return w_t.astype(BF16), r_t, v_t.astype(BF16), lre, lim


def _layer_norm(r, g, b):
    mu = jnp.mean(r, axis=-1, keepdims=True)
    d = r - mu
    var = jnp.mean(d * d, axis=-1, keepdims=True)
    return d * lax.rsqrt(var + LN_EPS) * g + b


def _mix_kernel(x_ref, a_ref, y_ref, u_ref, d_ref, wglu_ref, wout_ref, g_ref, b_ref, o_ref, ob_ref,
                *, alpha):
    n_heads, tm, hd = a_ref.shape
    wa = n_heads * hd
    for half in range(2):
        rows = pl.ds(half * (tm // 2), tm // 2)
        yy = y_ref[rows, :] + d_ref[...] * u_ref[rows, :]
        z = jax.nn.gelu(yy)
        gate = jax.nn.sigmoid(jnp.dot(z.astype(BF16), wglu_ref[...], preferred_element_type=F32))
        s = (z * gate).astype(BF16)
        a = jnp.concatenate([a_ref[h, rows, :] for h in range(n_heads)], axis=1)
        mixed = jnp.dot(a, wout_ref[pl.ds(0, wa), :], preferred_element_type=F32)
        mixed = mixed + jnp.dot(s, wout_ref[pl.ds(wa, s.shape[1]), :], preferred_element_type=F32)
        r = alpha * x_ref[rows, :] + mixed
        o = _layer_norm(r, g_ref[...], b_ref[...])
        o_ref[rows, :] = o
        ob_ref[rows, :] = o.astype(BF16)


def _mix(x2d, attn, y, u, d, wglu, wout, g, b, alpha, tm=512):
    T, D = x2d.shape
    W = y.shape[1]
    row = lambda i: (i, 0)
    fixed = lambda i: (0, 0)
    once = pl.Buffered(1)
    return pl.pallas_call(
        functools.partial(_mix_kernel, alpha=alpha),
        grid=(T // tm,),
        in_specs=[
            pl.BlockSpec((tm, D), row),
            pl.BlockSpec((attn.shape[0], tm, attn.shape[2]), lambda i: (0, i, 0)),
            pl.BlockSpec((tm, W), row),
            pl.BlockSpec((tm, W), row),
            pl.BlockSpec((1, W), fixed),
            pl.BlockSpec(wglu.shape, fixed, pipeline_mode=once),
            pl.BlockSpec(wout.shape, fixed, pipeline_mode=once),
            pl.BlockSpec((1, D), fixed),
            pl.BlockSpec((1, D), fixed),
        ],
        out_specs=[pl.BlockSpec((tm, D), row), pl.BlockSpec((tm, D), row)],
        out_shape=[jax.ShapeDtypeStruct((T, D), F32), jax.ShapeDtypeStruct((T, D), BF16)],
        compiler_params=_cparams(("arbitrary",)),
        name="glu_outproj_ln",
    )(x2d, attn, y, u, d, wglu, wout, g, b)


def _ffn_up_kernel(x_ref, wg_ref, wu_ref, h_ref):
    x = x_ref[...]
    g = jnp.dot(x, wg_ref[...], preferred_element_type=F32)
    up = jnp.dot(x, wu_ref[...], preferred_element_type=F32)
    h_ref[...] = (jax.nn.silu(g) * up).astype(h_ref.dtype)


def _ffn_up(xb, wg, wu, tm=1024, tf=512):
    T, D = xb.shape
    F = wg.shape[1]
    return pl.pallas_call(
        _ffn_up_kernel,
        grid=(T // tm, F // tf),
        in_specs=[
            pl.BlockSpec((tm, D), lambda i, j: (i, 0)),
            pl.BlockSpec((D, tf), lambda i, j: (0, j)),
            pl.BlockSpec((D, tf), lambda i, j: (0, j)),
        ],
        out_specs=pl.BlockSpec((tm, tf), lambda i, j: (i, j)),
        out_shape=jax.ShapeDtypeStruct((T, F), BF16),
        compiler_params=_cparams(("arbitrary", "arbitrary")),
        name="ffn_up",
    )(xb, wg, wu)


def _ffn_down_kernel(h_ref, wd_ref, x_ref, g_ref, b_ref, o_ref, *, alpha):
    down = jnp.dot(h_ref[...], wd_ref[...], preferred_element_type=F32)
    o_ref[...] = _layer_norm(alpha * x_ref[...] + down, g_ref[...], b_ref[...])


def _ffn_down(h, wd, x1, g, b, alpha, tm=256):
    T, F = h.shape
    D = wd.shape[1]
    return pl.pallas_call(
        functools.partial(_ffn_down_kernel, alpha=alpha),
        grid=(T // tm,),
        in_specs=[
            pl.BlockSpec((tm, F), lambda i: (i, 0)),
            pl.BlockSpec((F, D), lambda i: (0, 0), pipeline_mode=pl.Buffered(1)),
            pl.BlockSpec((tm, D), lambda i: (i, 0)),
            pl.BlockSpec((1, D), lambda i: (0, 0)),
            pl.BlockSpec((1, D), lambda i: (0, 0)),
        ],
        out_specs=pl.BlockSpec((tm, D), lambda i: (i, 0)),
        out_shape=jax.ShapeDtypeStruct((T, D), F32),
        compiler_params=_cparams(("arbitrary",)),
        name="ffn_down_ln",
    )(h, wd, x1, g, b)


def _rope_tables(L):
    half = ROT_DIM // 2
    inv = ROPE_THETA ** (-jnp.arange(0, ROT_DIM, 2, dtype=F32) / ROT_DIM)
    ang = jnp.arange(L, dtype=F32)[:, None] * inv[None, :]
    cos, sin = jnp.cos(ang), jnp.sin(ang)
    ones = jnp.ones((L, DIFF_HEAD_DIM - ROT_DIM), F32)
    zeros_h = jnp.zeros((L, half), F32)
    zeros_r = jnp.zeros((L, DIFF_HEAD_DIM - ROT_DIM), F32)
    c = jnp.concatenate([cos, cos, ones], axis=1)
    sa = jnp.concatenate([-sin, zeros_h, zeros_r], axis=1)
    sb = jnp.concatenate([zeros_h, sin, zeros_r], axis=1)
    tab = jnp.stack([jnp.tile(t, (1, LANES // DIFF_HEAD_DIM)) for t in (c, sa, sb)])
    return jnp.stack([tab * (DIFF_HEAD_DIM ** -0.5 * math.log2(math.e)), tab])


def _prepare_layer(p):
    (w_in, lq1, lk1, lq2, lk2, subln_w, a_re, a_im, log_dt, b_re, b_im, c_re, c_im, ssm_d, w_glu,
     w_out, ln1_g, ln1_b, w_gate, w_up, w_down, ln2_g, ln2_b) = p
    W = w_glu.shape[0]
    D = w_in.shape[0]
    row = lambda t, n: t.reshape(1, n).astype(F32)
    return dict(
        w_qku=jnp.concatenate([w_in[:, :2 * W], w_in[:, 3 * W:]], axis=1).astype(BF16),
        wvt=w_in[:, 2 * W:3 * W].T.astype(BF16),
        lam_vecs=jnp.stack([lq1, lk1, lq2, lk2]).astype(F32),
        sub_col=subln_w.astype(F32).reshape(-1, 1),
        ssm=_ssm_tables(a_re, a_im, log_dt, b_re, b_im, c_re, c_im),
        ssm_d=row(ssm_d, W), w_glu=w_glu.astype(BF16), w_out=w_out.astype(BF16),
        ln1_g=row(ln1_g, D), ln1_b=row(ln1_b, D),
        w_gate=w_gate.astype(BF16), w_up=w_up.astype(BF16), w_down=w_down.astype(BF16),
        ln2_g=row(ln2_g, D), ln2_b=row(ln2_b, D),
    )


def _encoder_layer(x, layer_idx, depth, pp):
    B, L, D = x.shape
    T = B * L
    W = pp["w_glu"].shape[0]
    alpha = (2 * depth) ** 0.25
    lam_init = 0.8 - 0.6 * math.exp(-0.3 * layer_idx)

    x2d = x.reshape(T, D)
    q, k, vt, u, ub = _inproj(x2d.astype(BF16), pp["w_qku"], pp["wvt"], _rope_tables(L), L)
    attn = _attention(q, k, vt, pp["lam_vecs"], pp["sub_col"], lam_init, L)
    y = _ssm(ub, *pp["ssm"], L)
    x1, x1b = _mix(x2d, attn, y, u, pp["ssm_d"], pp["w_glu"], pp["w_out"], pp["ln1_g"], pp["ln1_b"], alpha)
    h = _ffn_up(x1b, pp["w_gate"], pp["w_up"])
    out = _ffn_down(h, pp["w_down"], x1, pp["ln2_g"], pp["ln2_b"], alpha)
    return out.reshape(B, L, D)


def kernel(x_prompt, x_sample, w_in, lambda_q1, lambda_k1, lambda_q2, lambda_k2, subln_w, ssm_a_re, ssm_a_im, ssm_log_dt, ssm_b_re, ssm_b_im, ssm_c_re, ssm_c_im, ssm_d, w_glu, w_out, ln1_g, ln1_b, w_gate, w_up, w_down, ln2_g, ln2_b):
    params = (w_in, lambda_q1, lambda_k1, lambda_q2, lambda_k2, subln_w,
              ssm_a_re, ssm_a_im, ssm_log_dt, ssm_b_re, ssm_b_im, ssm_c_re, ssm_c_im, ssm_d, w_glu,
              w_out, ln1_g, ln1_b, w_gate, w_up, w_down, ln2_g, ln2_b)
    depth = w_in.shape[0]
    y_prompt, y_sample = x_prompt, x_sample
    for layer in range(depth):
        lp = _prepare_layer(tuple(t[layer] for t in params))
        y_prompt = _encoder_layer(y_prompt, layer, depth, lp)
        y_sample = _encoder_layer(y_sample, layer, depth, lp)
    return (y_prompt, y_sample)
```

```python
import functools
import math

import jax
import jax.numpy as jnp
from jax import lax
from jax.experimental import pallas as pl
from jax.experimental.pallas import tpu as pltpu

F32 = jnp.float32
BF16 = jnp.bfloat16

DIFF_HEAD_DIM = 64
ROT_DIM = DIFF_HEAD_DIM // 4
ROPE_THETA = 500000.0
SSM_GROUP = 16
SSM_STATE = 64
LN_EPS = 1e-5

LANES = 128
SUBLANES = 8
VMEM_LIMIT_BYTES = 56 * 1024 * 1024

SSM_CHUNK = 32
SSM_GBLK = SUBLANES
ATT_BK = 512
ATT_BQ = 512
ATT_UNROLL = 8
ATT_ONES_ROWS = 16
NEG_BIG = -1e30


def _cparams(sem):
    return pltpu.CompilerParams(dimension_semantics=sem, vmem_limit_bytes=VMEM_LIMIT_BYTES)


def _inproj_kernel(xb_ref, w_ref, wvt_ref, rope_ref, q_ref, k_ref, vt_ref, u_ref, ub_ref):
    j = pl.program_id(1)

    def rope_to(out_ref):
        acc = jnp.dot(xb_ref[...], w_ref[...], preferred_element_type=F32)
        c = rope_ref[0, 0]
        sa = rope_ref[0, 1]
        sb = rope_ref[0, 2]
        for cb in range(acc.shape[1] // LANES):
            xs = acc[:, cb * LANES:(cb + 1) * LANES]
            nxt = pltpu.roll(xs, LANES - ROT_DIM // 2, axis=1)
            prv = pltpu.roll(xs, ROT_DIM // 2, axis=1)
            out_ref[cb] = (xs * c + nxt * sa + prv * sb).astype(out_ref.dtype)

    @pl.when(j == 0)
    def _():
        rope_to(q_ref)

    @pl.when(j == 1)
    def _():
        rope_to(k_ref)

    @pl.when(j == 2)
    def _():
        vt = lax.dot_general(wvt_ref[...], xb_ref[...], (((1,), (1,)), ((), ())),
                             preferred_element_type=F32)
        hd = 2 * DIFF_HEAD_DIM
        ones = jnp.ones((vt_ref.shape[2] - hd, ATT_BK), vt_ref.dtype)
        for cc in range(vt_ref.shape[0]):
            for h in range(vt_ref.shape[1]):
                vt_ref[cc, h, pl.ds(0, hd), :] = (
                    vt[h * hd:(h + 1) * hd, cc * ATT_BK:(cc + 1) * ATT_BK].astype(vt_ref.dtype))
                vt_ref[cc, h, pl.ds(hd, ones.shape[0]), :] = ones

    @pl.when(j == 3)
    def _():
        u = jnp.dot(xb_ref[...], w_ref[...], preferred_element_type=F32)
        u_ref[...] = u
        ub_ref[...] = u.astype(ub_ref.dtype)


def _inproj(xb2d, w_qku, wvt, rope, seq_len, tm=1024):
    T, D = xb2d.shape
    W = wvt.shape[0]
    hd = 2 * DIFF_HEAD_DIM
    H = W // hd
    hda = hd + ATT_ONES_ROWS
    nseq_blk = seq_len // tm
    grid = (T // tm, 4)
    wmap = lambda i, j: (0, jnp.where(j >= 2, j - 1, j))
    return pl.pallas_call(
        _inproj_kernel,
        grid=grid,
        in_specs=[
            pl.BlockSpec((tm, D), lambda i, j: (i, 0)),
            pl.BlockSpec((D, W), wmap),
            pl.BlockSpec((W, D), lambda i, j: (0, 0), pipeline_mode=pl.Buffered(1)),
            pl.BlockSpec((1, 3, tm, LANES), lambda i, j: (jnp.minimum(j, 1), 0, i % nseq_blk, 0)),
        ],
        out_specs=[
            pl.BlockSpec((H, tm, hd), lambda i, j: (0, i, 0)),
            pl.BlockSpec((H, tm, hd), lambda i, j: (0, i, 0)),
            pl.BlockSpec((tm // ATT_BK, H, hda, ATT_BK), lambda i, j: (i, 0, 0, 0)),
            pl.BlockSpec((tm, W), lambda i, j: (i, 0)),
            pl.BlockSpec((tm, W), lambda i, j: (i, 0)),
        ],
        out_shape=[
            jax.ShapeDtypeStruct((H, T, hd), BF16),
            jax.ShapeDtypeStruct((H, T, hd), BF16),
            jax.ShapeDtypeStruct((T // ATT_BK, H, hda, ATT_BK), BF16),
            jax.ShapeDtypeStruct((T, W), F32),
            jax.ShapeDtypeStruct((T, W), BF16),
        ],
        compiler_params=_cparams(("arbitrary", "arbitrary")),
        name="inproj",
    )(xb2d, w_qku, wvt, rope)


def _attn_kernel(lam_ref, sub_ref, q_ref, k_ref, vt_ref, o_ref, q1t_ref, q2t_ref, o1_ref, o2_ref, s_ref,
                 *, lam_init, bq, unroll):
    nk = vt_ref.shape[0]
    L, hd = q_ref.shape
    nq = L // bq
    trips_per_q = nk // unroll

    lv = lam_ref[...]
    d1 = jnp.sum(lv[0:1] * lv[1:2], axis=1, keepdims=True)
    d2 = jnp.sum(lv[2:3] * lv[3:4], axis=1, keepdims=True)
    lam = jnp.exp(d1) - jnp.exp(d2) + lam_init

    def transpose_q(qi, carry):
        qt = q_ref[pl.ds(pl.multiple_of(qi * bq, bq), bq), :].astype(F32).T
        row = lax.broadcasted_iota(jnp.int32, qt.shape, 0)
        q1t_ref[qi] = jnp.where(row < DIFF_HEAD_DIM, qt, 0.0).astype(BF16)
        q2t_ref[qi] = jnp.where(row >= DIFF_HEAD_DIM, qt, 0.0).astype(BF16)
        return carry

    lax.fori_loop(0, nq, transpose_q, 0)
    o1_ref[...] = jnp.zeros_like(o1_ref)
    o2_ref[...] = jnp.zeros_like(o2_ref)

    def scores(qi, c, slot):
        kb = k_ref[pl.ds(pl.multiple_of(c * ATT_BK, ATT_BK), ATT_BK), :]
        tops = []
        for br, qt_ref in enumerate((q1t_ref, q2t_ref)):
            s = jnp.dot(kb, qt_ref[qi], preferred_element_type=F32)
            s_ref[slot, br] = s
            tops.append(jnp.max(s, axis=0, keepdims=True))
        return tuple(tops)

    def update(c, slot, tops, ms):
        vtb = vt_ref[c]
        new_ms = []
        for br, o_acc in enumerate((o1_ref, o2_ref)):
            m_new = jnp.maximum(ms[br], tops[br])
            alpha = jnp.exp2(ms[br] - m_new)
            p = jnp.exp2(s_ref[slot, br] - m_new).astype(BF16)
            pv = jnp.dot(vtb, p, preferred_element_type=F32)
            o_acc[...] = alpha * o_acc[...] + pv
            new_ms.append(m_new)
        return tuple(new_ms)

    def finish(qi):
        l1 = o1_ref[pl.ds(hd, 1), :]
        l2 = o2_ref[pl.ds(hd, 1), :]
        o = o1_ref[pl.ds(0, hd), :] / l1 - lam * (o2_ref[pl.ds(0, hd), :] / l2)
        msq = jnp.mean(o * o, axis=0, keepdims=True)
        o = o * lax.rsqrt(msq + LN_EPS) * sub_ref[...]
        o = o * (1.0 - lam_init)
        o_ref[pl.ds(pl.multiple_of(qi * bq, bq), bq), :] = o.T.astype(o_ref.dtype)
        o1_ref[...] = jnp.zeros_like(o1_ref)
        o2_ref[...] = jnp.zeros_like(o2_ref)

    neg = jnp.full((1, bq), NEG_BIG, F32)

    def body(t, carry):
        ms, tops = carry
        qi = t // trips_per_q
        c0 = unroll * (t % trips_per_q)
        last = c0 + unroll == nk
        for j in range(unroll):
            if j + 1 < unroll:
                nxt_tops = scores(qi, c0 + j + 1, (j + 1) % 2)
            else:
                nxt_q = jnp.where(last, jnp.minimum(qi + 1, nq - 1), qi)
                nxt_c = jnp.where(last, 0, c0 + unroll)
                nxt_tops = scores(nxt_q, nxt_c, 0)
            ms = update(c0 + j, j % 2, tops, ms)
            tops = nxt_tops

        @pl.when(last)
        def _():
            finish(qi)

        ms = tuple(jnp.where(last, neg, m) for m in ms)
        return ms, tops

    tops0 = scores(0, 0, 0)
    lax.fori_loop(0, nq * trips_per_q, body, ((neg, neg), tops0))


def _attention(q, k, vt, lam_vecs, sub_col, lam_init, seq_len):
    H, T, hd = q.shape
    L = seq_len
    B = T // L
    hda = vt.shape[2]
    nk = L // ATT_BK
    unroll = min(ATT_UNROLL, nk)
    assert unroll % 2 == 0 and nk % unroll == 0
    bq = min(ATT_BQ, L)
    nq = L // bq
    seq = lambda b, h: (h, b, 0)
    return pl.pallas_call(
        functools.partial(_attn_kernel, lam_init=lam_init, bq=bq, unroll=unroll),
        grid=(B, H),
        in_specs=[
            pl.BlockSpec(lam_vecs.shape, lambda b, h: (0, 0)),
            pl.BlockSpec(sub_col.shape, lambda b, h: (0, 0)),
            pl.BlockSpec((None, L, hd), seq),
            pl.BlockSpec((None, L, hd), seq),
            pl.BlockSpec((nk, None, hda, ATT_BK), lambda b, h: (b, h, 0, 0)),
        ],
        out_specs=pl.BlockSpec((None, L, hd), seq),
        out_shape=jax.ShapeDtypeStruct((H, T, hd), BF16),
        scratch_shapes=[
            pltpu.VMEM((nq, hd, bq), BF16),
            pltpu.VMEM((nq, hd, bq), BF16),
            pltpu.VMEM((hda, bq), F32),
            pltpu.VMEM((hda, bq), F32),
            pltpu.VMEM((2, 2, ATT_BK, bq), F32),
        ],
        compiler_params=_cparams(("arbitrary", "arbitrary")),
        name="diff_attn",
    )(lam_vecs, sub_col, q, k, vt)


def _ssm_kernel(*refs, cps):
    T = SSM_CHUNK
    u_refs = refs[:T]
    p_ref, pt_ref, w_ref, r_ref, v_ref, lre_ref, lim_ref, y_ref = refs[T:T + 8]
    ug_ref, yg_ref, m_ref, sre, sim, hfre, hfim, hbre, hbim = refs[T + 8:]
    o = pl.program_id(1)

    @pl.when(o == 0)
    def _():
        _ssm_compute(u_refs, p_ref, w_ref, r_ref, v_ref, lre_ref, lim_ref,
                     ug_ref, yg_ref, m_ref, sre, sim, hfre, hfim, hbre, hbim, cps)

    yo = yg_ref[o]
    hi = yo.astype(BF16)
    lo = (yo - hi.astype(F32)).astype(BF16)
    res = (jnp.dot(hi, pt_ref[...], preferred_element_type=F32)
           + jnp.dot(lo, pt_ref[...], preferred_element_type=F32))
    for t8 in range(SUBLANES):
        y_ref[:, t8, :] = res[:, t8 * LANES:(t8 + 1) * LANES]


def _ssm_compute(u_refs, p_ref, w_ref, r_ref, v_ref, lre_ref, lim_ref,
                 ug_ref, yg_ref, m_ref, sre, sim, hfre, hfim, hbre, hbim, cps):
    gb = ug_ref.shape[0]
    nc = ug_ref.shape[1]
    half = LANES // 2
    n_oct = len(u_refs) // SUBLANES

    for oc in range(n_oct):
        z = jnp.concatenate([u_refs[oc * SUBLANES + t8][...] for t8 in range(SUBLANES)], axis=1)
        zg = jnp.dot(z, p_ref[...], preferred_element_type=F32).astype(BF16)
        for gi in range(gb):
            ug_ref[gi, :, pl.ds(oc * LANES, LANES)] = zg[:, gi * LANES:(gi + 1) * LANES]

    T = len(u_refs)
    C = SSM_GROUP
    for gi in range(gb):
        r = r_ref[gi]
        for s_ in range(T):
            off = (T - 1 - s_) * C
            m_ref[gi, pl.ds(s_ * C, C), :] = r[:, off:off + T * C].astype(BF16)

    for gi in range(gb):
        s = jnp.dot(ug_ref[gi], w_ref[gi], preferred_element_type=F32)
        sre[pl.ds(gi, nc, stride=gb), :] = s[:, :LANES]
        sim[pl.ds(gi, nc, stride=gb), :] = s[:, LANES:]

    zero = jnp.zeros((gb, LANES), F32)
    lre = lre_ref[...]
    lim = lim_ref[...]
    fwd_lane = lax.broadcasted_iota(jnp.int32, (gb, LANES), 1) < half

    def body(j, carry):
        hr, hi = carry
        fresh = (j % cps) == 0
        hr = jnp.where(fresh, 0.0, hr)
        hi = jnp.where(fresh, 0.0, hi)
        jf = pl.multiple_of(j * gb, gb)
        jb = pl.multiple_of((nc - 1 - j) * gb, gb)
        hfre[pl.ds(jf, gb), :] = hr
        hfim[pl.ds(jf, gb), :] = hi
        hbre[pl.ds(jb, gb), :] = hr
        hbim[pl.ds(jb, gb), :] = hi
        sr = jnp.where(fwd_lane, sre[pl.ds(jf, gb), :], sre[pl.ds(jb, gb), :])
        si = jnp.where(fwd_lane, sim[pl.ds(jf, gb), :], sim[pl.ds(jb, gb), :])
        return lre * hr - lim * hi + sr, lre * hi + lim * hr + si

    lax.fori_loop(0, nc, body, (zero, zero))

    fwd_col = lax.broadcasted_iota(jnp.int32, (nc, LANES), 1) < half
    for gi in range(gb):
        hr = jnp.where(fwd_col, hfre[pl.ds(gi, nc, stride=gb), :], hbre[pl.ds(gi, nc, stride=gb), :])
        hi = jnp.where(fwd_col, hfim[pl.ds(gi, nc, stride=gb), :], hbim[pl.ds(gi, nc, stride=gb), :])
        h = jnp.concatenate([hr, hi], axis=1).astype(BF16)
        y = jnp.dot(ug_ref[gi], m_ref[gi], preferred_element_type=F32)
        y = y + jnp.dot(h, v_ref[gi], preferred_element_type=F32)
        for oc in range(n_oct):
            yg_ref[oc, :, pl.ds(gi * LANES, LANES)] = y[:, oc * LANES:(oc + 1) * LANES]


def _lane_regroup_matrix():
    n = SUBLANES * LANES
    i = jnp.arange(n)
    t8, gl, c = i // LANES, (i % LANES) // SSM_GROUP, i % SSM_GROUP
    dest = gl * LANES + t8 * SSM_GROUP + c
    return (dest[:, None] == i[None, :]).astype(BF16)


def _ssm(ub, w_t, r_t, v_t, lre, lim, seq_len):
    Ttok, Wd = ub.shape
    T = SSM_CHUNK
    nc = Ttok // T
    cps = seq_len // T
    gb = SSM_GBLK
    assert gb * SSM_GROUP == LANES and T % SUBLANES == 0 and nc % cps == 0
    n_slab = Wd // LANES
    n_oct = T // SUBLANES
    K = T * SSM_GROUP
    ncols = w_t.shape[2]
    pmat = _lane_regroup_matrix()
    u2 = ub.reshape(nc, T * Wd)
    once = pl.Buffered(1)
    fixed2 = lambda g, o: (0, 0)
    per_slab = lambda g, o: (g, 0, 0)

    def u_spec(t):
        return pl.BlockSpec((nc, LANES), lambda g, o: (0, t * n_slab + g))

    y = pl.pallas_call(
        functools.partial(_ssm_kernel, cps=cps),
        grid=(n_slab, n_oct),
        in_specs=[u_spec(t) for t in range(T)] + [
            pl.BlockSpec(pmat.shape, fixed2, pipeline_mode=once),
            pl.BlockSpec(pmat.shape, fixed2, pipeline_mode=once),
            pl.BlockSpec((gb, K, ncols), per_slab, pipeline_mode=once),
            pl.BlockSpec((gb,) + r_t.shape[1:], per_slab, pipeline_mode=once),
            pl.BlockSpec((gb, ncols, K), per_slab, pipeline_mode=once),
            pl.BlockSpec((gb, LANES), lambda g, o: (g, 0)),
            pl.BlockSpec((gb, LANES), lambda g, o: (g, 0)),
        ],
        out_specs=pl.BlockSpec((nc, SUBLANES, LANES), lambda g, o: (0, o, g)),
        out_shape=jax.ShapeDtypeStruct((nc, T, Wd), F32),
        scratch_shapes=[
            pltpu.VMEM((gb, nc, K), BF16),
            pltpu.VMEM((n_oct, nc, SUBLANES * LANES), F32),
            pltpu.VMEM((gb, K, K), BF16),
        ] + [pltpu.VMEM((nc * gb, LANES), F32)] * 6,
        compiler_params=_cparams(("arbitrary", "arbitrary")),
        name="s5_chunked",
    )(*([u2] * T), pmat, pmat.T, w_t, r_t, v_t, lre, lim)
    return y.reshape(Ttok, Wd)


def _ssm_tables(a_re, a_im, log_dt, b_re, b_im, c_re, c_im):
    hp = lax.Precision.HIGHEST
    T = SSM_CHUNK
    G, P = a_re.shape[1], a_re.shape[2]
    C = b_re.shape[3]
    dt = jnp.exp(log_dt)[..., None]
    zr = a_re * dt
    zi = a_im * dt
    tau = jnp.arange(T + 1, dtype=F32)[:, None, None, None]
    mag = jnp.exp(zr[None] * tau)
    pr = mag * jnp.cos(zi[None] * tau)
    pi = mag * jnp.sin(zi[None] * tau)
    ab_re, ab_im = pr[1], pi[1]
    den = a_re * a_re + a_im * a_im
    nr = ab_re - 1.0
    f_re = ((nr * a_re + ab_im * a_im) / den)[..., None]
    f_im = ((ab_im * a_re - nr * a_im) / den)[..., None]
    bb_re = f_re * b_re - f_im * b_im
    bb_im = f_re * b_im + f_im * b_re

    cp_re = c_re[None] * pr[:, :, :, None, :] - c_im[None] * pi[:, :, :, None, :]
    cp_im = c_re[None] * pi[:, :, :, None, :] + c_im[None] * pr[:, :, :, None, :]
    pb_re = pr[..., None] * bb_re[None] - pi[..., None] * bb_im[None]
    pb_im = pr[..., None] * bb_im[None] + pi[..., None] * bb_re[None]

    cb = jnp.einsum('tdgcq,dgqk->tdgck', jnp.concatenate([cp_re[:T], -cp_im[:T]], axis=-1),
                    jnp.concatenate([bb_re, bb_im], axis=2), precision=lax.Precision.HIGH)

    kern = jnp.concatenate([cb[T - 1:0:-1, 1], (cb[0, 0] + cb[0, 1])[None], cb[1:, 0],
                            jnp.zeros_like(cb[:1, 0])])
    r_t = kern.transpose(1, 3, 0, 2).reshape(G, C, 2 * T * C)

    rev = jnp.arange(T - 1, -1, -1)
    w_cols = [pb_re[rev, 0], pb_re[:T, 1], pb_im[rev, 0], pb_im[:T, 1]]
    w_t = jnp.concatenate([w.transpose(1, 0, 3, 2) for w in w_cols], axis=3)
    w_t = w_t.reshape(G, T * C, 4 * P)

    fwd_pow = jnp.arange(1, T + 1)
    bwd_pow = jnp.arange(T, 0, -1)
    v_rows = [cp_re[fwd_pow, 0], cp_re[bwd_pow, 1], -cp_im[fwd_pow, 0], -cp_im[bwd_pow, 1]]
    v_t = jnp.concatenate([v.transpose(1, 3, 0, 2) for v in v_rows], axis=1)
    v_t = v_t.reshape(G, 4 * P, T * C)

    lre = jnp.concatenate([pr[T, 0], pr[T, 1]], axis=1)
    lim = jnp.concatenate([pi[T, 0], pi[T, 1]], axis=1)
    return w_t.astype(BF16), r_t, v_t.astype(BF16), lre, lim


def _layer_norm(r, g, b):
    mu = jnp.mean(r, axis=-1, keepdims=True)
    d = r - mu
    var = jnp.mean(d * d, axis=-1, keepdims=True)
    return d * lax.rsqrt(var + LN_EPS) * g + b


def _mix_kernel(x_ref, a_ref, y_ref, u_ref, d_ref, wglu_ref, wout_ref, g_ref, b_ref, o_ref, ob_ref,
                *, alpha):
    n_heads, tm, hd = a_ref.shape
    wa = n_heads * hd
    for half in range(2):
        rows = pl.ds(half * (tm // 2), tm // 2)
        yy = y_ref[rows, :] + d_ref[...] * u_ref[rows, :]
        z = jax.nn.gelu(yy)
        gate = jax.nn.sigmoid(jnp.dot(z.astype(BF16), wglu_ref[...], preferred_element_type=F32))
        s = (z * gate).astype(BF16)
        a = jnp.concatenate([a_ref[h, rows, :] for h in range(n_heads)], axis=1)
        mixed = jnp.dot(a, wout_ref[pl.ds(0, wa), :], preferred_element_type=F32)
        mixed = mixed + jnp.dot(s, wout_ref[pl.ds(wa, s.shape[1]), :], preferred_element_type=F32)
        r = alpha * x_ref[rows, :] + mixed
        o = _layer_norm(r, g_ref[...], b_ref[...])
        o_ref[rows, :] = o
        ob_ref[rows, :] = o.astype(BF16)


def _mix(x2d, attn, y, u, d, wglu, wout, g, b, alpha, tm=512):
    T, D = x2d.shape
    W = y.shape[1]
    row = lambda i: (i, 0)
    fixed = lambda i: (0, 0)
    once = pl.Buffered(1)
    return pl.pallas_call(
        functools.partial(_mix_kernel, alpha=alpha),
        grid=(T // tm,),
        in_specs=[
            pl.BlockSpec((tm, D), row),
            pl.BlockSpec((attn.shape[0], tm, attn.shape[2]), lambda i: (0, i, 0)),
            pl.BlockSpec((tm, W), row),
            pl.BlockSpec((tm, W), row),
            pl.BlockSpec((1, W), fixed),
            pl.BlockSpec(wglu.shape, fixed, pipeline_mode=once),
            pl.BlockSpec(wout.shape, fixed, pipeline_mode=once),
            pl.BlockSpec((1, D), fixed),
            pl.BlockSpec((1, D), fixed),
        ],
        out_specs=[pl.BlockSpec((tm, D), row), pl.BlockSpec((tm, D), row)],
        out_shape=[jax.ShapeDtypeStruct((T, D), F32), jax.ShapeDtypeStruct((T, D), BF16)],
        compiler_params=_cparams(("arbitrary",)),
        name="glu_outproj_ln",
    )(x2d, attn, y, u, d, wglu, wout, g, b)


def _ffn_up_kernel(x_ref, wg_ref, wu_ref, h_ref):
    x = x_ref[...]
    g = jnp.dot(x, wg_ref[...], preferred_element_type=F32)
    up = jnp.dot(x, wu_ref[...], preferred_element_type=F32)
    h_ref[...] = (jax.nn.silu(g) * up).astype(h_ref.dtype)


def _ffn_up(xb, wg, wu, tm=1024, tf=512):
    T, D = xb.shape
    F = wg.shape[1]
    return pl.pallas_call(
        _ffn_up_kernel,
        grid=(T // tm, F // tf),
        in_specs=[
            pl.BlockSpec((tm, D), lambda i, j: (i, 0)),
            pl.BlockSpec((D, tf), lambda i, j: (0, j)),
            pl.BlockSpec((D, tf), lambda i, j: (0, j)),
        ],
        out_specs=pl.BlockSpec((tm, tf), lambda i, j: (i, j)),
        out_shape=jax.ShapeDtypeStruct((T, F), BF16),
        compiler_params=_cparams(("arbitrary", "arbitrary")),
        name="ffn_up",
    )(xb, wg, wu)


def _ffn_down_kernel(h_ref, wd_ref, x_ref, g_ref, b_ref, o_ref, *, alpha):
    down = jnp.dot(h_ref[...], wd_ref[...], preferred_element_type=F32)
    o_ref[...] = _layer_norm(alpha * x_ref[...] + down, g_ref[...], b_ref[...])


def _ffn_down(h, wd, x1, g, b, alpha, tm=256):
    T, F = h.shape
    D = wd.shape[1]
    return pl.pallas_call(
        functools.partial(_ffn_down_kernel, alpha=alpha),
        grid=(T // tm,),
        in_specs=[
            pl.BlockSpec((tm, F), lambda i: (i, 0)),
            pl.BlockSpec((F, D), lambda i: (0, 0), pipeline_mode=pl.Buffered(1)),
            pl.BlockSpec((tm, D), lambda i: (i, 0)),
            pl.BlockSpec((1, D), lambda i: (0, 0)),
            pl.BlockSpec((1, D), lambda i: (0, 0)),
        ],
        out_specs=pl.BlockSpec((tm, D), lambda i: (i, 0)),
        out_shape=jax.ShapeDtypeStruct((T, D), F32),
        compiler_params=_cparams(("arbitrary",)),
        name="ffn_down_ln",
    )(h, wd, x1, g, b)


def _rope_tables(L):
    half = ROT_DIM // 2
    inv = ROPE_THETA ** (-jnp.arange(0, ROT_DIM, 2, dtype=F32) / ROT_DIM)
    ang = jnp.arange(L, dtype=F32)[:, None] * inv[None, :]
    cos, sin = jnp.cos(ang), jnp.sin(ang)
    ones = jnp.ones((L, DIFF_HEAD_DIM - ROT_DIM), F32)
    zeros_h = jnp.zeros((L, half), F32)
    zeros_r = jnp.zeros((L, DIFF_HEAD_DIM - ROT_DIM), F32)
    c = jnp.concatenate([cos, cos, ones], axis=1)
    sa = jnp.concatenate([-sin, zeros_h, zeros_r], axis=1)
    sb = jnp.concatenate([zeros_h, sin, zeros_r], axis=1)
    tab = jnp.stack([jnp.tile(t, (1, LANES // DIFF_HEAD_DIM)) for t in (c, sa, sb)])
    return jnp.stack([tab * (DIFF_HEAD_DIM ** -0.5 * math.log2(math.e)), tab])


def _prepare_layer(p):
    (w_in, lq1, lk1, lq2, lk2, subln_w, a_re, a_im, log_dt, b_re, b_im, c_re, c_im, ssm_d, w_glu,
     w_out, ln1_g, ln1_b, w_gate, w_up, w_down, ln2_g, ln2_b) = p
    W = w_glu.shape[0]
    D = w_in.shape[0]
    row = lambda t, n: t.reshape(1, n).astype(F32)
    return dict(
        w_qku=jnp.concatenate([w_in[:, :2 * W], w_in[:, 3 * W:]], axis=1).astype(BF16),
        wvt=w_in[:, 2 * W:3 * W].T.astype(BF16),
        lam_vecs=jnp.stack([lq1, lk1, lq2, lk2]).astype(F32),
        sub_col=subln_w.astype(F32).reshape(-1, 1),
        ssm=_ssm_tables(a_re, a_im, log_dt, b_re, b_im, c_re, c_im),
        ssm_d=row(ssm_d, W), w_glu=w_glu.astype(BF16), w_out=w_out.astype(BF16),
        ln1_g=row(ln1_g, D), ln1_b=row(ln1_b, D),
        w_gate=w_gate.astype(BF16), w_up=w_up.astype(BF16), w_down=w_down.astype(BF16),
        ln2_g=row(ln2_g, D), ln2_b=row(ln2_b, D),
    )


def _encoder_layer(x, layer_idx, depth, pp):
    B, L, D = x.shape
    T = B * L
    W = pp["w_glu"].shape[0]
    alpha = (2 * depth) ** 0.25
    lam_init = 0.8 - 0.6 * math.exp(-0.3 * layer_idx)

    x2d = x.reshape(T, D)
    q, k, vt, u, ub = _inproj(x2d.astype(BF16), pp["w_qku"], pp["wvt"], _rope_tables(L), L)
    attn = _attention(q, k, vt, pp["lam_vecs"], pp["sub_col"], lam_init, L)
    y = _ssm(ub, *pp["ssm"], L)
    x1, x1b = _mix(x2d, attn, y, u, pp["ssm_d"], pp["w_glu"], pp["w_out"], pp["ln1_g"], pp["ln1_b"], alpha)
    h = _ffn_up(x1b, pp["w_gate"], pp["w_up"])
    out = _ffn_down(h, pp["w_down"], x1, pp["ln2_g"], pp["ln2_b"], alpha)
    return out.reshape(B, L, D)


def kernel(x_prompt, x_sample, w_in, lambda_q1, lambda_k1, lambda_q2, lambda_k2, subln_w, ssm_a_re, ssm_a_im, ssm_log_dt, ssm_b_re, ssm_b_im, ssm_c_re, ssm_c_im, ssm_d, w_glu, w_out, ln1_g, ln1_b, w_gate, w_up, w_down, ln2_g, ln2_b):
    params = (w_in, lambda_q1, lambda_k1, lambda_q2, lambda_k2, subln_w,
              ssm_a_re, ssm_a_im, ssm_log_dt, ssm_b_re, ssm_b_im, ssm_c_re, ssm_c_im, ssm_d, w_glu,
              w_out, ln1_g, ln1_b, w_gate, w_up, w_down, ln2_g, ln2_b)
    depth = w_in.shape[0]
    y_prompt, y_sample = x_prompt, x_sample
    for layer in range(depth):
        lp = _prepare_layer(tuple(t[layer] for t in params))
        y_prompt = _encoder_layer(y_prompt, layer, depth, lp)
        y_sample = _encoder_layer(y_sample, layer, depth, lp)
    return (y_prompt, y_sample)
```

```python
import functools
import math

import jax
import jax.numpy as jnp
from jax import lax
from jax.experimental import pallas as pl
from jax.experimental.pallas import tpu as pltpu

F32 = jnp.float32
BF16 = jnp.bfloat16

DIFF_HEAD_DIM = 64
ROT_DIM = DIFF_HEAD_DIM // 4
ROPE_THETA = 500000.0
SSM_GROUP = 16
SSM_STATE = 64
LN_EPS = 1e-5

LANES = 128
SUBLANES = 8
VMEM_LIMIT_BYTES = 56 * 1024 * 1024

SSM_CHUNK = 32
SSM_GBLK = SUBLANES
ATT_BK = 512
ATT_BQ = 512
ATT_UNROLL = 8
ATT_UNROLL_SHORT = 8
ATT_ONES_ROWS = 16
NEG_BIG = -1e30


def _cparams(sem):
    return pltpu.CompilerParams(dimension_semantics=sem, vmem_limit_bytes=VMEM_LIMIT_BYTES)


def _inproj_kernel(xb_ref, w_ref, wvt_ref, rope_ref, q_ref, k_ref, vt_ref, u_ref, ub_ref):
    j = pl.program_id(1)

    def rope_to(out_ref):
        acc = jnp.dot(xb_ref[...], w_ref[...], preferred_element_type=F32)
        c = rope_ref[0, 0]
        sa = rope_ref[0, 1]
        sb = rope_ref[0, 2]
        for cb in range(acc.shape[1] // LANES):
            xs = acc[:, cb * LANES:(cb + 1) * LANES]
            nxt = pltpu.roll(xs, LANES - ROT_DIM // 2, axis=1)
            prv = pltpu.roll(xs, ROT_DIM // 2, axis=1)
            out_ref[cb] = (xs * c + nxt * sa + prv * sb).astype(out_ref.dtype)

    @pl.when(j == 0)
    def _():
        rope_to(q_ref)

    @pl.when(j == 1)
    def _():
        rope_to(k_ref)

    @pl.when(j == 2)
    def _():
        vt = lax.dot_general(wvt_ref[...], xb_ref[...], (((1,), (1,)), ((), ())),
                             preferred_element_type=F32)
        hd = 2 * DIFF_HEAD_DIM
        ones = jnp.ones((vt_ref.shape[2] - hd, ATT_BK), vt_ref.dtype)
        for cc in range(vt_ref.shape[0]):
            for h in range(vt_ref.shape[1]):
                vt_ref[cc, h, pl.ds(0, hd), :] = (
                    vt[h * hd:(h + 1) * hd, cc * ATT_BK:(cc + 1) * ATT_BK].astype(vt_ref.dtype))
                vt_ref[cc, h, pl.ds(hd, ones.shape[0]), :] = ones

    @pl.when(j == 3)
    def _():
        u = jnp.dot(xb_ref[...], w_ref[...], preferred_element_type=F32)
        u_ref[...] = u
        ub_ref[...] = u.astype(ub_ref.dtype)


def _inproj(xb2d, w_qku, wvt, rope, seq_len, tm=1024):
    T, D = xb2d.shape
    W = wvt.shape[0]
    hd = 2 * DIFF_HEAD_DIM
    H = W // hd
    hda = hd + ATT_ONES_ROWS
    nseq_blk = seq_len // tm
    grid = (T // tm, 4)
    wmap = lambda i, j: (0, jnp.where(j >= 2, j - 1, j))
    return pl.pallas_call(
        _inproj_kernel,
        grid=grid,
        in_specs=[
            pl.BlockSpec((tm, D), lambda i, j: (i, 0)),
            pl.BlockSpec((D, W), wmap),
            pl.BlockSpec((W, D), lambda i, j: (0, 0), pipeline_mode=pl.Buffered(1)),
            pl.BlockSpec((1, 3, tm, LANES), lambda i, j: (jnp.minimum(j, 1), 0, i % nseq_blk, 0)),
        ],
        out_specs=[
            pl.BlockSpec((H, tm, hd), lambda i, j: (0, i, 0)),
            pl.BlockSpec((H, tm, hd), lambda i, j: (0, i, 0)),
            pl.BlockSpec((tm // ATT_BK, H, hda, ATT_BK), lambda i, j: (i, 0, 0, 0)),
            pl.BlockSpec((tm, W), lambda i, j: (i, 0)),
            pl.BlockSpec((tm, W), lambda i, j: (i, 0)),
        ],
        out_shape=[
            jax.ShapeDtypeStruct((H, T, hd), BF16),
            jax.ShapeDtypeStruct((H, T, hd), BF16),
            jax.ShapeDtypeStruct((T // ATT_BK, H, hda, ATT_BK), BF16),
            jax.ShapeDtypeStruct((T, W), F32),
            jax.ShapeDtypeStruct((T, W), BF16),
        ],
        compiler_params=_cparams(("arbitrary", "arbitrary")),
        name="inproj",
    )(xb2d, w_qku, wvt, rope)


def _attn_kernel(lam_ref, sub_ref, q_ref, k_ref, vt_ref, o_ref, q1t_ref, q2t_ref, o1_ref, o2_ref, s_ref,
                 *, lam_init, bq, unroll):
    nk = vt_ref.shape[0]
    L, hd = q_ref.shape
    nq = L // bq

    lv = lam_ref[...]
    d1 = jnp.sum(lv[0:1] * lv[1:2], axis=1, keepdims=True)
    d2 = jnp.sum(lv[2:3] * lv[3:4], axis=1, keepdims=True)
    lam = jnp.exp(d1) - jnp.exp(d2) + lam_init

    def transpose_q(qi, carry):
        qt = q_ref[pl.ds(pl.multiple_of(qi * bq, bq), bq), :].astype(F32).T
        row = lax.broadcasted_iota(jnp.int32, qt.shape, 0)
        q1t_ref[qi] = jnp.where(row < DIFF_HEAD_DIM, qt, 0.0).astype(BF16)
        q2t_ref[qi] = jnp.where(row >= DIFF_HEAD_DIM, qt, 0.0).astype(BF16)
        return carry

    lax.fori_loop(0, nq, transpose_q, 0)
    o1_ref[...] = jnp.zeros_like(o1_ref)
    o2_ref[...] = jnp.zeros_like(o2_ref)

    def scores(qi, c, slot):
        kb = k_ref[pl.ds(pl.multiple_of(c * ATT_BK, ATT_BK), ATT_BK), :]
        tops = []
        for br, qt_ref in enumerate((q1t_ref, q2t_ref)):
            s = jnp.dot(kb, qt_ref[qi], preferred_element_type=F32)
            s_ref[slot, br] = s
            tops.append(jnp.max(s, axis=0, keepdims=True))
        return tuple(tops)

    def update(c, slot, tops, ms):
        vtb = vt_ref[c]
        new_ms = []
        for br, o_acc in enumerate((o1_ref, o2_ref)):
            m_new = jnp.maximum(ms[br], tops[br])
            alpha = jnp.exp2(ms[br] - m_new)
            p = jnp.exp2(s_ref[slot, br] - m_new).astype(BF16)
            pv = jnp.dot(vtb, p, preferred_element_type=F32)
            o_acc[...] = alpha * o_acc[...] + pv
            new_ms.append(m_new)
        return tuple(new_ms)

    def finish(qi):
        l1 = o1_ref[pl.ds(hd, 1), :]
        l2 = o2_ref[pl.ds(hd, 1), :]
        o = o1_ref[pl.ds(0, hd), :] / l1 - lam * (o2_ref[pl.ds(0, hd), :] / l2)
        msq = jnp.mean(o * o, axis=0, keepdims=True)
        o = o * lax.rsqrt(msq + LN_EPS) * sub_ref[...]
        o = o * (1.0 - lam_init)
        o_ref[pl.ds(pl.multiple_of(qi * bq, bq), bq), :] = o.T.astype(o_ref.dtype)
        o1_ref[...] = jnp.zeros_like(o1_ref)
        o2_ref[...] = jnp.zeros_like(o2_ref)

    neg = jnp.full((1, bq), NEG_BIG, F32)

    def body_long(t, carry):
        ms, tops = carry
        qi = t // trips_per_q
        c0 = unroll * (t % trips_per_q)
        last = c0 + unroll == nk
        for j in range(unroll):
            if j + 1 < unroll:
                nxt_tops = scores(qi, c0 + j + 1, (j + 1) % 2)
            else:
                nxt_q = jnp.where(last, jnp.minimum(qi + 1, nq - 1), qi)
                nxt_c = jnp.where(last, 0, c0 + unroll)
                nxt_tops = scores(nxt_q, nxt_c, 0)
            ms = update(c0 + j, j % 2, tops, ms)
            tops = nxt_tops

        @pl.when(last)
        def _():
            finish(qi)

        ms = tuple(jnp.where(last, neg, m) for m in ms)
        return ms, tops

    def body_short(t, carry):
        ms, tops = carry
        q0 = t * (unroll // nk)
        for j in range(unroll):
            qi, c = q0 + j // nk, j % nk
            if j + 1 < unroll:
                nxt_tops = scores(q0 + (j + 1) // nk, (j + 1) % nk, (j + 1) % 2)
            else:
                nxt_tops = scores(jnp.minimum(q0 + unroll // nk, nq - 1), 0, 0)
            ms = update(c, j % 2, tops, ms)
            tops = nxt_tops
            if c == nk - 1:
                finish(qi)
                ms = (neg, neg)
        return ms, tops

    tops0 = scores(0, 0, 0)
    if unroll <= nk:
        trips_per_q = nk // unroll
        lax.fori_loop(0, nq * trips_per_q, body_long, ((neg, neg), tops0))
    else:
        lax.fori_loop(0, nq * nk // unroll, body_short, ((neg, neg), tops0))


def _attention(q, k, vt, lam_vecs, sub_col, lam_init, seq_len):
    H, T, hd = q.shape
    L = seq_len
    B = T // L
    hda = vt.shape[2]
    nk = L // ATT_BK
    bq = min(ATT_BQ, L)
    nq = L // bq
    if nk >= ATT_UNROLL:
        unroll = ATT_UNROLL
        assert nk % unroll == 0
    else:
        unroll = nk * min(nq, ATT_UNROLL_SHORT // nk)
        assert unroll >= nk and (nq * nk) % unroll == 0
    assert unroll % 2 == 0
    seq = lambda b, h: (h, b, 0)
    return pl.pallas_call(
        functools.partial(_attn_kernel, lam_init=lam_init, bq=bq, unroll=unroll),
        grid=(B, H),
        in_specs=[
            pl.BlockSpec(lam_vecs.shape, lambda b, h: (0, 0)),
            pl.BlockSpec(sub_col.shape, lambda b, h: (0, 0)),
            pl.BlockSpec((None, L, hd), seq),
            pl.BlockSpec((None, L, hd), seq),
            pl.BlockSpec((nk, None, hda, ATT_BK), lambda b, h: (b, h, 0, 0)),
        ],
        out_specs=pl.BlockSpec((None, L, hd), seq),
        out_shape=jax.ShapeDtypeStruct((H, T, hd), BF16),
        scratch_shapes=[
            pltpu.VMEM((nq, hd, bq), BF16),
            pltpu.VMEM((nq, hd, bq), BF16),
            pltpu.VMEM((hda, bq), F32),
            pltpu.VMEM((hda, bq), F32),
            pltpu.VMEM((2, 2, ATT_BK, bq), F32),
        ],
        compiler_params=_cparams(("arbitrary", "arbitrary")),
        name="diff_attn",
    )(lam_vecs, sub_col, q, k, vt)


def _ssm_kernel(*refs, cps):
    T = SSM_CHUNK
    u_refs = refs[:T]
    p_ref, pt_ref, w_ref, r_ref, v_ref, lre_ref, lim_ref, y_ref = refs[T:T + 8]
    ug_ref, yg_ref, m_ref, sre, sim, hfre, hfim, hbre, hbim = refs[T + 8:]
    o = pl.program_id(1)

    @pl.when(o == 0)
    def _():
        _ssm_compute(u_refs, p_ref, w_ref, r_ref, v_ref, lre_ref, lim_ref,
                     ug_ref, yg_ref, m_ref, sre, sim, hfre, hfim, hbre, hbim, cps)

    yo = yg_ref[o]
    hi = yo.astype(BF16)
    lo = (yo - hi.astype(F32)).astype(BF16)
    res = (jnp.dot(hi, pt_ref[...], preferred_element_type=F32)
           + jnp.dot(lo, pt_ref[...], preferred_element_type=F32))
    for t8 in range(SUBLANES):
        y_ref[:, t8, :] = res[:, t8 * LANES:(t8 + 1) * LANES]


def _ssm_compute(u_refs, p_ref, w_ref, r_ref, v_ref, lre_ref, lim_ref,
                 ug_ref, yg_ref, m_ref, sre, sim, hfre, hfim, hbre, hbim, cps):
    gb = ug_ref.shape[0]
    nc = ug_ref.shape[1]
    half = LANES // 2
    n_oct = len(u_refs) // SUBLANES

    for oc in range(n_oct):
        z = jnp.concatenate([u_refs[oc * SUBLANES + t8][...] for t8 in range(SUBLANES)], axis=1)
        zg = jnp.dot(z, p_ref[...], preferred_element_type=F32).astype(BF16)
        for gi in range(gb):
            ug_ref[gi, :, pl.ds(oc * LANES, LANES)] = zg[:, gi * LANES:(gi + 1) * LANES]

    T = len(u_refs)
    C = SSM_GROUP
    for gi in range(gb):
        r = r_ref[gi]
        for s_ in range(T):
            off = (T - 1 - s_) * C
            m_ref[gi, pl.ds(s_ * C, C), :] = r[:, off:off + T * C].astype(BF16)

    for gi in range(gb):
        s = jnp.dot(ug_ref[gi], w_ref[gi], preferred_element_type=F32)
        sre[pl.ds(gi, nc, stride=gb), :] = s[:, :LANES]
        sim[pl.ds(gi, nc, stride=gb), :] = s[:, LANES:]

    zero = jnp.zeros((gb, LANES), F32)
    lre = lre_ref[...]
    lim = lim_ref[...]
    fwd_lane = lax.broadcasted_iota(jnp.int32, (gb, LANES), 1) < half

    def body(j, carry):
        hr, hi = carry
        fresh = (j % cps) == 0
        hr = jnp.where(fresh, 0.0, hr)
        hi = jnp.where(fresh, 0.0, hi)
        jf = pl.multiple_of(j * gb, gb)
        jb = pl.multiple_of((nc - 1 - j) * gb, gb)
        hfre[pl.ds(jf, gb), :] = hr
        hfim[pl.ds(jf, gb), :] = hi
        hbre[pl.ds(jb, gb), :] = hr
        hbim[pl.ds(jb, gb), :] = hi
        sr = jnp.where(fwd_lane, sre[pl.ds(jf, gb), :], sre[pl.ds(jb, gb), :])
        si = jnp.where(fwd_lane, sim[pl.ds(jf, gb), :], sim[pl.ds(jb, gb), :])
        return lre * hr - lim * hi + sr, lre * hi + lim * hr + si

    lax.fori_loop(0, nc, body, (zero, zero))

    fwd_col = lax.broadcasted_iota(jnp.int32, (nc, LANES), 1) < half
    for gi in range(gb):
        hr = jnp.where(fwd_col, hfre[pl.ds(gi, nc, stride=gb), :], hbre[pl.ds(gi, nc, stride=gb), :])
        hi = jnp.where(fwd_col, hfim[pl.ds(gi, nc, stride=gb), :], hbim[pl.ds(gi, nc, stride=gb), :])
        h = jnp.concatenate([hr, hi], axis=1).astype(BF16)
        y = jnp.dot(ug_ref[gi], m_ref[gi], preferred_element_type=F32)
        y = y + lax.dot_general(h, v_ref[gi], (((1,), (1,)), ((), ())),
                                preferred_element_type=F32)
        for oc in range(n_oct):
            yg_ref[oc, :, pl.ds(gi * LANES, LANES)] = y[:, oc * LANES:(oc + 1) * LANES]


def _lane_regroup_matrix():
    n = SUBLANES * LANES
    i = jnp.arange(n)
    t8, gl, c = i // LANES, (i % LANES) // SSM_GROUP, i % SSM_GROUP
    dest = gl * LANES + t8 * SSM_GROUP + c
    return (dest[:, None] == i[None, :]).astype(BF16)


def _ssm(ub, w_t, r_t, v_t, lre, lim, seq_len):
    Ttok, Wd = ub.shape
    T = SSM_CHUNK
    nc = Ttok // T
    cps = seq_len // T
    gb = SSM_GBLK
    assert gb * SSM_GROUP == LANES and T % SUBLANES == 0 and nc % cps == 0
    n_slab = Wd // LANES
    n_oct = T // SUBLANES
    K = T * SSM_GROUP
    ncols = w_t.shape[2]
    pmat = _lane_regroup_matrix()
    u2 = ub.reshape(nc, T * Wd)
    once = pl.Buffered(1)
    fixed2 = lambda g, o: (0, 0)
    per_slab = lambda g, o: (g, 0, 0)

    def u_spec(t):
        return pl.BlockSpec((nc, LANES), lambda g, o: (0, t * n_slab + g))

    y = pl.pallas_call(
        functools.partial(_ssm_kernel, cps=cps),
        grid=(n_slab, n_oct),
        in_specs=[u_spec(t) for t in range(T)] + [
            pl.BlockSpec(pmat.shape, fixed2, pipeline_mode=once),
            pl.BlockSpec(pmat.shape, fixed2, pipeline_mode=once),
            pl.BlockSpec((gb, K, ncols), per_slab, pipeline_mode=once),
            pl.BlockSpec((gb,) + r_t.shape[1:], per_slab, pipeline_mode=once),
            pl.BlockSpec((gb, K, ncols), per_slab, pipeline_mode=once),
            pl.BlockSpec((gb, LANES), lambda g, o: (g, 0)),
            pl.BlockSpec((gb, LANES), lambda g, o: (g, 0)),
        ],
        out_specs=pl.BlockSpec((nc, SUBLANES, LANES), lambda g, o: (0, o, g)),
        out_shape=jax.ShapeDtypeStruct((nc, T, Wd), F32),
        scratch_shapes=[
            pltpu.VMEM((gb, nc, K), BF16),
            pltpu.VMEM((n_oct, nc, SUBLANES * LANES), F32),
            pltpu.VMEM((gb, K, K), BF16),
        ] + [pltpu.VMEM((nc * gb, LANES), F32)] * 6,
        compiler_params=_cparams(("arbitrary", "arbitrary")),
        name="s5_chunked",
    )(*([u2] * T), pmat, pmat.T, w_t, r_t, v_t, lre, lim)
    return y.reshape(Ttok, Wd)


def _ssm_tables(a_re, a_im, log_dt, b_re, b_im, c_re, c_im):
    T = SSM_CHUNK
    G, P = a_re.shape[1], a_re.shape[2]
    C = b_re.shape[3]
    dt = jnp.exp(log_dt)[..., None]
    zr = a_re * dt
    zi = a_im * dt
    tau = jnp.arange(T + 1, dtype=F32)[:, None]
    mag = jnp.exp(zr[:, :, None] * tau)
    pr = mag * jnp.cos(zi[:, :, None] * tau)
    pi = mag * jnp.sin(zi[:, :, None] * tau)
    ab_re, ab_im = pr[:, :, 1], pi[:, :, 1]
    den = a_re * a_re + a_im * a_im
    nr = ab_re - 1.0
    f_re = ((nr * a_re + ab_im * a_im) / den)[:, :, None, :]
    f_im = ((ab_im * a_re - nr * a_im) / den)[:, :, None, :]
    bt_re = b_re.transpose(0, 1, 3, 2)
    bt_im = b_im.transpose(0, 1, 3, 2)
    bbt_re = f_re * bt_re - f_im * bt_im
    bbt_im = f_re * bt_im + f_im * bt_re

    def rows(pw_re, pw_im, m_re, m_im, sign):
        pw_re = jnp.concatenate(pw_re, axis=-1)[:, :, None, :]
        pw_im = jnp.concatenate(pw_im, axis=-1)[:, :, None, :]
        m_re = jnp.concatenate([m_re[0], m_re[1]], axis=-1)[:, None]
        m_im = jnp.concatenate([m_im[0], m_im[1]], axis=-1)[:, None]
        re = pw_re * m_re - pw_im * m_im
        im = pw_re * m_im + pw_im * m_re
        return jnp.concatenate([re, sign * im], axis=-1).astype(BF16).reshape(G, T * C, 4 * P)

    w_t = rows([pr[0, :, T - 1::-1], pr[1, :, :T]], [pi[0, :, T - 1::-1], pi[1, :, :T]], bbt_re, bbt_im, 1.0)
    vt_t = rows([pr[0, :, 1:], pr[1, :, :0:-1]], [pi[0, :, 1:], pi[1, :, :0:-1]], c_re, c_im, -1.0)

    p0r = pr[:, :, :T, None, :]
    p0i = pi[:, :, :T, None, :]
    cp_re = c_re[:, :, None] * p0r - c_im[:, :, None] * p0i
    cp_im = c_re[:, :, None] * p0i + c_im[:, :, None] * p0r
    cb = jnp.einsum('dgtcq,dgkq->dgtck', jnp.concatenate([cp_re, -cp_im], axis=-1),
                    jnp.concatenate([bbt_re, bbt_im], axis=-1), precision=lax.Precision.HIGH)

    kern = jnp.concatenate([cb[1, :, T - 1:0:-1], (cb[0, :, 0] + cb[1, :, 0])[:, None], cb[0, :, 1:],
                            jnp.zeros_like(cb[0, :, :1])], axis=1)
    r_t = kern.transpose(0, 3, 1, 2).reshape(G, C, 2 * T * C)

    lre = jnp.concatenate([pr[0, :, T], pr[1, :, T]], axis=1)
    lim = jnp.concatenate([pi[0, :, T], pi[1, :, T]], axis=1)
    return w_t, r_t, vt_t, lre, lim


def _layer_norm(r, g, b):
    mu = jnp.mean(r, axis=-1, keepdims=True)
    d = r - mu
    var = jnp.mean(d * d, axis=-1, keepdims=True)
    return d * lax.rsqrt(var + LN_EPS) * g + b


def _mix_kernel(x_ref, a_ref, y_ref, u_ref, d_ref, wglu_ref, wout_ref, g_ref, b_ref, o_ref, ob_ref,
                *, alpha):
    n_heads, tm, hd = a_ref.shape
    wa = n_heads * hd
    for half in range(2):
        rows = pl.ds(half * (tm // 2), tm // 2)
        yy = y_ref[rows, :] + d_ref[...] * u_ref[rows, :]
        z = jax.nn.gelu(yy)
        gate = jax.nn.sigmoid(jnp.dot(z.astype(BF16), wglu_ref[...], preferred_element_type=F32))
        s = (z * gate).astype(BF16)
        a = jnp.concatenate([a_ref[h, rows, :] for h in range(n_heads)], axis=1)
        mixed = jnp.dot(a, wout_ref[pl.ds(0, wa), :], preferred_element_type=F32)
        mixed = mixed + jnp.dot(s, wout_ref[pl.ds(wa, s.shape[1]), :], preferred_element_type=F32)
        r = alpha * x_ref[rows, :] + mixed
        o = _layer_norm(r, g_ref[...], b_ref[...])
        o_ref[rows, :] = o
        ob_ref[rows, :] = o.astype(BF16)


def _mix(x2d, attn, y, u, d, wglu, wout, g, b, alpha, tm=512):
    T, D = x2d.shape
    W = y.shape[1]
    row = lambda i: (i, 0)
    fixed = lambda i: (0, 0)
    once = pl.Buffered(1)
    return pl.pallas_call(
        functools.partial(_mix_kernel, alpha=alpha),
        grid=(T // tm,),
        in_specs=[
            pl.BlockSpec((tm, D), row),
            pl.BlockSpec((attn.shape[0], tm, attn.shape[2]), lambda i: (0, i, 0)),
            pl.BlockSpec((tm, W), row),
            pl.BlockSpec((tm, W), row),
            pl.BlockSpec((1, W), fixed),
            pl.BlockSpec(wglu.shape, fixed, pipeline_mode=once),
            pl.BlockSpec(wout.shape, fixed, pipeline_mode=once),
            pl.BlockSpec((1, D), fixed),
            pl.BlockSpec((1, D), fixed),
        ],
        out_specs=[pl.BlockSpec((tm, D), row), pl.BlockSpec((tm, D), row)],
        out_shape=[jax.ShapeDtypeStruct((T, D), F32), jax.ShapeDtypeStruct((T, D), BF16)],
        compiler_params=_cparams(("arbitrary",)),
        name="glu_outproj_ln",
    )(x2d, attn, y, u, d, wglu, wout, g, b)


def _ffn_up_kernel(x_ref, wg_ref, wu_ref, h_ref):
    x = x_ref[...]
    g = jnp.dot(x, wg_ref[...], preferred_element_type=F32)
    up = jnp.dot(x, wu_ref[...], preferred_element_type=F32)
    h_ref[...] = (jax.nn.silu(g) * up).astype(h_ref.dtype)


def _ffn_up(xb, wg, wu, tm=1024, tf=512):
    T, D = xb.shape
    F = wg.shape[1]
    return pl.pallas_call(
        _ffn_up_kernel,
        grid=(T // tm, F // tf),
        in_specs=[
            pl.BlockSpec((tm, D), lambda i, j: (i, 0)),
            pl.BlockSpec((D, tf), lambda i, j: (0, j)),
            pl.BlockSpec((D, tf), lambda i, j: (0, j)),
        ],
        out_specs=pl.BlockSpec((tm, tf), lambda i, j: (i, j)),
        out_shape=jax.ShapeDtypeStruct((T, F), BF16),
        compiler_params=_cparams(("arbitrary", "arbitrary")),
        name="ffn_up",
    )(xb, wg, wu)


def _ffn_down_kernel(h_ref, wd_ref, x_ref, g_ref, b_ref, o_ref, *, alpha):
    down = jnp.dot(h_ref[...], wd_ref[...], preferred_element_type=F32)
    o_ref[...] = _layer_norm(alpha * x_ref[...] + down, g_ref[...], b_ref[...])


def _ffn_down(h, wd, x1, g, b, alpha, tm=256):
    T, F = h.shape
    D = wd.shape[1]
    return pl.pallas_call(
        functools.partial(_ffn_down_kernel, alpha=alpha),
        grid=(T // tm,),
        in_specs=[
            pl.BlockSpec((tm, F), lambda i: (i, 0)),
            pl.BlockSpec((F, D), lambda i: (0, 0), pipeline_mode=pl.Buffered(1)),
            pl.BlockSpec((tm, D), lambda i: (i, 0)),
            pl.BlockSpec((1, D), lambda i: (0, 0)),
            pl.BlockSpec((1, D), lambda i: (0, 0)),
        ],
        out_specs=pl.BlockSpec((tm, D), lambda i: (i, 0)),
        out_shape=jax.ShapeDtypeStruct((T, D), F32),
        compiler_params=_cparams(("arbitrary",)),
        name="ffn_down_ln",
    )(h, wd, x1, g, b)


def _rope_tables(L):
    half = ROT_DIM // 2
    inv = ROPE_THETA ** (-jnp.arange(0, ROT_DIM, 2, dtype=F32) / ROT_DIM)
    ang = jnp.arange(L, dtype=F32)[:, None] * inv[None, :]
    cos, sin = jnp.cos(ang), jnp.sin(ang)
    lane = jnp.arange(LANES)
    d = lane % DIFF_HEAD_DIM
    freq = jnp.arange(half)[:, None]
    first = ((d < half)[None, :] & (d[None, :] == freq)).astype(F32)
    second = (((d >= half) & (d < ROT_DIM))[None, :] & (d[None, :] - half == freq)).astype(F32)
    spread = functools.partial(jnp.dot, precision=lax.Precision.HIGHEST)
    c = spread(cos, first + second) + (d >= ROT_DIM).astype(F32)[None, :]
    sa = -spread(sin, first)
    sb = spread(sin, second)
    tab = jnp.stack([c, sa, sb])
    return jnp.stack([tab * (DIFF_HEAD_DIM ** -0.5 * math.log2(math.e)), tab])


def _prepare_layer(p):
    (w_in, lq1, lk1, lq2, lk2, subln_w, a_re, a_im, log_dt, b_re, b_im, c_re, c_im, ssm_d, w_glu,
     w_out, ln1_g, ln1_b, w_gate, w_up, w_down, ln2_g, ln2_b) = p
    W = w_glu.shape[0]
    D = w_in.shape[0]
    row = lambda t, n: t.reshape(1, n).astype(F32)
    return dict(
        w_qku=jnp.concatenate([w_in[:, :2 * W], w_in[:, 3 * W:]], axis=1).astype(BF16),
        wvt=w_in[:, 2 * W:3 * W].T.astype(BF16),
        lam_vecs=jnp.stack([lq1, lk1, lq2, lk2]).astype(F32),
        sub_col=subln_w.astype(F32).reshape(-1, 1),
        ssm=_ssm_tables(a_re, a_im, log_dt, b_re, b_im, c_re, c_im),
        ssm_d=row(ssm_d, W), w_glu=w_glu.astype(BF16), w_out=w_out.astype(BF16),
        ln1_g=row(ln1_g, D), ln1_b=row(ln1_b, D),
        w_gate=w_gate.astype(BF16), w_up=w_up.astype(BF16), w_down=w_down.astype(BF16),
        ln2_g=row(ln2_g, D), ln2_b=row(ln2_b, D),
    )


def _encoder_layer(x, layer_idx, depth, pp):
    B, L, D = x.shape
    T = B * L
    W = pp["w_glu"].shape[0]
    alpha = (2 * depth) ** 0.25
    lam_init = 0.8 - 0.6 * math.exp(-0.3 * layer_idx)

    x2d = x.reshape(T, D)
    q, k, vt, u, ub = _inproj(x2d.astype(BF16), pp["w_qku"], pp["wvt"], _rope_tables(L), L)
    attn = _attention(q, k, vt, pp["lam_vecs"], pp["sub_col"], lam_init, L)
    y = _ssm(ub, *pp["ssm"], L)
    x1, x1b = _mix(x2d, attn, y, u, pp["ssm_d"], pp["w_glu"], pp["w_out"], pp["ln1_g"], pp["ln1_b"], alpha)
    h = _ffn_up(x1b, pp["w_gate"], pp["w_up"])
    out = _ffn_down(h, pp["w_down"], x1, pp["ln2_g"], pp["ln2_b"], alpha)
    return out.reshape(B, L, D)


def kernel(x_prompt, x_sample, w_in, lambda_q1, lambda_k1, lambda_q2, lambda_k2, subln_w, ssm_a_re, ssm_a_im, ssm_log_dt, ssm_b_re, ssm_b_im, ssm_c_re, ssm_c_im, ssm_d, w_glu, w_out, ln1_g, ln1_b, w_gate, w_up, w_down, ln2_g, ln2_b):
    params = (w_in, lambda_q1, lambda_k1, lambda_q2, lambda_k2, subln_w,
              ssm_a_re, ssm_a_im, ssm_log_dt, ssm_b_re, ssm_b_im, ssm_c_re, ssm_c_im, ssm_d, w_glu,
              w_out, ln1_g, ln1_b, w_gate, w_up, w_down, ln2_g, ln2_b)
    depth = w_in.shape[0]
    y_prompt, y_sample = x_prompt, x_sample
    for layer in range(depth):
        lp = _prepare_layer(tuple(t[layer] for t in params))
        y_prompt = _encoder_layer(y_prompt, layer, depth, lp)
        y_sample = _encoder_layer(y_sample, layer, depth, lp)
    return (y_prompt, y_sample)
```

```python
import functools
import math

import jax
import jax.numpy as jnp
from jax import lax
from jax.experimental import pallas as pl
from jax.experimental.pallas import tpu as pltpu

F32 = jnp.float32
BF16 = jnp.bfloat16

DIFF_HEAD_DIM = 64
ROT_DIM = DIFF_HEAD_DIM // 4
ROPE_THETA = 500000.0
SSM_GROUP = 16
SSM_STATE = 64
LN_EPS = 1e-5

LANES = 128
SUBLANES = 8
VMEM_LIMIT_BYTES = 56 * 1024 * 1024

SSM_CHUNK = 32
SSM_GBLK = SUBLANES
ATT_BK = 512
ATT_BQ = 512
ATT_UNROLL = 8
ATT_UNROLL_SHORT = 8
ATT_ONES_ROWS = 16
NEG_BIG = -1e30


def _cparams(sem):
    return pltpu.CompilerParams(dimension_semantics=sem, vmem_limit_bytes=VMEM_LIMIT_BYTES)


def _inproj_kernel(xb_ref, w_ref, wvt_ref, rope_ref, perm_ref, q_ref, k_ref, vt_ref, u_ref, ub_ref):
    j = pl.program_id(1)

    def rope_to(out_ref):
        acc = jnp.dot(xb_ref[...], w_ref[...], preferred_element_type=F32)
        c = rope_ref[0, 0]
        sa = rope_ref[0, 1]
        sb = rope_ref[0, 2]
        for cb in range(acc.shape[1] // LANES):
            xs = acc[:, cb * LANES:(cb + 1) * LANES]
            nxt = pltpu.roll(xs, LANES - ROT_DIM // 2, axis=1)
            prv = pltpu.roll(xs, ROT_DIM // 2, axis=1)
            out_ref[cb] = (xs * c + nxt * sa + prv * sb).astype(out_ref.dtype)

    @pl.when(j == 0)
    def _():
        rope_to(q_ref)

    @pl.when(j == 1)
    def _():
        rope_to(k_ref)

    @pl.when(j == 2)
    def _():
        vt = lax.dot_general(wvt_ref[...], xb_ref[...], (((1,), (1,)), ((), ())),
                             preferred_element_type=F32)
        hd = 2 * DIFF_HEAD_DIM
        ones = jnp.ones((vt_ref.shape[2] - hd, ATT_BK), vt_ref.dtype)
        for cc in range(vt_ref.shape[0]):
            for h in range(vt_ref.shape[1]):
                vt_ref[cc, h, pl.ds(0, hd), :] = (
                    vt[h * hd:(h + 1) * hd, cc * ATT_BK:(cc + 1) * ATT_BK].astype(vt_ref.dtype))
                vt_ref[cc, h, pl.ds(hd, ones.shape[0]), :] = ones

    @pl.when(j == 3)
    def _():
        u = jnp.dot(xb_ref[...], w_ref[...], preferred_element_type=F32)
        u_ref[...] = u
        ut = jnp.dot(perm_ref[...], u.astype(BF16), preferred_element_type=F32).astype(ub_ref.dtype)
        T = SSM_CHUNK
        nch = ub_ref.shape[0]
        for slab in range(u.shape[1] // LANES):
            for t in range(T):
                ub_ref[:, pl.ds((slab * T + t) * LANES, LANES)] = (
                    ut[t * nch:(t + 1) * nch, slab * LANES:(slab + 1) * LANES])


def _inproj(xb2d, w_qku, wvt, rope, seq_len, tm=1024):
    T, D = xb2d.shape
    W = wvt.shape[0]
    hd = 2 * DIFF_HEAD_DIM
    H = W // hd
    hda = hd + ATT_ONES_ROWS
    nseq_blk = seq_len // tm
    grid = (T // tm, 4)
    nch = tm // SSM_CHUNK
    r = jnp.arange(tm)
    perm = ((r % nch) * SSM_CHUNK + r // nch)[:, None] == r[None, :]
    perm = perm.astype(BF16)
    wmap = lambda i, j: (0, jnp.where(j >= 2, j - 1, j))
    return pl.pallas_call(
        _inproj_kernel,
        grid=grid,
        in_specs=[
            pl.BlockSpec((tm, D), lambda i, j: (i, 0)),
            pl.BlockSpec((D, W), wmap),
            pl.BlockSpec((W, D), lambda i, j: (0, 0), pipeline_mode=pl.Buffered(1)),
            pl.BlockSpec((1, 3, tm, LANES), lambda i, j: (jnp.minimum(j, 1), 0, i % nseq_blk, 0)),
            pl.BlockSpec((tm, tm), lambda i, j: (0, 0), pipeline_mode=pl.Buffered(1)),
        ],
        out_specs=[
            pl.BlockSpec((H, tm, hd), lambda i, j: (0, i, 0)),
            pl.BlockSpec((H, tm, hd), lambda i, j: (0, i, 0)),
            pl.BlockSpec((tm // ATT_BK, H, hda, ATT_BK), lambda i, j: (i, 0, 0, 0)),
            pl.BlockSpec((tm, W), lambda i, j: (i, 0)),
            pl.BlockSpec((nch, SSM_CHUNK * W), lambda i, j: (i, 0)),
        ],
        out_shape=[
            jax.ShapeDtypeStruct((H, T, hd), BF16),
            jax.ShapeDtypeStruct((H, T, hd), BF16),
            jax.ShapeDtypeStruct((T // ATT_BK, H, hda, ATT_BK), BF16),
            jax.ShapeDtypeStruct((T, W), F32),
            jax.ShapeDtypeStruct((T // SSM_CHUNK, SSM_CHUNK * W), BF16),
        ],
        compiler_params=_cparams(("arbitrary", "arbitrary")),
        name="inproj",
    )(xb2d, w_qku, wvt, rope, perm)


def _attn_kernel(lam_ref, sub_ref, q_ref, k_ref, vt_ref, o_ref, q1t_ref, q2t_ref, o1_ref, o2_ref, s_ref,
                 *, lam_init, bq, unroll):
    nk = vt_ref.shape[0]
    L, hd = q_ref.shape
    nq = L // bq

    lv = lam_ref[...]
    d1 = jnp.sum(lv[0:1] * lv[1:2], axis=1, keepdims=True)
    d2 = jnp.sum(lv[2:3] * lv[3:4], axis=1, keepdims=True)
    lam = jnp.exp(d1) - jnp.exp(d2) + lam_init

    def transpose_q(qi, carry):
        qt = q_ref[pl.ds(pl.multiple_of(qi * bq, bq), bq), :].astype(F32).T
        row = lax.broadcasted_iota(jnp.int32, qt.shape, 0)
        q1t_ref[qi] = jnp.where(row < DIFF_HEAD_DIM, qt, 0.0).astype(BF16)
        q2t_ref[qi] = jnp.where(row >= DIFF_HEAD_DIM, qt, 0.0).astype(BF16)
        return carry

    lax.fori_loop(0, nq, transpose_q, 0)
    o1_ref[...] = jnp.zeros_like(o1_ref)
    o2_ref[...] = jnp.zeros_like(o2_ref)

    def scores(qi, c, slot):
        kb = k_ref[pl.ds(pl.multiple_of(c * ATT_BK, ATT_BK), ATT_BK), :]
        tops = []
        for br, qt_ref in enumerate((q1t_ref, q2t_ref)):
            s = jnp.dot(kb, qt_ref[qi], preferred_element_type=F32)
            s_ref[slot, br] = s
            tops.append(jnp.max(s, axis=0, keepdims=True))
        return tuple(tops)

    def update(c, slot, tops, ms):
        vtb = vt_ref[c]
        new_ms = []
        for br, o_acc in enumerate((o1_ref, o2_ref)):
            m_new = jnp.maximum(ms[br], tops[br])
            alpha = jnp.exp2(ms[br] - m_new)
            p = jnp.exp2(s_ref[slot, br] - m_new).astype(BF16)
            pv = jnp.dot(vtb, p, preferred_element_type=F32)
            o_acc[...] = alpha * o_acc[...] + pv
            new_ms.append(m_new)
        return tuple(new_ms)

    def finish(qi):
        l1 = o1_ref[pl.ds(hd, 1), :]
        l2 = o2_ref[pl.ds(hd, 1), :]
        o = o1_ref[pl.ds(0, hd), :] / l1 - lam * (o2_ref[pl.ds(0, hd), :] / l2)
        msq = jnp.mean(o * o, axis=0, keepdims=True)
        o = o * lax.rsqrt(msq + LN_EPS) * sub_ref[...]
        o = o * (1.0 - lam_init)
        o_ref[pl.ds(pl.multiple_of(qi * bq, bq), bq), :] = o.T.astype(o_ref.dtype)
        o1_ref[...] = jnp.zeros_like(o1_ref)
        o2_ref[...] = jnp.zeros_like(o2_ref)

    neg = jnp.full((1, bq), NEG_BIG, F32)

    def body_long(t, carry):
        ms, tops = carry
        qi = t // trips_per_q
        c0 = unroll * (t % trips_per_q)
        last = c0 + unroll == nk
        for j in range(unroll):
            if j + 1 < unroll:
                nxt_tops = scores(qi, c0 + j + 1, (j + 1) % 2)
            else:
                nxt_q = jnp.where(last, jnp.minimum(qi + 1, nq - 1), qi)
                nxt_c = jnp.where(last, 0, c0 + unroll)
                nxt_tops = scores(nxt_q, nxt_c, 0)
            ms = update(c0 + j, j % 2, tops, ms)
            tops = nxt_tops

        @pl.when(last)
        def _():
            finish(qi)

        ms = tuple(jnp.where(last, neg, m) for m in ms)
        return ms, tops

    def body_short(t, carry):
        ms, tops = carry
        q0 = t * (unroll // nk)
        for j in range(unroll):
            qi, c = q0 + j // nk, j % nk
            if j + 1 < unroll:
                nxt_tops = scores(q0 + (j + 1) // nk, (j + 1) % nk, (j + 1) % 2)
            else:
                nxt_tops = scores(jnp.minimum(q0 + unroll // nk, nq - 1), 0, 0)
            ms = update(c, j % 2, tops, ms)
            tops = nxt_tops
            if c == nk - 1:
                finish(qi)
                ms = (neg, neg)
        return ms, tops

    tops0 = scores(0, 0, 0)
    if unroll <= nk:
        trips_per_q = nk // unroll
        lax.fori_loop(0, nq * trips_per_q, body_long, ((neg, neg), tops0))
    else:
        lax.fori_loop(0, nq * nk // unroll, body_short, ((neg, neg), tops0))


def _attention(q, k, vt, lam_vecs, sub_col, lam_init, seq_len):
    H, T, hd = q.shape
    L = seq_len
    B = T // L
    hda = vt.shape[2]
    nk = L // ATT_BK
    bq = min(ATT_BQ, L)
    nq = L // bq
    if nk >= ATT_UNROLL:
        unroll = ATT_UNROLL
        assert nk % unroll == 0
    else:
        unroll = nk * min(nq, ATT_UNROLL_SHORT // nk)
        assert unroll >= nk and (nq * nk) % unroll == 0
    assert unroll % 2 == 0
    seq = lambda b, h: (h, b, 0)
    return pl.pallas_call(
        functools.partial(_attn_kernel, lam_init=lam_init, bq=bq, unroll=unroll),
        grid=(B, H),
        in_specs=[
            pl.BlockSpec(lam_vecs.shape, lambda b, h: (0, 0)),
            pl.BlockSpec(sub_col.shape, lambda b, h: (0, 0)),
            pl.BlockSpec((None, L, hd), seq),
            pl.BlockSpec((None, L, hd), seq),
            pl.BlockSpec((nk, None, hda, ATT_BK), lambda b, h: (b, h, 0, 0)),
        ],
        out_specs=pl.BlockSpec((None, L, hd), seq),
        out_shape=jax.ShapeDtypeStruct((H, T, hd), BF16),
        scratch_shapes=[
            pltpu.VMEM((nq, hd, bq), BF16),
            pltpu.VMEM((nq, hd, bq), BF16),
            pltpu.VMEM((hda, bq), F32),
            pltpu.VMEM((hda, bq), F32),
            pltpu.VMEM((2, 2, ATT_BK, bq), F32),
        ],
        compiler_params=_cparams(("arbitrary", "arbitrary")),
        name="diff_attn",
    )(lam_vecs, sub_col, q, k, vt)


def _ssm_kernel(u_ref, p_ref, pt_ref, w_ref, r_ref, v_ref, lre_ref, lim_ref, y_ref,
                ug_ref, yg_ref, m_ref, sre, sim, hfre, hfim, hbre, hbim, *, cps):
    o = pl.program_id(1)

    @pl.when(o == 0)
    def _():
        _ssm_compute(u_ref, p_ref, w_ref, r_ref, v_ref, lre_ref, lim_ref,
                     ug_ref, yg_ref, m_ref, sre, sim, hfre, hfim, hbre, hbim, cps)

    yo = yg_ref[o]
    hi = yo.astype(BF16)
    lo = (yo - hi.astype(F32)).astype(BF16)
    res = (jnp.dot(hi, pt_ref[...], preferred_element_type=F32)
           + jnp.dot(lo, pt_ref[...], preferred_element_type=F32))
    for t8 in range(SUBLANES):
        y_ref[:, t8, :] = res[:, t8 * LANES:(t8 + 1) * LANES]


def _ssm_compute(u_ref, p_ref, w_ref, r_ref, v_ref, lre_ref, lim_ref,
                 ug_ref, yg_ref, m_ref, sre, sim, hfre, hfim, hbre, hbim, cps):
    gb = ug_ref.shape[0]
    nc = ug_ref.shape[1]
    half = LANES // 2
    T = SSM_CHUNK
    n_oct = T // SUBLANES
    oct_lanes = SUBLANES * LANES

    for oc in range(n_oct):
        z = u_ref[:, pl.ds(oc * oct_lanes, oct_lanes)]
        zg = jnp.dot(z, p_ref[...], preferred_element_type=F32).astype(BF16)
        for gi in range(gb):
            ug_ref[gi, :, pl.ds(oc * LANES, LANES)] = zg[:, gi * LANES:(gi + 1) * LANES]

    C = SSM_GROUP
    for gi in range(gb):
        r = r_ref[gi]
        for s_ in range(T):
            off = (T - 1 - s_) * C
            m_ref[gi, pl.ds(s_ * C, C), :] = r[:, off:off + T * C].astype(BF16)

    for gi in range(gb):
        s = jnp.dot(ug_ref[gi], w_ref[gi], preferred_element_type=F32)
        sre[pl.ds(gi, nc, stride=gb), :] = s[:, :LANES]
        sim[pl.ds(gi, nc, stride=gb), :] = s[:, LANES:]

    zero = jnp.zeros((gb, LANES), F32)
    lre = lre_ref[...]
    lim = lim_ref[...]
    fwd_lane = lax.broadcasted_iota(jnp.int32, (gb, LANES), 1) < half

    def body(j, carry):
        hr, hi = carry
        fresh = (j % cps) == 0
        hr = jnp.where(fresh, 0.0, hr)
        hi = jnp.where(fresh, 0.0, hi)
        jf = pl.multiple_of(j * gb, gb)
        jb = pl.multiple_of((nc - 1 - j) * gb, gb)
        hfre[pl.ds(jf, gb), :] = hr
        hfim[pl.ds(jf, gb), :] = hi
        hbre[pl.ds(jb, gb), :] = hr
        hbim[pl.ds(jb, gb), :] = hi
        sr = jnp.where(fwd_lane, sre[pl.ds(jf, gb), :], sre[pl.ds(jb, gb), :])
        si = jnp.where(fwd_lane, sim[pl.ds(jf, gb), :], sim[pl.ds(jb, gb), :])
        return lre * hr - lim * hi + sr, lre * hi + lim * hr + si

    lax.fori_loop(0, nc, body, (zero, zero))

    fwd_col = lax.broadcasted_iota(jnp.int32, (nc, LANES), 1) < half
    for gi in range(gb):
        hr = jnp.where(fwd_col, hfre[pl.ds(gi, nc, stride=gb), :], hbre[pl.ds(gi, nc, stride=gb), :])
        hi = jnp.where(fwd_col, hfim[pl.ds(gi, nc, stride=gb), :], hbim[pl.ds(gi, nc, stride=gb), :])
        h = jnp.concatenate([hr, hi], axis=1).astype(BF16)
        y = jnp.dot(ug_ref[gi], m_ref[gi], preferred_element_type=F32)
        y = y + lax.dot_general(h, v_ref[gi], (((1,), (1,)), ((), ())),
                                preferred_element_type=F32)
        for oc in range(n_oct):
            yg_ref[oc, :, pl.ds(gi * LANES, LANES)] = y[:, oc * LANES:(oc + 1) * LANES]


def _lane_regroup_matrix():
    n = SUBLANES * LANES
    i = jnp.arange(n)
    t8, gl, c = i // LANES, (i % LANES) // SSM_GROUP, i % SSM_GROUP
    dest = gl * LANES + t8 * SSM_GROUP + c
    return (dest[:, None] == i[None, :]).astype(BF16)


def _ssm(ub, w_t, r_t, v_t, lre, lim, seq_len):
    T = SSM_CHUNK
    nc = ub.shape[0]
    Wd = ub.shape[1] // T
    Ttok = nc * T
    cps = seq_len // T
    gb = SSM_GBLK
    assert gb * SSM_GROUP == LANES and T % SUBLANES == 0 and nc % cps == 0
    n_slab = Wd // LANES
    n_oct = T // SUBLANES
    K = T * SSM_GROUP
    ncols = w_t.shape[2]
    pmat = _lane_regroup_matrix()
    once = pl.Buffered(1)
    fixed2 = lambda g, o: (0, 0)
    per_slab = lambda g, o: (g, 0, 0)

    y = pl.pallas_call(
        functools.partial(_ssm_kernel, cps=cps),
        grid=(n_slab, n_oct),
        in_specs=[
            pl.BlockSpec((nc, T * LANES), lambda g, o: (0, g)),
            pl.BlockSpec(pmat.shape, fixed2, pipeline_mode=once),
            pl.BlockSpec(pmat.shape, fixed2, pipeline_mode=once),
            pl.BlockSpec((gb, K, ncols), per_slab, pipeline_mode=once),
            pl.BlockSpec((gb,) + r_t.shape[1:], per_slab, pipeline_mode=once),
            pl.BlockSpec((gb, K, ncols), per_slab, pipeline_mode=once),
            pl.BlockSpec((gb, LANES), lambda g, o: (g, 0)),
            pl.BlockSpec((gb, LANES), lambda g, o: (g, 0)),
        ],
        out_specs=pl.BlockSpec((nc, SUBLANES, LANES), lambda g, o: (0, o, g)),
        out_shape=jax.ShapeDtypeStruct((nc, T, Wd), F32),
        scratch_shapes=[
            pltpu.VMEM((gb, nc, K), BF16),
            pltpu.VMEM((n_oct, nc, SUBLANES * LANES), F32),
            pltpu.VMEM((gb, K, K), BF16),
        ] + [pltpu.VMEM((nc * gb, LANES), F32)] * 6,
        compiler_params=_cparams(("arbitrary", "arbitrary")),
        name="s5_chunked",
    )(ub, pmat, pmat.T, w_t, r_t, v_t, lre, lim)
    return y.reshape(Ttok, Wd)


def _ssm_tables(a_re, a_im, log_dt, b_re, b_im, c_re, c_im):
    T = SSM_CHUNK
    G, P = a_re.shape[1], a_re.shape[2]
    C = b_re.shape[3]
    dt = jnp.exp(log_dt)[..., None]
    zr = a_re * dt
    zi = a_im * dt
    tau = jnp.arange(T + 1, dtype=F32)[:, None]
    mag = jnp.exp(zr[:, :, None] * tau)
    pr = mag * jnp.cos(zi[:, :, None] * tau)
    pi = mag * jnp.sin(zi[:, :, None] * tau)
    ab_re, ab_im = pr[:, :, 1], pi[:, :, 1]
    den = a_re * a_re + a_im * a_im
    nr = ab_re - 1.0
    f_re = ((nr * a_re + ab_im * a_im) / den)[:, :, None, :]
    f_im = ((ab_im * a_re - nr * a_im) / den)[:, :, None, :]
    bt_re = b_re.transpose(0, 1, 3, 2)
    bt_im = b_im.transpose(0, 1, 3, 2)
    bbt_re = f_re * bt_re - f_im * bt_im
    bbt_im = f_re * bt_im + f_im * bt_re

    def rows(pw_re, pw_im, m_re, m_im, sign):
        pw_re = jnp.concatenate(pw_re, axis=-1)[:, :, None, :]
        pw_im = jnp.concatenate(pw_im, axis=-1)[:, :, None, :]
        m_re = jnp.concatenate([m_re[0], m_re[1]], axis=-1)[:, None]
        m_im = jnp.concatenate([m_im[0], m_im[1]], axis=-1)[:, None]
        re = pw_re * m_re - pw_im * m_im
        im = pw_re * m_im + pw_im * m_re
        return jnp.concatenate([re, sign * im], axis=-1).astype(BF16).reshape(G, T * C, 4 * P)

    w_t = rows([pr[0, :, T - 1::-1], pr[1, :, :T]], [pi[0, :, T - 1::-1], pi[1, :, :T]], bbt_re, bbt_im, 1.0)
    vt_t = rows([pr[0, :, 1:], pr[1, :, :0:-1]], [pi[0, :, 1:], pi[1, :, :0:-1]], c_re, c_im, -1.0)

    p0r = pr[:, :, :T, None, :]
    p0i = pi[:, :, :T, None, :]
    cp_re = c_re[:, :, None] * p0r - c_im[:, :, None] * p0i
    cp_im = c_re[:, :, None] * p0i + c_im[:, :, None] * p0r
    cb = jnp.einsum('dgtcq,dgkq->dgtck', jnp.concatenate([cp_re, -cp_im], axis=-1),
                    jnp.concatenate([bbt_re, bbt_im], axis=-1), precision=lax.Precision.HIGH)

    kern = jnp.concatenate([cb[1, :, T - 1:0:-1], (cb[0, :, 0] + cb[1, :, 0])[:, None], cb[0, :, 1:],
                            jnp.zeros_like(cb[0, :, :1])], axis=1)
    r_t = kern.transpose(0, 3, 1, 2).reshape(G, C, 2 * T * C)

    lre = jnp.concatenate([pr[0, :, T], pr[1, :, T]], axis=1)
    lim = jnp.concatenate([pi[0, :, T], pi[1, :, T]], axis=1)
    return w_t, r_t, vt_t, lre, lim


def _layer_norm(r, g, b):
    mu = jnp.mean(r, axis=-1, keepdims=True)
    d = r - mu
    var = jnp.mean(d * d, axis=-1, keepdims=True)
    return d * lax.rsqrt(var + LN_EPS) * g + b


def _mix_kernel(x_ref, a_ref, y_ref, u_ref, d_ref, wglu_ref, wout_ref, g_ref, b_ref, o_ref, ob_ref,
                *, alpha):
    n_heads, tm, hd = a_ref.shape
    wa = n_heads * hd
    for half in range(2):
        rows = pl.ds(half * (tm // 2), tm // 2)
        yy = y_ref[rows, :] + d_ref[...] * u_ref[rows, :]
        z = jax.nn.gelu(yy)
        gate = jax.nn.sigmoid(jnp.dot(z.astype(BF16), wglu_ref[...], preferred_element_type=F32))
        s = (z * gate).astype(BF16)
        a = jnp.concatenate([a_ref[h, rows, :] for h in range(n_heads)], axis=1)
        mixed = jnp.dot(a, wout_ref[pl.ds(0, wa), :], preferred_element_type=F32)
        mixed = mixed + jnp.dot(s, wout_ref[pl.ds(wa, s.shape[1]), :], preferred_element_type=F32)
        r = alpha * x_ref[rows, :] + mixed
        o = _layer_norm(r, g_ref[...], b_ref[...])
        o_ref[rows, :] = o
        ob_ref[rows, :] = o.astype(BF16)


def _mix(x2d, attn, y, u, d, wglu, wout, g, b, alpha, tm=512):
    T, D = x2d.shape
    W = y.shape[1]
    row = lambda i: (i, 0)
    fixed = lambda i: (0, 0)
    once = pl.Buffered(1)
    return pl.pallas_call(
        functools.partial(_mix_kernel, alpha=alpha),
        grid=(T // tm,),
        in_specs=[
            pl.BlockSpec((tm, D), row),
            pl.BlockSpec((attn.shape[0], tm, attn.shape[2]), lambda i: (0, i, 0)),
            pl.BlockSpec((tm, W), row),
            pl.BlockSpec((tm, W), row),
            pl.BlockSpec((1, W), fixed),
            pl.BlockSpec(wglu.shape, fixed, pipeline_mode=once),
            pl.BlockSpec(wout.shape, fixed, pipeline_mode=once),
            pl.BlockSpec((1, D), fixed),
            pl.BlockSpec((1, D), fixed),
        ],
        out_specs=[pl.BlockSpec((tm, D), row), pl.BlockSpec((tm, D), row)],
        out_shape=[jax.ShapeDtypeStruct((T, D), F32), jax.ShapeDtypeStruct((T, D), BF16)],
        compiler_params=_cparams(("arbitrary",)),
        name="glu_outproj_ln",
    )(x2d, attn, y, u, d, wglu, wout, g, b)


def _ffn_up_kernel(x_ref, wg_ref, wu_ref, h_ref):
    x = x_ref[...]
    g = jnp.dot(x, wg_ref[...], preferred_element_type=F32)
    up = jnp.dot(x, wu_ref[...], preferred_element_type=F32)
    h_ref[...] = (jax.nn.silu(g) * up).astype(h_ref.dtype)


def _ffn_up(xb, wg, wu, tm=1024, tf=512):
    T, D = xb.shape
    F = wg.shape[1]
    return pl.pallas_call(
        _ffn_up_kernel,
        grid=(T // tm, F // tf),
        in_specs=[
            pl.BlockSpec((tm, D), lambda i, j: (i, 0)),
            pl.BlockSpec((D, tf), lambda i, j: (0, j)),
            pl.BlockSpec((D, tf), lambda i, j: (0, j)),
        ],
        out_specs=pl.BlockSpec((tm, tf), lambda i, j: (i, j)),
        out_shape=jax.ShapeDtypeStruct((T, F), BF16),
        compiler_params=_cparams(("arbitrary", "arbitrary")),
        name="ffn_up",
    )(xb, wg, wu)


def _ffn_down_kernel(h_ref, wd_ref, x_ref, g_ref, b_ref, o_ref, *, alpha):
    down = jnp.dot(h_ref[...], wd_ref[...], preferred_element_type=F32)
    o_ref[...] = _layer_norm(alpha * x_ref[...] + down, g_ref[...], b_ref[...])


def _ffn_down(h, wd, x1, g, b, alpha, tm=256):
    T, F = h.shape
    D = wd.shape[1]
    return pl.pallas_call(
        functools.partial(_ffn_down_kernel, alpha=alpha),
        grid=(T // tm,),
        in_specs=[
            pl.BlockSpec((tm, F), lambda i: (i, 0)),
            pl.BlockSpec((F, D), lambda i: (0, 0), pipeline_mode=pl.Buffered(1)),
            pl.BlockSpec((tm, D), lambda i: (i, 0)),
            pl.BlockSpec((1, D), lambda i: (0, 0)),
            pl.BlockSpec((1, D), lambda i: (0, 0)),
        ],
        out_specs=pl.BlockSpec((tm, D), lambda i: (i, 0)),
        out_shape=jax.ShapeDtypeStruct((T, D), F32),
        compiler_params=_cparams(("arbitrary",)),
        name="ffn_down_ln",
    )(h, wd, x1, g, b)


def _rope_tables(L):
    half = ROT_DIM // 2
    inv = ROPE_THETA ** (-jnp.arange(0, ROT_DIM, 2, dtype=F32) / ROT_DIM)
    ang = jnp.arange(L, dtype=F32)[:, None] * inv[None, :]
    cos, sin = jnp.cos(ang), jnp.sin(ang)
    lane = jnp.arange(LANES)
    d = lane % DIFF_HEAD_DIM
    freq = jnp.arange(half)[:, None]
    first = ((d < half)[None, :] & (d[None, :] == freq)).astype(F32)
    second = (((d >= half) & (d < ROT_DIM))[None, :] & (d[None, :] - half == freq)).astype(F32)
    spread = functools.partial(jnp.dot, precision=lax.Precision.HIGHEST)
    c = spread(cos, first + second) + (d >= ROT_DIM).astype(F32)[None, :]
    sa = -spread(sin, first)
    sb = spread(sin, second)
    tab = jnp.stack([c, sa, sb])
    return jnp.stack([tab * (DIFF_HEAD_DIM ** -0.5 * math.log2(math.e)), tab])


def _prepare_layer(p):
    (w_in, lq1, lk1, lq2, lk2, subln_w, a_re, a_im, log_dt, b_re, b_im, c_re, c_im, ssm_d, w_glu,
     w_out, ln1_g, ln1_b, w_gate, w_up, w_down, ln2_g, ln2_b) = p
    W = w_glu.shape[0]
    D = w_in.shape[0]
    row = lambda t, n: t.reshape(1, n).astype(F32)
    return dict(
        w_qku=jnp.concatenate([w_in[:, :2 * W], w_in[:, 3 * W:]], axis=1).astype(BF16),
        wvt=w_in[:, 2 * W:3 * W].T.astype(BF16),
        lam_vecs=jnp.stack([lq1, lk1, lq2, lk2]).astype(F32),
        sub_col=subln_w.astype(F32).reshape(-1, 1),
        ssm=_ssm_tables(a_re, a_im, log_dt, b_re, b_im, c_re, c_im),
        ssm_d=row(ssm_d, W), w_glu=w_glu.astype(BF16), w_out=w_out.astype(BF16),
        ln1_g=row(ln1_g, D), ln1_b=row(ln1_b, D),
        w_gate=w_gate.astype(BF16), w_up=w_up.astype(BF16), w_down=w_down.astype(BF16),
        ln2_g=row(ln2_g, D), ln2_b=row(ln2_b, D),
    )


def _encoder_layer(x, layer_idx, depth, pp):
    B, L, D = x.shape
    T = B * L
    W = pp["w_glu"].shape[0]
    alpha = (2 * depth) ** 0.25
    lam_init = 0.8 - 0.6 * math.exp(-0.3 * layer_idx)

    x2d = x.reshape(T, D)
    q, k, vt, u, ub = _inproj(x2d.astype(BF16), pp["w_qku"], pp["wvt"], _rope_tables(L), L)
    attn = _attention(q, k, vt, pp["lam_vecs"], pp["sub_col"], lam_init, L)
    y = _ssm(ub, *pp["ssm"], L)
    x1, x1b = _mix(x2d, attn, y, u, pp["ssm_d"], pp["w_glu"], pp["w_out"], pp["ln1_g"], pp["ln1_b"], alpha)
    h = _ffn_up(x1b, pp["w_gate"], pp["w_up"])
    out = _ffn_down(h, pp["w_down"], x1, pp["ln2_g"], pp["ln2_b"], alpha)
    return out.reshape(B, L, D)


def kernel(x_prompt, x_sample, w_in, lambda_q1, lambda_k1, lambda_q2, lambda_k2, subln_w, ssm_a_re, ssm_a_im, ssm_log_dt, ssm_b_re, ssm_b_im, ssm_c_re, ssm_c_im, ssm_d, w_glu, w_out, ln1_g, ln1_b, w_gate, w_up, w_down, ln2_g, ln2_b):
    params = (w_in, lambda_q1, lambda_k1, lambda_q2, lambda_k2, subln_w,
              ssm_a_re, ssm_a_im, ssm_log_dt, ssm_b_re, ssm_b_im, ssm_c_re, ssm_c_im, ssm_d, w_glu,
              w_out, ln1_g, ln1_b, w_gate, w_up, w_down, ln2_g, ln2_b)
    depth = w_in.shape[0]
    y_prompt, y_sample = x_prompt, x_sample
    for layer in range(depth):
        lp = _prepare_layer(tuple(t[layer] for t in params))
        y_prompt = _encoder_layer(y_prompt, layer, depth, lp)
        y_sample = _encoder_layer(y_sample, layer, depth, lp)
    return (y_prompt, y_sample)
```

```python
import functools
import math

import jax
import jax.numpy as jnp
from jax import lax
from jax.experimental import pallas as pl
from jax.experimental.pallas import tpu as pltpu

F32 = jnp.float32
BF16 = jnp.bfloat16

DIFF_HEAD_DIM = 64
ROT_DIM = DIFF_HEAD_DIM // 4
ROPE_THETA = 500000.0
SSM_GROUP = 16
SSM_STATE = 64
LN_EPS = 1e-5

LANES = 128
SUBLANES = 8
VMEM_LIMIT_BYTES = 56 * 1024 * 1024

SSM_CHUNK = 32
SSM_GBLK = SUBLANES
ATT_BK = 512
ATT_BQ = 512
ATT_UNROLL = 8
ATT_UNROLL_SHORT = 8
ATT_ONES_ROWS = 16
NEG_BIG = -1e30


def _cparams(sem):
    return pltpu.CompilerParams(dimension_semantics=sem, vmem_limit_bytes=VMEM_LIMIT_BYTES)


def _inproj_kernel(xb_ref, w_ref, wvt_ref, rope_ref, perm_ref, q_ref, k_ref, vt_ref, u_ref, ub_ref):
    j = pl.program_id(1)

    def rope_to(out_ref):
        acc = jnp.dot(xb_ref[...], w_ref[...], preferred_element_type=F32)
        c = rope_ref[0, 0]
        sa = rope_ref[0, 1]
        sb = rope_ref[0, 2]
        for cb in range(acc.shape[1] // LANES):
            xs = acc[:, cb * LANES:(cb + 1) * LANES]
            nxt = pltpu.roll(xs, LANES - ROT_DIM // 2, axis=1)
            prv = pltpu.roll(xs, ROT_DIM // 2, axis=1)
            out_ref[cb] = (xs * c + nxt * sa + prv * sb).astype(out_ref.dtype)

    @pl.when(j == 0)
    def _():
        rope_to(q_ref)

    @pl.when(j == 1)
    def _():
        rope_to(k_ref)

    @pl.when(j == 2)
    def _():
        vt = lax.dot_general(wvt_ref[...], xb_ref[...], (((1,), (1,)), ((), ())),
                             preferred_element_type=F32)
        hd = 2 * DIFF_HEAD_DIM
        ones = jnp.ones((vt_ref.shape[2] - hd, ATT_BK), vt_ref.dtype)
        for cc in range(vt_ref.shape[0]):
            for h in range(vt_ref.shape[1]):
                vt_ref[cc, h, pl.ds(0, hd), :] = (
                    vt[h * hd:(h + 1) * hd, cc * ATT_BK:(cc + 1) * ATT_BK].astype(vt_ref.dtype))
                vt_ref[cc, h, pl.ds(hd, ones.shape[0]), :] = ones

    @pl.when(j == 3)
    def _():
        u = jnp.dot(xb_ref[...], w_ref[...], preferred_element_type=F32)
        u_ref[...] = u
        ut = jnp.dot(perm_ref[...], u.astype(BF16), preferred_element_type=F32).astype(ub_ref.dtype)
        T = SSM_CHUNK
        nch = ub_ref.shape[0]
        for slab in range(u.shape[1] // LANES):
            for t in range(T):
                ub_ref[:, pl.ds((slab * T + t) * LANES, LANES)] = (
                    ut[t * nch:(t + 1) * nch, slab * LANES:(slab + 1) * LANES])


def _inproj(xb2d, w_qku, wvt, rope, seq_len, tm=1024):
    T, D = xb2d.shape
    W = wvt.shape[0]
    hd = 2 * DIFF_HEAD_DIM
    H = W // hd
    hda = hd + ATT_ONES_ROWS
    nseq_blk = seq_len // tm
    grid = (T // tm, 4)
    nch = tm // SSM_CHUNK
    r = jnp.arange(tm)
    perm = ((r % nch) * SSM_CHUNK + r // nch)[:, None] == r[None, :]
    perm = perm.astype(BF16)
    wmap = lambda i, j: (0, jnp.where(j >= 2, j - 1, j))
    return pl.pallas_call(
        _inproj_kernel,
        grid=grid,
        in_specs=[
            pl.BlockSpec((tm, D), lambda i, j: (i, 0)),
            pl.BlockSpec((D, W), wmap),
            pl.BlockSpec((W, D), lambda i, j: (0, 0), pipeline_mode=pl.Buffered(1)),
            pl.BlockSpec((1, 3, tm, LANES), lambda i, j: (jnp.minimum(j, 1), 0, i % nseq_blk, 0)),
            pl.BlockSpec((tm, tm), lambda i, j: (0, 0), pipeline_mode=pl.Buffered(1)),
        ],
        out_specs=[
            pl.BlockSpec((H, tm, hd), lambda i, j: (0, i, 0)),
            pl.BlockSpec((H, tm, hd), lambda i, j: (0, i, 0)),
            pl.BlockSpec((tm // ATT_BK, H, hda, ATT_BK), lambda i, j: (i, 0, 0, 0)),
            pl.BlockSpec((tm, W), lambda i, j: (i, 0)),
            pl.BlockSpec((nch, SSM_CHUNK * W), lambda i, j: (i, 0)),
        ],
        out_shape=[
            jax.ShapeDtypeStruct((H, T, hd), BF16),
            jax.ShapeDtypeStruct((H, T, hd), BF16),
            jax.ShapeDtypeStruct((T // ATT_BK, H, hda, ATT_BK), BF16),
            jax.ShapeDtypeStruct((T, W), F32),
            jax.ShapeDtypeStruct((T // SSM_CHUNK, SSM_CHUNK * W), BF16),
        ],
        compiler_params=_cparams(("arbitrary", "arbitrary")),
        name="inproj",
    )(xb2d, w_qku, wvt, rope, perm)


def _attn_kernel(lam_ref, sub_ref, q_ref, k_ref, vt_ref, o_ref, q1t_ref, q2t_ref, o1_ref, o2_ref, s_ref,
                 *, lam_init, bq, unroll):
    nk = vt_ref.shape[0]
    L, hd = q_ref.shape
    nq = L // bq

    lv = lam_ref[...]
    d1 = jnp.sum(lv[0:1] * lv[1:2], axis=1, keepdims=True)
    d2 = jnp.sum(lv[2:3] * lv[3:4], axis=1, keepdims=True)
    lam = jnp.exp(d1) - jnp.exp(d2) + lam_init

    def transpose_q(qi, carry):
        qt = q_ref[pl.ds(pl.multiple_of(qi * bq, bq), bq), :].astype(F32).T
        row = lax.broadcasted_iota(jnp.int32, qt.shape, 0)
        q1t_ref[qi] = jnp.where(row < DIFF_HEAD_DIM, qt, 0.0).astype(BF16)
        q2t_ref[qi] = jnp.where(row >= DIFF_HEAD_DIM, qt, 0.0).astype(BF16)
        return carry

    lax.fori_loop(0, nq, transpose_q, 0)
    o1_ref[...] = jnp.zeros_like(o1_ref)
    o2_ref[...] = jnp.zeros_like(o2_ref)

    def scores(qi, c, slot):
        kb = k_ref[pl.ds(pl.multiple_of(c * ATT_BK, ATT_BK), ATT_BK), :]
        tops = []
        for br, qt_ref in enumerate((q1t_ref, q2t_ref)):
            s = jnp.dot(kb, qt_ref[qi], preferred_element_type=F32)
            s_ref[slot, br] = s
            tops.append(jnp.max(s, axis=0, keepdims=True))
        return tuple(tops)

    def update(c, slot, tops, ms):
        vtb = vt_ref[c]
        new_ms = []
        for br, o_acc in enumerate((o1_ref, o2_ref)):
            m_new = jnp.maximum(ms[br], tops[br])
            alpha = jnp.exp2(ms[br] - m_new)
            p = jnp.exp2(s_ref[slot, br] - m_new).astype(BF16)
            pv = jnp.dot(vtb, p, preferred_element_type=F32)
            o_acc[...] = alpha * o_acc[...] + pv
            new_ms.append(m_new)
        return tuple(new_ms)

    def finish(qi):
        l1 = o1_ref[pl.ds(hd, 1), :]
        l2 = o2_ref[pl.ds(hd, 1), :]
        o = o1_ref[pl.ds(0, hd), :] / l1 - lam * (o2_ref[pl.ds(0, hd), :] / l2)
        msq = jnp.mean(o * o, axis=0, keepdims=True)
        o = o * lax.rsqrt(msq + LN_EPS) * sub_ref[...]
        o = o * (1.0 - lam_init)
        o_ref[pl.ds(pl.multiple_of(qi * bq, bq), bq), :] = o.T.astype(o_ref.dtype)
        o1_ref[...] = jnp.zeros_like(o1_ref)
        o2_ref[...] = jnp.zeros_like(o2_ref)

    neg = jnp.full((1, bq), NEG_BIG, F32)

    def body_long(t, carry):
        ms, tops = carry
        qi = t // trips_per_q
        c0 = unroll * (t % trips_per_q)
        last = c0 + unroll == nk
        for j in range(unroll):
            if j + 1 < unroll:
                nxt_tops = scores(qi, c0 + j + 1, (j + 1) % 2)
            else:
                nxt_q = jnp.where(last, jnp.minimum(qi + 1, nq - 1), qi)
                nxt_c = jnp.where(last, 0, c0 + unroll)
                nxt_tops = scores(nxt_q, nxt_c, 0)
            ms = update(c0 + j, j % 2, tops, ms)
            tops = nxt_tops

        @pl.when(last)
        def _():
            finish(qi)

        ms = tuple(jnp.where(last, neg, m) for m in ms)
        return ms, tops

    def body_short(t, carry):
        ms, tops = carry
        q0 = t * (unroll // nk)
        for j in range(unroll):
            qi, c = q0 + j // nk, j % nk
            if j + 1 < unroll:
                nxt_tops = scores(q0 + (j + 1) // nk, (j + 1) % nk, (j + 1) % 2)
            else:
                nxt_tops = scores(jnp.minimum(q0 + unroll // nk, nq - 1), 0, 0)
            ms = update(c, j % 2, tops, ms)
            tops = nxt_tops
            if c == nk - 1:
                finish(qi)
                ms = (neg, neg)
        return ms, tops

    tops0 = scores(0, 0, 0)
    if unroll <= nk:
        trips_per_q = nk // unroll
        lax.fori_loop(0, nq * trips_per_q, body_long, ((neg, neg), tops0))
    else:
        lax.fori_loop(0, nq * nk // unroll, body_short, ((neg, neg), tops0))


def _attention(q, k, vt, lam_vecs, sub_col, lam_init, seq_len):
    H, T, hd = q.shape
    L = seq_len
    B = T // L
    hda = vt.shape[2]
    nk = L // ATT_BK
    bq = min(ATT_BQ, L)
    nq = L // bq
    if nk >= ATT_UNROLL:
        unroll = ATT_UNROLL
        assert nk % unroll == 0
    else:
        unroll = nk * min(nq, ATT_UNROLL_SHORT // nk)
        assert unroll >= nk and (nq * nk) % unroll == 0
    assert unroll % 2 == 0
    seq = lambda b, h: (h, b, 0)
    return pl.pallas_call(
        functools.partial(_attn_kernel, lam_init=lam_init, bq=bq, unroll=unroll),
        grid=(B, H),
        in_specs=[
            pl.BlockSpec(lam_vecs.shape, lambda b, h: (0, 0)),
            pl.BlockSpec(sub_col.shape, lambda b, h: (0, 0)),
            pl.BlockSpec((None, L, hd), seq),
            pl.BlockSpec((None, L, hd), seq),
            pl.BlockSpec((nk, None, hda, ATT_BK), lambda b, h: (b, h, 0, 0)),
        ],
        out_specs=pl.BlockSpec((None, L, hd), seq),
        out_shape=jax.ShapeDtypeStruct((H, T, hd), BF16),
        scratch_shapes=[
            pltpu.VMEM((nq, hd, bq), BF16),
            pltpu.VMEM((nq, hd, bq), BF16),
            pltpu.VMEM((hda, bq), F32),
            pltpu.VMEM((hda, bq), F32),
            pltpu.VMEM((2, 2, ATT_BK, bq), F32),
        ],
        compiler_params=_cparams(("arbitrary", "arbitrary")),
        name="diff_attn",
    )(lam_vecs, sub_col, q, k, vt)


def _ssm_kernel(u_ref, p_ref, pt_ref, w_ref, r_ref, v_ref, lre_ref, lim_ref, y_ref,
                ug_ref, yg_ref, m_ref, sre, sim, hfre, hfim, hbre, hbim, *, cps):
    o = pl.program_id(1)

    @pl.when(o == 0)
    def _():
        _ssm_compute(u_ref, p_ref, w_ref, r_ref, v_ref, lre_ref, lim_ref,
                     ug_ref, yg_ref, m_ref, sre, sim, hfre, hfim, hbre, hbim, cps)

    res = jnp.dot(yg_ref[o], pt_ref[...], preferred_element_type=F32)
    for t8 in range(SUBLANES):
        y_ref[:, t8, :] = res[:, t8 * LANES:(t8 + 1) * LANES]


def _ssm_compute(u_ref, p_ref, w_ref, r_ref, v_ref, lre_ref, lim_ref,
                 ug_ref, yg_ref, m_ref, sre, sim, hfre, hfim, hbre, hbim, cps):
    gb = ug_ref.shape[0]
    nc = ug_ref.shape[1]
    half = LANES // 2
    T = SSM_CHUNK
    n_oct = T // SUBLANES
    oct_lanes = SUBLANES * LANES

    for oc in range(n_oct):
        z = u_ref[:, pl.ds(oc * oct_lanes, oct_lanes)]
        zg = jnp.dot(z, p_ref[...], preferred_element_type=F32).astype(BF16)
        for gi in range(gb):
            ug_ref[gi, :, pl.ds(oc * LANES, LANES)] = zg[:, gi * LANES:(gi + 1) * LANES]

    C = SSM_GROUP
    for gi in range(gb):
        r = r_ref[gi]
        for s_ in range(T):
            off = (T - 1 - s_) * C
            m_ref[gi, pl.ds(s_ * C, C), :] = r[:, off:off + T * C].astype(BF16)

    for gi in range(gb):
        s = jnp.dot(ug_ref[gi], w_ref[gi], preferred_element_type=F32)
        sre[pl.ds(gi, nc, stride=gb), :] = s[:, :LANES]
        sim[pl.ds(gi, nc, stride=gb), :] = s[:, LANES:]

    zero = jnp.zeros((gb, LANES), F32)
    lre = lre_ref[...]
    lim = lim_ref[...]
    fwd_lane = lax.broadcasted_iota(jnp.int32, (gb, LANES), 1) < half

    def body(j, carry):
        hr, hi = carry
        fresh = (j % cps) == 0
        hr = jnp.where(fresh, 0.0, hr)
        hi = jnp.where(fresh, 0.0, hi)
        jf = pl.multiple_of(j * gb, gb)
        jb = pl.multiple_of((nc - 1 - j) * gb, gb)
        hfre[pl.ds(jf, gb), :] = hr
        hfim[pl.ds(jf, gb), :] = hi
        hbre[pl.ds(jb, gb), :] = hr
        hbim[pl.ds(jb, gb), :] = hi
        sr = jnp.where(fwd_lane, sre[pl.ds(jf, gb), :], sre[pl.ds(jb, gb), :])
        si = jnp.where(fwd_lane, sim[pl.ds(jf, gb), :], sim[pl.ds(jb, gb), :])
        return lre * hr - lim * hi + sr, lre * hi + lim * hr + si

    lax.fori_loop(0, nc, body, (zero, zero))

    fwd_col = lax.broadcasted_iota(jnp.int32, (nc, LANES), 1) < half
    for gi in range(gb):
        hr = jnp.where(fwd_col, hfre[pl.ds(gi, nc, stride=gb), :], hbre[pl.ds(gi, nc, stride=gb), :])
        hi = jnp.where(fwd_col, hfim[pl.ds(gi, nc, stride=gb), :], hbim[pl.ds(gi, nc, stride=gb), :])
        h = jnp.concatenate([hr, hi], axis=1).astype(BF16)
        y = jnp.dot(ug_ref[gi], m_ref[gi], preferred_element_type=F32)
        y = y + lax.dot_general(h, v_ref[gi], (((1,), (1,)), ((), ())),
                                preferred_element_type=F32)
        for oc in range(n_oct):
            yg_ref[oc, :, pl.ds(gi * LANES, LANES)] = y[:, oc * LANES:(oc + 1) * LANES].astype(yg_ref.dtype)


def _lane_regroup_matrix():
    n = SUBLANES * LANES
    i = jnp.arange(n)
    t8, gl, c = i // LANES, (i % LANES) // SSM_GROUP, i % SSM_GROUP
    dest = gl * LANES + t8 * SSM_GROUP + c
    return (dest[:, None] == i[None, :]).astype(BF16)


def _ssm(ub, w_t, r_t, v_t, lre, lim, seq_len):
    T = SSM_CHUNK
    nc = ub.shape[0]
    Wd = ub.shape[1] // T
    Ttok = nc * T
    cps = seq_len // T
    gb = SSM_GBLK
    assert gb * SSM_GROUP == LANES and T % SUBLANES == 0 and nc % cps == 0
    n_slab = Wd // LANES
    n_oct = T // SUBLANES
    K = T * SSM_GROUP
    ncols = w_t.shape[2]
    pmat = _lane_regroup_matrix()
    once = pl.Buffered(1)
    fixed2 = lambda g, o: (0, 0)
    per_slab = lambda g, o: (g, 0, 0)

    y = pl.pallas_call(
        functools.partial(_ssm_kernel, cps=cps),
        grid=(n_slab, n_oct),
        in_specs=[
            pl.BlockSpec((nc, T * LANES), lambda g, o: (0, g)),
            pl.BlockSpec(pmat.shape, fixed2, pipeline_mode=once),
            pl.BlockSpec(pmat.shape, fixed2, pipeline_mode=once),
            pl.BlockSpec((gb, K, ncols), per_slab, pipeline_mode=once),
            pl.BlockSpec((gb,) + r_t.shape[1:], per_slab, pipeline_mode=once),
            pl.BlockSpec((gb, K, ncols), per_slab, pipeline_mode=once),
            pl.BlockSpec((gb, LANES), lambda g, o: (g, 0)),
            pl.BlockSpec((gb, LANES), lambda g, o: (g, 0)),
        ],
        out_specs=pl.BlockSpec((nc, SUBLANES, LANES), lambda g, o: (0, o, g)),
        out_shape=jax.ShapeDtypeStruct((nc, T, Wd), F32),
        scratch_shapes=[
            pltpu.VMEM((gb, nc, K), BF16),
            pltpu.VMEM((n_oct, nc, SUBLANES * LANES), BF16),
            pltpu.VMEM((gb, K, K), BF16),
        ] + [pltpu.VMEM((nc * gb, LANES), F32)] * 6,
        compiler_params=_cparams(("arbitrary", "arbitrary")),
        name="s5_chunked",
    )(ub, pmat, pmat.T, w_t, r_t, v_t, lre, lim)
    return y.reshape(Ttok, Wd)


def _ssm_tables(a_re, a_im, log_dt, b_re, b_im, c_re, c_im):
    T = SSM_CHUNK
    G, P = a_re.shape[1], a_re.shape[2]
    C = b_re.shape[3]
    dt = jnp.exp(log_dt)[..., None]
    zr = a_re * dt
    zi = a_im * dt
    tau = jnp.arange(T + 1, dtype=F32)[:, None]
    mag = jnp.exp(zr[:, :, None] * tau)
    pr = mag * jnp.cos(zi[:, :, None] * tau)
    pi = mag * jnp.sin(zi[:, :, None] * tau)
    ab_re, ab_im = pr[:, :, 1], pi[:, :, 1]
    den = a_re * a_re + a_im * a_im
    nr = ab_re - 1.0
    f_re = ((nr * a_re + ab_im * a_im) / den)[:, :, None, :]
    f_im = ((ab_im * a_re - nr * a_im) / den)[:, :, None, :]
    bt_re = b_re.transpose(0, 1, 3, 2)
    bt_im = b_im.transpose(0, 1, 3, 2)
    bbt_re = f_re * bt_re - f_im * bt_im
    bbt_im = f_re * bt_im + f_im * bt_re

    def rows(pw_re, pw_im, m_re, m_im, sign):
        pw_re = jnp.concatenate(pw_re, axis=-1)[:, :, None, :]
        pw_im = jnp.concatenate(pw_im, axis=-1)[:, :, None, :]
        m_re = jnp.concatenate([m_re[0], m_re[1]], axis=-1)[:, None]
        m_im = jnp.concatenate([m_im[0], m_im[1]], axis=-1)[:, None]
        re = pw_re * m_re - pw_im * m_im
        im = pw_re * m_im + pw_im * m_re
        return jnp.concatenate([re, sign * im], axis=-1).astype(BF16).reshape(G, T * C, 4 * P)

    w_t = rows([pr[0, :, T - 1::-1], pr[1, :, :T]], [pi[0, :, T - 1::-1], pi[1, :, :T]], bbt_re, bbt_im, 1.0)
    vt_t = rows([pr[0, :, 1:], pr[1, :, :0:-1]], [pi[0, :, 1:], pi[1, :, :0:-1]], c_re, c_im, -1.0)

    p0r = pr[:, :, :T, None, :]
    p0i = pi[:, :, :T, None, :]
    cp_re = c_re[:, :, None] * p0r - c_im[:, :, None] * p0i
    cp_im = c_re[:, :, None] * p0i + c_im[:, :, None] * p0r
    cb = jnp.einsum('dgtcq,dgkq->dgtck', jnp.concatenate([cp_re, -cp_im], axis=-1),
                    jnp.concatenate([bbt_re, bbt_im], axis=-1), precision=lax.Precision.HIGH)

    kern = jnp.concatenate([cb[1, :, T - 1:0:-1], (cb[0, :, 0] + cb[1, :, 0])[:, None], cb[0, :, 1:],
                            jnp.zeros_like(cb[0, :, :1])], axis=1)
    r_t = kern.transpose(0, 3, 1, 2).reshape(G, C, 2 * T * C)

    lre = jnp.concatenate([pr[0, :, T], pr[1, :, T]], axis=1)
    lim = jnp.concatenate([pi[0, :, T], pi[1, :, T]], axis=1)
    return w_t, r_t, vt_t, lre, lim


def _layer_norm(r, g, b):
    mu = jnp.mean(r, axis=-1, keepdims=True)
    d = r - mu
    var = jnp.mean(d * d, axis=-1, keepdims=True)
    return d * lax.rsqrt(var + LN_EPS) * g + b


def _mix_kernel(x_ref, a_ref, y_ref, u_ref, d_ref, wglu_ref, wout_ref, g_ref, b_ref, o_ref, ob_ref,
                *, alpha):
    n_heads, tm, hd = a_ref.shape
    wa = n_heads * hd
    for half in range(2):
        rows = pl.ds(half * (tm // 2), tm // 2)
        yy = y_ref[rows, :] + d_ref[...] * u_ref[rows, :]
        z = jax.nn.gelu(yy)
        gate = jax.nn.sigmoid(jnp.dot(z.astype(BF16), wglu_ref[...], preferred_element_type=F32))
        s = (z * gate).astype(BF16)
        a = jnp.concatenate([a_ref[h, rows, :] for h in range(n_heads)], axis=1)
        mixed = jnp.dot(a, wout_ref[pl.ds(0, wa), :], preferred_element_type=F32)
        mixed = mixed + jnp.dot(s, wout_ref[pl.ds(wa, s.shape[1]), :], preferred_element_type=F32)
        r = alpha * x_ref[rows, :] + mixed
        o = _layer_norm(r, g_ref[...], b_ref[...])
        o_ref[rows, :] = o
        ob_ref[rows, :] = o.astype(BF16)


def _mix(x2d, attn, y, u, d, wglu, wout, g, b, alpha, tm=512):
    T, D = x2d.shape
    W = y.shape[1]
    row = lambda i: (i, 0)
    fixed = lambda i: (0, 0)
    once = pl.Buffered(1)
    return pl.pallas_call(
        functools.partial(_mix_kernel, alpha=alpha),
        grid=(T // tm,),
        in_specs=[
            pl.BlockSpec((tm, D), row),
            pl.BlockSpec((attn.shape[0], tm, attn.shape[2]), lambda i: (0, i, 0)),
            pl.BlockSpec((tm, W), row),
            pl.BlockSpec((tm, W), row),
            pl.BlockSpec((1, W), fixed),
            pl.BlockSpec(wglu.shape, fixed, pipeline_mode=once),
            pl.BlockSpec(wout.shape, fixed, pipeline_mode=once),
            pl.BlockSpec((1, D), fixed),
            pl.BlockSpec((1, D), fixed),
        ],
        out_specs=[pl.BlockSpec((tm, D), row), pl.BlockSpec((tm, D), row)],
        out_shape=[jax.ShapeDtypeStruct((T, D), F32), jax.ShapeDtypeStruct((T, D), BF16)],
        compiler_params=_cparams(("arbitrary",)),
        name="glu_outproj_ln",
    )(x2d, attn, y, u, d, wglu, wout, g, b)


def _ffn_up_kernel(x_ref, wg_ref, wu_ref, h_ref):
    x = x_ref[...]
    g = jnp.dot(x, wg_ref[...], preferred_element_type=F32)
    up = jnp.dot(x, wu_ref[...], preferred_element_type=F32)
    h_ref[...] = (jax.nn.silu(g) * up).astype(h_ref.dtype)


def _ffn_up(xb, wg, wu, tm=1024, tf=512):
    T, D = xb.shape
    F = wg.shape[1]
    return pl.pallas_call(
        _ffn_up_kernel,
        grid=(T // tm, F // tf),
        in_specs=[
            pl.BlockSpec((tm, D), lambda i, j: (i, 0)),
            pl.BlockSpec((D, tf), lambda i, j: (0, j)),
            pl.BlockSpec((D, tf), lambda i, j: (0, j)),
        ],
        out_specs=pl.BlockSpec((tm, tf), lambda i, j: (i, j)),
        out_shape=jax.ShapeDtypeStruct((T, F), BF16),
        compiler_params=_cparams(("arbitrary", "arbitrary")),
        name="ffn_up",
    )(xb, wg, wu)


def _ffn_down_kernel(h_ref, wd_ref, x_ref, g_ref, b_ref, o_ref, *, alpha):
    down = jnp.dot(h_ref[...], wd_ref[...], preferred_element_type=F32)
    o_ref[...] = _layer_norm(alpha * x_ref[...] + down, g_ref[...], b_ref[...])


def _ffn_down(h, wd, x1, g, b, alpha, tm=256):
    T, F = h.shape
    D = wd.shape[1]
    return pl.pallas_call(
        functools.partial(_ffn_down_kernel, alpha=alpha),
        grid=(T // tm,),
        in_specs=[
            pl.BlockSpec((tm, F), lambda i: (i, 0)),
            pl.BlockSpec((F, D), lambda i: (0, 0), pipeline_mode=pl.Buffered(1)),
            pl.BlockSpec((tm, D), lambda i: (i, 0)),
            pl.BlockSpec((1, D), lambda i: (0, 0)),
            pl.BlockSpec((1, D), lambda i: (0, 0)),
        ],
        out_specs=pl.BlockSpec((tm, D), lambda i: (i, 0)),
        out_shape=jax.ShapeDtypeStruct((T, D), F32),
        compiler_params=_cparams(("arbitrary",)),
        name="ffn_down_ln",
    )(h, wd, x1, g, b)


def _rope_tables(L):
    half = ROT_DIM // 2
    inv = ROPE_THETA ** (-jnp.arange(0, ROT_DIM, 2, dtype=F32) / ROT_DIM)
    ang = jnp.arange(L, dtype=F32)[:, None] * inv[None, :]
    cos, sin = jnp.cos(ang), jnp.sin(ang)
    lane = jnp.arange(LANES)
    d = lane % DIFF_HEAD_DIM
    freq = jnp.arange(half)[:, None]
    first = ((d < half)[None, :] & (d[None, :] == freq)).astype(F32)
    second = (((d >= half) & (d < ROT_DIM))[None, :] & (d[None, :] - half == freq)).astype(F32)
    spread = functools.partial(jnp.dot, precision=lax.Precision.HIGHEST)
    c = spread(cos, first + second) + (d >= ROT_DIM).astype(F32)[None, :]
    sa = -spread(sin, first)
    sb = spread(sin, second)
    tab = jnp.stack([c, sa, sb])
    return jnp.stack([tab * (DIFF_HEAD_DIM ** -0.5 * math.log2(math.e)), tab])


def _prepare_layer(p):
    (w_in, lq1, lk1, lq2, lk2, subln_w, a_re, a_im, log_dt, b_re, b_im, c_re, c_im, ssm_d, w_glu,
     w_out, ln1_g, ln1_b, w_gate, w_up, w_down, ln2_g, ln2_b) = p
    W = w_glu.shape[0]
    D = w_in.shape[0]
    row = lambda t, n: t.reshape(1, n).astype(F32)
    return dict(
        w_qku=jnp.concatenate([w_in[:, :2 * W], w_in[:, 3 * W:]], axis=1).astype(BF16),
        wvt=w_in[:, 2 * W:3 * W].T.astype(BF16),
        lam_vecs=jnp.stack([lq1, lk1, lq2, lk2]).astype(F32),
        sub_col=subln_w.astype(F32).reshape(-1, 1),
        ssm=_ssm_tables(a_re, a_im, log_dt, b_re, b_im, c_re, c_im),
        ssm_d=row(ssm_d, W), w_glu=w_glu.astype(BF16), w_out=w_out.astype(BF16),
        ln1_g=row(ln1_g, D), ln1_b=row(ln1_b, D),
        w_gate=w_gate.astype(BF16), w_up=w_up.astype(BF16), w_down=w_down.astype(BF16),
        ln2_g=row(ln2_g, D), ln2_b=row(ln2_b, D),
    )


def _encoder_layer(x, layer_idx, depth, pp):
    B, L, D = x.shape
    T = B * L
    W = pp["w_glu"].shape[0]
    alpha = (2 * depth) ** 0.25
    lam_init = 0.8 - 0.6 * math.exp(-0.3 * layer_idx)

    x2d = x.reshape(T, D)
    q, k, vt, u, ub = _inproj(x2d.astype(BF16), pp["w_qku"], pp["wvt"], _rope_tables(L), L)
    attn = _attention(q, k, vt, pp["lam_vecs"], pp["sub_col"], lam_init, L)
    y = _ssm(ub, *pp["ssm"], L)
    x1, x1b = _mix(x2d, attn, y, u, pp["ssm_d"], pp["w_glu"], pp["w_out"], pp["ln1_g"], pp["ln1_b"], alpha)
    h = _ffn_up(x1b, pp["w_gate"], pp["w_up"])
    out = _ffn_down(h, pp["w_down"], x1, pp["ln2_g"], pp["ln2_b"], alpha)
    return out.reshape(B, L, D)


def kernel(x_prompt, x_sample, w_in, lambda_q1, lambda_k1, lambda_q2, lambda_k2, subln_w, ssm_a_re, ssm_a_im, ssm_log_dt, ssm_b_re, ssm_b_im, ssm_c_re, ssm_c_im, ssm_d, w_glu, w_out, ln1_g, ln1_b, w_gate, w_up, w_down, ln2_g, ln2_b):
    params = (w_in, lambda_q1, lambda_k1, lambda_q2, lambda_k2, subln_w,
              ssm_a_re, ssm_a_im, ssm_log_dt, ssm_b_re, ssm_b_im, ssm_c_re, ssm_c_im, ssm_d, w_glu,
              w_out, ln1_g, ln1_b, w_gate, w_up, w_down, ln2_g, ln2_b)
    depth = w_in.shape[0]
    y_prompt, y_sample = x_prompt, x_sample
    for layer in range(depth):
        lp = _prepare_layer(tuple(t[layer] for t in params))
        y_prompt = _encoder_layer(y_prompt, layer, depth, lp)
        y_sample = _encoder_layer(y_sample, layer, depth, lp)
    return (y_prompt, y_sample)
```

```python
import functools
import math

import jax
import jax.numpy as jnp
from jax import lax
from jax.experimental import pallas as pl
from jax.experimental.pallas import tpu as pltpu

F32 = jnp.float32
BF16 = jnp.bfloat16

DIFF_HEAD_DIM = 64
ROT_DIM = DIFF_HEAD_DIM // 4
ROPE_THETA = 500000.0
SSM_GROUP = 16
SSM_STATE = 64
LN_EPS = 1e-5

LANES = 128
SUBLANES = 8
VMEM_LIMIT_BYTES = 56 * 1024 * 1024

SSM_CHUNK = 32
SSM_GBLK = SUBLANES
ATT_BK = 512
ATT_BQ = 512
ATT_UNROLL = 8
ATT_UNROLL_SHORT = 8
ATT_ONES_ROWS = 16
NEG_BIG = -1e30


def _cparams(sem):
    return pltpu.CompilerParams(dimension_semantics=sem, vmem_limit_bytes=VMEM_LIMIT_BYTES)


def _inproj_kernel(xb_ref, w_ref, wvt_ref, rope_ref, perm_ref, q_ref, k_ref, vt_ref, u_ref, ub_ref):
    j = pl.program_id(1)

    def rope_to(out_ref):
        acc = jnp.dot(xb_ref[...], w_ref[...], preferred_element_type=F32)
        c = rope_ref[0, 0]
        sa = rope_ref[0, 1]
        sb = rope_ref[0, 2]
        for cb in range(acc.shape[1] // LANES):
            xs = acc[:, cb * LANES:(cb + 1) * LANES]
            nxt = pltpu.roll(xs, LANES - ROT_DIM // 2, axis=1)
            prv = pltpu.roll(xs, ROT_DIM // 2, axis=1)
            out_ref[cb] = (xs * c + nxt * sa + prv * sb).astype(out_ref.dtype)

    @pl.when(j == 0)
    def _():
        rope_to(q_ref)

    @pl.when(j == 1)
    def _():
        rope_to(k_ref)

    @pl.when(j == 2)
    def _():
        vt = lax.dot_general(wvt_ref[...], xb_ref[...], (((1,), (1,)), ((), ())),
                             preferred_element_type=F32)
        hd = 2 * DIFF_HEAD_DIM
        ones = jnp.ones((vt_ref.shape[2] - hd, ATT_BK), vt_ref.dtype)
        for cc in range(vt_ref.shape[0]):
            for h in range(vt_ref.shape[1]):
                vt_ref[cc, h, pl.ds(0, hd), :] = (
                    vt[h * hd:(h + 1) * hd, cc * ATT_BK:(cc + 1) * ATT_BK].astype(vt_ref.dtype))
                vt_ref[cc, h, pl.ds(hd, ones.shape[0]), :] = ones

    @pl.when(j == 3)
    def _():
        u = jnp.dot(xb_ref[...], w_ref[...], preferred_element_type=F32)
        u_ref[...] = u
        ut = jnp.dot(perm_ref[...], u.astype(BF16), preferred_element_type=F32).astype(ub_ref.dtype)
        T = SSM_CHUNK
        nch = ub_ref.shape[0]
        for slab in range(u.shape[1] // LANES):
            for t in range(T):
                ub_ref[:, pl.ds((slab * T + t) * LANES, LANES)] = (
                    ut[t * nch:(t + 1) * nch, slab * LANES:(slab + 1) * LANES])


def _inproj(xb2d, w_qku, wvt, rope, seq_len, tm=1024):
    T, D = xb2d.shape
    W = wvt.shape[0]
    hd = 2 * DIFF_HEAD_DIM
    H = W // hd
    hda = hd + ATT_ONES_ROWS
    nseq_blk = seq_len // tm
    grid = (T // tm, 4)
    nch = tm // SSM_CHUNK
    r = jnp.arange(tm)
    perm = ((r % nch) * SSM_CHUNK + r // nch)[:, None] == r[None, :]
    perm = perm.astype(BF16)
    wmap = lambda i, j: (0, jnp.where(j >= 2, j - 1, j))
    return pl.pallas_call(
        _inproj_kernel,
        grid=grid,
        in_specs=[
            pl.BlockSpec((tm, D), lambda i, j: (i, 0)),
            pl.BlockSpec((D, W), wmap),
            pl.BlockSpec((W, D), lambda i, j: (0, 0), pipeline_mode=pl.Buffered(1)),
            pl.BlockSpec((1, 3, tm, LANES), lambda i, j: (jnp.minimum(j, 1), 0, i % nseq_blk, 0)),
            pl.BlockSpec((tm, tm), lambda i, j: (0, 0), pipeline_mode=pl.Buffered(1)),
        ],
        out_specs=[
            pl.BlockSpec((H, tm, hd), lambda i, j: (0, i, 0)),
            pl.BlockSpec((H, tm, hd), lambda i, j: (0, i, 0)),
            pl.BlockSpec((tm // ATT_BK, H, hda, ATT_BK), lambda i, j: (i, 0, 0, 0)),
            pl.BlockSpec((tm, W), lambda i, j: (i, 0)),
            pl.BlockSpec((nch, SSM_CHUNK * W), lambda i, j: (i, 0)),
        ],
        out_shape=[
            jax.ShapeDtypeStruct((H, T, hd), BF16),
            jax.ShapeDtypeStruct((H, T, hd), BF16),
            jax.ShapeDtypeStruct((T // ATT_BK, H, hda, ATT_BK), BF16),
            jax.ShapeDtypeStruct((T, W), F32),
            jax.ShapeDtypeStruct((T // SSM_CHUNK, SSM_CHUNK * W), BF16),
        ],
        compiler_params=_cparams(("arbitrary", "arbitrary")),
        name="inproj",
    )(xb2d, w_qku, wvt, rope, perm)


def _attn_kernel(lam_ref, sub_ref, q_ref, k_ref, vt_ref, o_ref, q1t_ref, q2t_ref, o1_ref, o2_ref, s_ref,
                 *, lam_init, bq, unroll):
    nk = vt_ref.shape[0]
    L, hd = q_ref.shape
    nq = L // bq

    lv = lam_ref[...]
    d1 = jnp.sum(lv[0:1] * lv[1:2], axis=1, keepdims=True)
    d2 = jnp.sum(lv[2:3] * lv[3:4], axis=1, keepdims=True)
    lam = jnp.exp(d1) - jnp.exp(d2) + lam_init

    def transpose_q(qi, carry):
        qt = q_ref[pl.ds(pl.multiple_of(qi * bq, bq), bq), :].astype(F32).T
        row = lax.broadcasted_iota(jnp.int32, qt.shape, 0)
        q1t_ref[qi] = jnp.where(row < DIFF_HEAD_DIM, qt, 0.0).astype(BF16)
        q2t_ref[qi] = jnp.where(row >= DIFF_HEAD_DIM, qt, 0.0).astype(BF16)
        return carry

    lax.fori_loop(0, nq, transpose_q, 0)
    o1_ref[...] = jnp.zeros_like(o1_ref)
    o2_ref[...] = jnp.zeros_like(o2_ref)

    def scores(qi, c, slot):
        kb = k_ref[pl.ds(pl.multiple_of(c * ATT_BK, ATT_BK), ATT_BK), :]
        tops = []
        for br, qt_ref in enumerate((q1t_ref, q2t_ref)):
            s = jnp.dot(kb, qt_ref[qi], preferred_element_type=F32)
            s_ref[slot, br] = s
            tops.append(jnp.max(s, axis=0, keepdims=True))
        return tuple(tops)

    def update(c, slot, tops, ms):
        vtb = vt_ref[c]
        new_ms = []
        for br, o_acc in enumerate((o1_ref, o2_ref)):
            m_new = jnp.maximum(ms[br], tops[br])
            alpha = jnp.exp2(ms[br] - m_new)
            p = jnp.exp2(s_ref[slot, br] - m_new).astype(BF16)
            pv = jnp.dot(vtb, p, preferred_element_type=F32)
            o_acc[...] = alpha * o_acc[...] + pv
            new_ms.append(m_new)
        return tuple(new_ms)

    def finish(qi):
        l1 = o1_ref[pl.ds(hd, 1), :]
        l2 = o2_ref[pl.ds(hd, 1), :]
        o = o1_ref[pl.ds(0, hd), :] / l1 - lam * (o2_ref[pl.ds(0, hd), :] / l2)
        msq = jnp.mean(o * o, axis=0, keepdims=True)
        o = o * lax.rsqrt(msq + LN_EPS) * sub_ref[...]
        o = o * (1.0 - lam_init)
        o_ref[pl.ds(pl.multiple_of(qi * bq, bq), bq), :] = o.T.astype(o_ref.dtype)
        o1_ref[...] = jnp.zeros_like(o1_ref)
        o2_ref[...] = jnp.zeros_like(o2_ref)

    neg = jnp.full((1, bq), NEG_BIG, F32)

    def body_long(t, carry):
        ms, tops = carry
        qi = t // trips_per_q
        c0 = unroll * (t % trips_per_q)
        last = c0 + unroll == nk
        for j in range(unroll):
            if j + 1 < unroll:
                nxt_tops = scores(qi, c0 + j + 1, (j + 1) % 2)
            else:
                nxt_q = jnp.where(last, jnp.minimum(qi + 1, nq - 1), qi)
                nxt_c = jnp.where(last, 0, c0 + unroll)
                nxt_tops = scores(nxt_q, nxt_c, 0)
            ms = update(c0 + j, j % 2, tops, ms)
            tops = nxt_tops

        @pl.when(last)
        def _():
            finish(qi)

        ms = tuple(jnp.where(last, neg, m) for m in ms)
        return ms, tops

    def body_short(t, carry):
        ms, tops = carry
        q0 = t * (unroll // nk)
        for j in range(unroll):
            qi, c = q0 + j // nk, j % nk
            if j + 1 < unroll:
                nxt_tops = scores(q0 + (j + 1) // nk, (j + 1) % nk, (j + 1) % 2)
            else:
                nxt_tops = scores(jnp.minimum(q0 + unroll // nk, nq - 1), 0, 0)
            ms = update(c, j % 2, tops, ms)
            tops = nxt_tops
            if c == nk - 1:
                finish(qi)
                ms = (neg, neg)
        return ms, tops

    tops0 = scores(0, 0, 0)
    if unroll <= nk:
        trips_per_q = nk // unroll
        lax.fori_loop(0, nq * trips_per_q, body_long, ((neg, neg), tops0))
    else:
        lax.fori_loop(0, nq * nk // unroll, body_short, ((neg, neg), tops0))


def _attention(q, k, vt, lam_vecs, sub_col, lam_init, seq_len):
    H, T, hd = q.shape
    L = seq_len
    B = T // L
    hda = vt.shape[2]
    nk = L // ATT_BK
    bq = min(ATT_BQ, L)
    nq = L // bq
    if nk >= ATT_UNROLL:
        unroll = ATT_UNROLL
        assert nk % unroll == 0
    else:
        unroll = nk * min(nq, ATT_UNROLL_SHORT // nk)
        assert unroll >= nk and (nq * nk) % unroll == 0
    assert unroll % 2 == 0
    seq = lambda b, h: (h, b, 0)
    return pl.pallas_call(
        functools.partial(_attn_kernel, lam_init=lam_init, bq=bq, unroll=unroll),
        grid=(B, H),
        in_specs=[
            pl.BlockSpec(lam_vecs.shape, lambda b, h: (0, 0)),
            pl.BlockSpec(sub_col.shape, lambda b, h: (0, 0)),
            pl.BlockSpec((None, L, hd), seq),
            pl.BlockSpec((None, L, hd), seq),
            pl.BlockSpec((nk, None, hda, ATT_BK), lambda b, h: (b, h, 0, 0)),
        ],
        out_specs=pl.BlockSpec((None, L, hd), seq),
        out_shape=jax.ShapeDtypeStruct((H, T, hd), BF16),
        scratch_shapes=[
            pltpu.VMEM((nq, hd, bq), BF16),
            pltpu.VMEM((nq, hd, bq), BF16),
            pltpu.VMEM((hda, bq), F32),
            pltpu.VMEM((hda, bq), F32),
            pltpu.VMEM((2, 2, ATT_BK, bq), F32),
        ],
        compiler_params=_cparams(("arbitrary", "arbitrary")),
        name="diff_attn",
    )(lam_vecs, sub_col, q, k, vt)


def _ssm_kernel(u_ref, p_ref, pt_ref, pw_ref, mat_ref, r_ref, lre_ref, lim_ref, y_ref,
                ug_ref, yg_ref, w_ref, m_ref, v_ref, sre, sim, hfre, hfim, hbre, hbim, *, cps):
    o = pl.program_id(1)

    @pl.when(o == 0)
    def _():
        _ssm_compute(u_ref, p_ref, pw_ref, mat_ref, r_ref, lre_ref, lim_ref,
                     ug_ref, yg_ref, w_ref, m_ref, v_ref, sre, sim, hfre, hfim, hbre, hbim, cps)

    res = jnp.dot(yg_ref[o], pt_ref[...], preferred_element_type=F32)
    for t8 in range(SUBLANES):
        y_ref[:, t8, :] = res[:, t8 * LANES:(t8 + 1) * LANES]


def _ssm_compute(u_ref, p_ref, pw_ref, mat_ref, r_ref, lre_ref, lim_ref,
                 ug_ref, yg_ref, w_ref, m_ref, v_ref, sre, sim, hfre, hfim, hbre, hbim, cps):
    gb = ug_ref.shape[0]
    nc = ug_ref.shape[1]
    half = LANES // 2
    T = SSM_CHUNK
    n_oct = T // SUBLANES
    oct_lanes = SUBLANES * LANES

    for oc in range(n_oct):
        z = u_ref[:, pl.ds(oc * oct_lanes, oct_lanes)]
        zg = jnp.dot(z, p_ref[...], preferred_element_type=F32).astype(BF16)
        for gi in range(gb):
            ug_ref[gi, :, pl.ds(oc * LANES, LANES)] = zg[:, gi * LANES:(gi + 1) * LANES]

    C = SSM_GROUP
    for gi in range(gb):
        r = r_ref[gi]
        b_re, b_im, c_re, c_im = (mat_ref[gi, i] for i in range(4))
        for s_ in range(T):
            rows = pl.ds(s_ * C, C)
            off = (T - 1 - s_) * C
            m_ref[gi, rows, :] = r[:, off:off + T * C].astype(BF16)
            wr, wi, vr, vi = (pw_ref[gi, i, pl.ds(s_, 1), :] for i in range(4))
            w_ref[gi, rows, pl.ds(0, LANES)] = (wr * b_re - wi * b_im).astype(BF16)
            w_ref[gi, rows, pl.ds(LANES, LANES)] = (wr * b_im + wi * b_re).astype(BF16)
            v_ref[gi, rows, pl.ds(0, LANES)] = (vr * c_re - vi * c_im).astype(BF16)
            v_ref[gi, rows, pl.ds(LANES, LANES)] = (-(vr * c_im + vi * c_re)).astype(BF16)

    for gi in range(gb):
        s = jnp.dot(ug_ref[gi], w_ref[gi], preferred_element_type=F32)
        sre[pl.ds(gi, nc, stride=gb), :] = s[:, :LANES]
        sim[pl.ds(gi, nc, stride=gb), :] = s[:, LANES:]

    zero = jnp.zeros((gb, LANES), F32)
    lre = lre_ref[...]
    lim = lim_ref[...]
    fwd_lane = lax.broadcasted_iota(jnp.int32, (gb, LANES), 1) < half

    def body(j, carry):
        hr, hi = carry
        fresh = (j % cps) == 0
        hr = jnp.where(fresh, 0.0, hr)
        hi = jnp.where(fresh, 0.0, hi)
        jf = pl.multiple_of(j * gb, gb)
        jb = pl.multiple_of((nc - 1 - j) * gb, gb)
        hfre[pl.ds(jf, gb), :] = hr
        hfim[pl.ds(jf, gb), :] = hi
        hbre[pl.ds(jb, gb), :] = hr
        hbim[pl.ds(jb, gb), :] = hi
        sr = jnp.where(fwd_lane, sre[pl.ds(jf, gb), :], sre[pl.ds(jb, gb), :])
        si = jnp.where(fwd_lane, sim[pl.ds(jf, gb), :], sim[pl.ds(jb, gb), :])
        return lre * hr - lim * hi + sr, lre * hi + lim * hr + si

    lax.fori_loop(0, nc, body, (zero, zero))

    fwd_col = lax.broadcasted_iota(jnp.int32, (nc, LANES), 1) < half
    for gi in range(gb):
        hr = jnp.where(fwd_col, hfre[pl.ds(gi, nc, stride=gb), :], hbre[pl.ds(gi, nc, stride=gb), :])
        hi = jnp.where(fwd_col, hfim[pl.ds(gi, nc, stride=gb), :], hbim[pl.ds(gi, nc, stride=gb), :])
        h = jnp.concatenate([hr, hi], axis=1).astype(BF16)
        y = jnp.dot(ug_ref[gi], m_ref[gi], preferred_element_type=F32)
        y = y + lax.dot_general(h, v_ref[gi], (((1,), (1,)), ((), ())),
                                preferred_element_type=F32)
        for oc in range(n_oct):
            yg_ref[oc, :, pl.ds(gi * LANES, LANES)] = y[:, oc * LANES:(oc + 1) * LANES].astype(yg_ref.dtype)


def _lane_regroup_matrix():
    n = SUBLANES * LANES
    i = jnp.arange(n)
    t8, gl, c = i // LANES, (i % LANES) // SSM_GROUP, i % SSM_GROUP
    dest = gl * LANES + t8 * SSM_GROUP + c
    return (dest[:, None] == i[None, :]).astype(BF16)


def _ssm(ub, pw_t, mat_t, r_t, lre, lim, seq_len):
    T = SSM_CHUNK
    nc = ub.shape[0]
    Wd = ub.shape[1] // T
    Ttok = nc * T
    cps = seq_len // T
    gb = SSM_GBLK
    assert gb * SSM_GROUP == LANES and T % SUBLANES == 0 and nc % cps == 0
    n_slab = Wd // LANES
    n_oct = T // SUBLANES
    K = T * SSM_GROUP
    ncols = 2 * pw_t.shape[3]
    pmat = _lane_regroup_matrix()
    once = pl.Buffered(1)
    fixed2 = lambda g, o: (0, 0)
    per_slab = lambda g, o: (g, 0, 0)

    y = pl.pallas_call(
        functools.partial(_ssm_kernel, cps=cps),
        grid=(n_slab, n_oct),
        in_specs=[
            pl.BlockSpec((nc, T * LANES), lambda g, o: (0, g)),
            pl.BlockSpec(pmat.shape, fixed2, pipeline_mode=once),
            pl.BlockSpec(pmat.shape, fixed2, pipeline_mode=once),
            pl.BlockSpec((gb,) + pw_t.shape[1:], lambda g, o: (g, 0, 0, 0)),
            pl.BlockSpec((gb,) + mat_t.shape[1:], lambda g, o: (g, 0, 0, 0)),
            pl.BlockSpec((gb,) + r_t.shape[1:], per_slab),
            pl.BlockSpec((gb, LANES), lambda g, o: (g, 0)),
            pl.BlockSpec((gb, LANES), lambda g, o: (g, 0)),
        ],
        out_specs=pl.BlockSpec((nc, SUBLANES, LANES), lambda g, o: (0, o, g)),
        out_shape=jax.ShapeDtypeStruct((nc, T, Wd), F32),
        scratch_shapes=[
            pltpu.VMEM((gb, nc, K), BF16),
            pltpu.VMEM((n_oct, nc, SUBLANES * LANES), BF16),
            pltpu.VMEM((gb, K, ncols), BF16),
            pltpu.VMEM((gb, K, K), BF16),
            pltpu.VMEM((gb, K, ncols), BF16),
        ] + [pltpu.VMEM((nc * gb, LANES), F32)] * 6,
        compiler_params=_cparams(("arbitrary", "arbitrary")),
        name="s5_chunked",
    )(ub, pmat, pmat.T, pw_t, mat_t, r_t, lre, lim)
    return y.reshape(Ttok, Wd)


def _ssm_tables(a_re, a_im, log_dt, b_re, b_im, c_re, c_im):
    T = SSM_CHUNK
    G, P = a_re.shape[1], a_re.shape[2]
    C = b_re.shape[3]
    dt = jnp.exp(log_dt)[..., None]
    zr = a_re * dt
    zi = a_im * dt
    tau = jnp.arange(T + 1, dtype=F32)[:, None]
    mag = jnp.exp(zr[:, :, None] * tau)
    pr = mag * jnp.cos(zi[:, :, None] * tau)
    pi = mag * jnp.sin(zi[:, :, None] * tau)
    ab_re, ab_im = pr[:, :, 1], pi[:, :, 1]
    den = a_re * a_re + a_im * a_im
    nr = ab_re - 1.0
    f_re = ((nr * a_re + ab_im * a_im) / den)[:, :, None, :]
    f_im = ((ab_im * a_re - nr * a_im) / den)[:, :, None, :]
    bt_re = b_re.transpose(0, 1, 3, 2)
    bt_im = b_im.transpose(0, 1, 3, 2)
    bbt_re = f_re * bt_re - f_im * bt_im
    bbt_im = f_re * bt_im + f_im * bt_re

    fb = lambda f, b: jnp.concatenate([f, b], axis=-1)
    pw_t = jnp.stack([fb(pr[0, :, T - 1::-1], pr[1, :, :T]), fb(pi[0, :, T - 1::-1], pi[1, :, :T]),
                      fb(pr[0, :, 1:], pr[1, :, :0:-1]), fb(pi[0, :, 1:], pi[1, :, :0:-1])], axis=1)
    mat_t = jnp.stack([fb(bbt_re[0], bbt_re[1]), fb(bbt_im[0], bbt_im[1]),
                       fb(c_re[0], c_re[1]), fb(c_im[0], c_im[1])], axis=1)

    p0r = pr[:, :, :T, None, :]
    p0i = pi[:, :, :T, None, :]
    cp_re = c_re[:, :, None] * p0r - c_im[:, :, None] * p0i
    cp_im = c_re[:, :, None] * p0i + c_im[:, :, None] * p0r
    cb = jnp.einsum('dgtcq,dgkq->dgtck', jnp.concatenate([cp_re, -cp_im], axis=-1),
                    jnp.concatenate([bbt_re, bbt_im], axis=-1), precision=lax.Precision.HIGH)

    kern = jnp.concatenate([cb[1, :, T - 1:0:-1], (cb[0, :, 0] + cb[1, :, 0])[:, None], cb[0, :, 1:],
                            jnp.zeros_like(cb[0, :, :1])], axis=1)
    r_t = kern.transpose(0, 3, 1, 2).reshape(G, C, 2 * T * C)

    lre = jnp.concatenate([pr[0, :, T], pr[1, :, T]], axis=1)
    lim = jnp.concatenate([pi[0, :, T], pi[1, :, T]], axis=1)
    return pw_t, mat_t, r_t, lre, lim


def _layer_norm(r, g, b):
    mu = jnp.mean(r, axis=-1, keepdims=True)
    d = r - mu
    var = jnp.mean(d * d, axis=-1, keepdims=True)
    return d * lax.rsqrt(var + LN_EPS) * g + b


def _mix_kernel(x_ref, a_ref, y_ref, u_ref, d_ref, wglu_ref, wout_ref, g_ref, b_ref, o_ref, ob_ref,
                *, alpha):
    n_heads, tm, hd = a_ref.shape
    wa = n_heads * hd
    for half in range(2):
        rows = pl.ds(half * (tm // 2), tm // 2)
        yy = y_ref[rows, :] + d_ref[...] * u_ref[rows, :]
        z = jax.nn.gelu(yy)
        gate = jax.nn.sigmoid(jnp.dot(z.astype(BF16), wglu_ref[...], preferred_element_type=F32))
        s = (z * gate).astype(BF16)
        a = jnp.concatenate([a_ref[h, rows, :] for h in range(n_heads)], axis=1)
        mixed = jnp.dot(a, wout_ref[pl.ds(0, wa), :], preferred_element_type=F32)
        mixed = mixed + jnp.dot(s, wout_ref[pl.ds(wa, s.shape[1]), :], preferred_element_type=F32)
        r = alpha * x_ref[rows, :] + mixed
        o = _layer_norm(r, g_ref[...], b_ref[...])
        o_ref[rows, :] = o
        ob_ref[rows, :] = o.astype(BF16)


def _mix(x2d, attn, y, u, d, wglu, wout, g, b, alpha, tm=512):
    T, D = x2d.shape
    W = y.shape[1]
    row = lambda i: (i, 0)
    fixed = lambda i: (0, 0)
    once = pl.Buffered(1)
    return pl.pallas_call(
        functools.partial(_mix_kernel, alpha=alpha),
        grid=(T // tm,),
        in_specs=[
            pl.BlockSpec((tm, D), row),
            pl.BlockSpec((attn.shape[0], tm, attn.shape[2]), lambda i: (0, i, 0)),
            pl.BlockSpec((tm, W), row),
            pl.BlockSpec((tm, W), row),
            pl.BlockSpec((1, W), fixed),
            pl.BlockSpec(wglu.shape, fixed, pipeline_mode=once),
            pl.BlockSpec(wout.shape, fixed, pipeline_mode=once),
            pl.BlockSpec((1, D), fixed),
            pl.BlockSpec((1, D), fixed),
        ],
        out_specs=[pl.BlockSpec((tm, D), row), pl.BlockSpec((tm, D), row)],
        out_shape=[jax.ShapeDtypeStruct((T, D), F32), jax.ShapeDtypeStruct((T, D), BF16)],
        compiler_params=_cparams(("arbitrary",)),
        name="glu_outproj_ln",
    )(x2d, attn, y, u, d, wglu, wout, g, b)


def _ffn_up_kernel(x_ref, wg_ref, wu_ref, h_ref):
    x = x_ref[...]
    g = jnp.dot(x, wg_ref[...], preferred_element_type=F32)
    up = jnp.dot(x, wu_ref[...], preferred_element_type=F32)
    h_ref[...] = (jax.nn.silu(g) * up).astype(h_ref.dtype)


def _ffn_up(xb, wg, wu, tm=1024, tf=512):
    T, D = xb.shape
    F = wg.shape[1]
    return pl.pallas_call(
        _ffn_up_kernel,
        grid=(T // tm, F // tf),
        in_specs=[
            pl.BlockSpec((tm, D), lambda i, j: (i, 0)),
            pl.BlockSpec((D, tf), lambda i, j: (0, j)),
            pl.BlockSpec((D, tf), lambda i, j: (0, j)),
        ],
        out_specs=pl.BlockSpec((tm, tf), lambda i, j: (i, j)),
        out_shape=jax.ShapeDtypeStruct((T, F), BF16),
        compiler_params=_cparams(("arbitrary", "arbitrary")),
        name="ffn_up",
    )(xb, wg, wu)


def _ffn_down_kernel(h_ref, wd_ref, x_ref, g_ref, b_ref, o_ref, *, alpha):
    down = jnp.dot(h_ref[...], wd_ref[...], preferred_element_type=F32)
    o_ref[...] = _layer_norm(alpha * x_ref[...] + down, g_ref[...], b_ref[...])


def _ffn_down(h, wd, x1, g, b, alpha, tm=256):
    T, F = h.shape
    D = wd.shape[1]
    return pl.pallas_call(
        functools.partial(_ffn_down_kernel, alpha=alpha),
        grid=(T // tm,),
        in_specs=[
            pl.BlockSpec((tm, F), lambda i: (i, 0)),
            pl.BlockSpec((F, D), lambda i: (0, 0), pipeline_mode=pl.Buffered(1)),
            pl.BlockSpec((tm, D), lambda i: (i, 0)),
            pl.BlockSpec((1, D), lambda i: (0, 0)),
            pl.BlockSpec((1, D), lambda i: (0, 0)),
        ],
        out_specs=pl.BlockSpec((tm, D), lambda i: (i, 0)),
        out_shape=jax.ShapeDtypeStruct((T, D), F32),
        compiler_params=_cparams(("arbitrary",)),
        name="ffn_down_ln",
    )(h, wd, x1, g, b)


def _rope_tables(L):
    half = ROT_DIM // 2
    inv = ROPE_THETA ** (-jnp.arange(0, ROT_DIM, 2, dtype=F32) / ROT_DIM)
    ang = jnp.arange(L, dtype=F32)[:, None] * inv[None, :]
    cos, sin = jnp.cos(ang), jnp.sin(ang)
    lane = jnp.arange(LANES)
    d = lane % DIFF_HEAD_DIM
    freq = jnp.arange(half)[:, None]
    first = ((d < half)[None, :] & (d[None, :] == freq)).astype(F32)
    second = (((d >= half) & (d < ROT_DIM))[None, :] & (d[None, :] - half == freq)).astype(F32)
    spread = functools.partial(jnp.dot, precision=lax.Precision.HIGHEST)
    c = spread(cos, first + second) + (d >= ROT_DIM).astype(F32)[None, :]
    sa = -spread(sin, first)
    sb = spread(sin, second)
    tab = jnp.stack([c, sa, sb])
    return jnp.stack([tab * (DIFF_HEAD_DIM ** -0.5 * math.log2(math.e)), tab])


def _prepare_layer(p):
    (w_in, lq1, lk1, lq2, lk2, subln_w, a_re, a_im, log_dt, b_re, b_im, c_re, c_im, ssm_d, w_glu,
     w_out, ln1_g, ln1_b, w_gate, w_up, w_down, ln2_g, ln2_b) = p
    W = w_glu.shape[0]
    D = w_in.shape[0]
    row = lambda t, n: t.reshape(1, n).astype(F32)
    return dict(
        w_qku=jnp.concatenate([w_in[:, :2 * W], w_in[:, 3 * W:]], axis=1).astype(BF16),
        wvt=w_in[:, 2 * W:3 * W].T.astype(BF16),
        lam_vecs=jnp.stack([lq1, lk1, lq2, lk2]).astype(F32),
        sub_col=subln_w.astype(F32).reshape(-1, 1),
        ssm=_ssm_tables(a_re, a_im, log_dt, b_re, b_im, c_re, c_im),
        ssm_d=row(ssm_d, W), w_glu=w_glu.astype(BF16), w_out=w_out.astype(BF16),
        ln1_g=row(ln1_g, D), ln1_b=row(ln1_b, D),
        w_gate=w_gate.astype(BF16), w_up=w_up.astype(BF16), w_down=w_down.astype(BF16),
        ln2_g=row(ln2_g, D), ln2_b=row(ln2_b, D),
    )


def _encoder_layer(x, layer_idx, depth, pp):
    B, L, D = x.shape
    T = B * L
    W = pp["w_glu"].shape[0]
    alpha = (2 * depth) ** 0.25
    lam_init = 0.8 - 0.6 * math.exp(-0.3 * layer_idx)

    x2d = x.reshape(T, D)
    q, k, vt, u, ub = _inproj(x2d.astype(BF16), pp["w_qku"], pp["wvt"], _rope_tables(L), L)
    attn = _attention(q, k, vt, pp["lam_vecs"], pp["sub_col"], lam_init, L)
    y = _ssm(ub, *pp["ssm"], L)
    x1, x1b = _mix(x2d, attn, y, u, pp["ssm_d"], pp["w_glu"], pp["w_out"], pp["ln1_g"], pp["ln1_b"], alpha)
    h = _ffn_up(x1b, pp["w_gate"], pp["w_up"])
    out = _ffn_down(h, pp["w_down"], x1, pp["ln2_g"], pp["ln2_b"], alpha)
    return out.reshape(B, L, D)


def kernel(x_prompt, x_sample, w_in, lambda_q1, lambda_k1, lambda_q2, lambda_k2, subln_w, ssm_a_re, ssm_a_im, ssm_log_dt, ssm_b_re, ssm_b_im, ssm_c_re, ssm_c_im, ssm_d, w_glu, w_out, ln1_g, ln1_b, w_gate, w_up, w_down, ln2_g, ln2_b):
    params = (w_in, lambda_q1, lambda_k1, lambda_q2, lambda_k2, subln_w,
              ssm_a_re, ssm_a_im, ssm_log_dt, ssm_b_re, ssm_b_im, ssm_c_re, ssm_c_im, ssm_d, w_glu,
              w_out, ln1_g, ln1_b, w_gate, w_up, w_down, ln2_g, ln2_b)
    depth = w_in.shape[0]
    y_prompt, y_sample = x_prompt, x_sample
    for layer in range(depth):
        lp = _prepare_layer(tuple(t[layer] for t in params))
        y_prompt = _encoder_layer(y_prompt, layer, depth, lp)
        y_sample = _encoder_layer(y_sample, layer, depth, lp)
    return (y_prompt, y_sample)
```

```python
import functools
import math

import jax
import jax.numpy as jnp
from jax import lax
from jax.experimental import pallas as pl
from jax.experimental.pallas import tpu as pltpu

F32 = jnp.float32
BF16 = jnp.bfloat16

DIFF_HEAD_DIM = 64
ROT_DIM = DIFF_HEAD_DIM // 4
ROPE_THETA = 500000.0
SSM_GROUP = 16
SSM_STATE = 64
LN_EPS = 1e-5

LANES = 128
SUBLANES = 8
VMEM_LIMIT_BYTES = 56 * 1024 * 1024

SSM_CHUNK = 32
SSM_GBLK = SUBLANES
ATT_BK = 512
ATT_BQ = 512
ATT_UNROLL = 8
ATT_UNROLL_SHORT = 8
ATT_ONES_ROWS = 16
NEG_BIG = -1e30


def _cparams(sem):
    return pltpu.CompilerParams(dimension_semantics=sem, vmem_limit_bytes=VMEM_LIMIT_BYTES)


def _inproj_kernel(xb_ref, w_ref, wvt_ref, rope_ref, perm_ref, q_ref, k_ref, vt_ref, u_ref, ub_ref):
    j = pl.program_id(1)

    def rope_to(out_ref):
        acc = jnp.dot(xb_ref[...], w_ref[...], preferred_element_type=F32)
        c = rope_ref[0, 0]
        sa = rope_ref[0, 1]
        sb = rope_ref[0, 2]
        for cb in range(acc.shape[1] // LANES):
            xs = acc[:, cb * LANES:(cb + 1) * LANES]
            nxt = pltpu.roll(xs, LANES - ROT_DIM // 2, axis=1)
            prv = pltpu.roll(xs, ROT_DIM // 2, axis=1)
            out_ref[cb] = (xs * c + nxt * sa + prv * sb).astype(out_ref.dtype)

    @pl.when(j == 0)
    def _():
        rope_to(q_ref)

    @pl.when(j == 1)
    def _():
        rope_to(k_ref)

    @pl.when(j == 2)
    def _():
        vt = lax.dot_general(wvt_ref[...], xb_ref[...], (((1,), (1,)), ((), ())),
                             preferred_element_type=F32)
        hd = 2 * DIFF_HEAD_DIM
        ones = jnp.ones((vt_ref.shape[2] - hd, ATT_BK), vt_ref.dtype)
        for cc in range(vt_ref.shape[0]):
            for h in range(vt_ref.shape[1]):
                vt_ref[cc, h, pl.ds(0, hd), :] = (
                    vt[h * hd:(h + 1) * hd, cc * ATT_BK:(cc + 1) * ATT_BK].astype(vt_ref.dtype))
                vt_ref[cc, h, pl.ds(hd, ones.shape[0]), :] = ones

    @pl.when(j == 3)
    def _():
        u = jnp.dot(xb_ref[...], w_ref[...], preferred_element_type=F32)
        u_ref[...] = u
        ut = jnp.dot(perm_ref[...], u.astype(BF16), preferred_element_type=F32).astype(ub_ref.dtype)
        T = SSM_CHUNK
        nch = ub_ref.shape[0]
        for slab in range(u.shape[1] // LANES):
            for t in range(T):
                ub_ref[:, pl.ds((slab * T + t) * LANES, LANES)] = (
                    ut[t * nch:(t + 1) * nch, slab * LANES:(slab + 1) * LANES])


def _inproj(xb2d, w_qku, wvt, rope, seq_len, tm=1024):
    T, D = xb2d.shape
    W = wvt.shape[0]
    hd = 2 * DIFF_HEAD_DIM
    H = W // hd
    hda = hd + ATT_ONES_ROWS
    nseq_blk = seq_len // tm
    grid = (T // tm, 4)
    nch = tm // SSM_CHUNK
    r = jnp.arange(tm)
    perm = ((r % nch) * SSM_CHUNK + r // nch)[:, None] == r[None, :]
    perm = perm.astype(BF16)
    wmap = lambda i, j: (0, jnp.where(j >= 2, j - 1, j))
    return pl.pallas_call(
        _inproj_kernel,
        grid=grid,
        in_specs=[
            pl.BlockSpec((tm, D), lambda i, j: (i, 0)),
            pl.BlockSpec((D, W), wmap),
            pl.BlockSpec((W, D), lambda i, j: (0, 0), pipeline_mode=pl.Buffered(1)),
            pl.BlockSpec((1, 3, tm, LANES), lambda i, j: (jnp.minimum(j, 1), 0, i % nseq_blk, 0)),
            pl.BlockSpec((tm, tm), lambda i, j: (0, 0), pipeline_mode=pl.Buffered(1)),
        ],
        out_specs=[
            pl.BlockSpec((H, tm, hd), lambda i, j: (0, i, 0)),
            pl.BlockSpec((H, tm, hd), lambda i, j: (0, i, 0)),
            pl.BlockSpec((tm // ATT_BK, H, hda, ATT_BK), lambda i, j: (i, 0, 0, 0)),
            pl.BlockSpec((tm, W), lambda i, j: (i, 0)),
            pl.BlockSpec((nch, SSM_CHUNK * W), lambda i, j: (i, 0)),
        ],
        out_shape=[
            jax.ShapeDtypeStruct((H, T, hd), BF16),
            jax.ShapeDtypeStruct((H, T, hd), BF16),
            jax.ShapeDtypeStruct((T // ATT_BK, H, hda, ATT_BK), BF16),
            jax.ShapeDtypeStruct((T, W), F32),
            jax.ShapeDtypeStruct((T // SSM_CHUNK, SSM_CHUNK * W), BF16),
        ],
        compiler_params=_cparams(("arbitrary", "arbitrary")),
        name="inproj",
    )(xb2d, w_qku, wvt, rope, perm)


def _attn_kernel(lam_ref, sub_ref, q_ref, k_ref, vt_ref, o_ref, q1t_ref, q2t_ref, o1_ref, o2_ref, s_ref,
                 *, lam_init, bq, unroll):
    nk = vt_ref.shape[0]
    L, hd = q_ref.shape
    nq = L // bq

    lv = lam_ref[...]
    d1 = jnp.sum(lv[0:1] * lv[1:2], axis=1, keepdims=True)
    d2 = jnp.sum(lv[2:3] * lv[3:4], axis=1, keepdims=True)
    lam = jnp.exp(d1) - jnp.exp(d2) + lam_init

    def transpose_q(qi, carry):
        qt = q_ref[pl.ds(pl.multiple_of(qi * bq, bq), bq), :].astype(F32).T
        row = lax.broadcasted_iota(jnp.int32, qt.shape, 0)
        q1t_ref[qi] = jnp.where(row < DIFF_HEAD_DIM, qt, 0.0).astype(BF16)
        q2t_ref[qi] = jnp.where(row >= DIFF_HEAD_DIM, qt, 0.0).astype(BF16)
        return carry

    lax.fori_loop(0, nq, transpose_q, 0)
    o1_ref[...] = jnp.zeros_like(o1_ref)
    o2_ref[...] = jnp.zeros_like(o2_ref)

    def scores(qi, c, slot):
        kb = k_ref[pl.ds(pl.multiple_of(c * ATT_BK, ATT_BK), ATT_BK), :]
        tops = []
        for br, qt_ref in enumerate((q1t_ref, q2t_ref)):
            s = jnp.dot(kb, qt_ref[qi], preferred_element_type=F32)
            s_ref[slot, br] = s
            tops.append(jnp.max(s, axis=0, keepdims=True))
        return tuple(tops)

    def update(c, slot, tops, ms):
        vtb = vt_ref[c]
        new_ms = []
        for br, o_acc in enumerate((o1_ref, o2_ref)):
            m_new = jnp.maximum(ms[br], tops[br])
            alpha = jnp.exp2(ms[br] - m_new)
            p = jnp.exp2(s_ref[slot, br] - m_new).astype(BF16)
            pv = jnp.dot(vtb, p, preferred_element_type=F32)
            o_acc[...] = alpha * o_acc[...] + pv
            new_ms.append(m_new)
        return tuple(new_ms)

    def finish(qi):
        l1 = o1_ref[pl.ds(hd, 1), :]
        l2 = o2_ref[pl.ds(hd, 1), :]
        o = o1_ref[pl.ds(0, hd), :] / l1 - lam * (o2_ref[pl.ds(0, hd), :] / l2)
        msq = jnp.mean(o * o, axis=0, keepdims=True)
        o = o * lax.rsqrt(msq + LN_EPS) * sub_ref[...]
        o = o * (1.0 - lam_init)
        o_ref[pl.ds(pl.multiple_of(qi * bq, bq), bq), :] = o.T.astype(o_ref.dtype)
        o1_ref[...] = jnp.zeros_like(o1_ref)
        o2_ref[...] = jnp.zeros_like(o2_ref)

    neg = jnp.full((1, bq), NEG_BIG, F32)

    def body_long(t, carry):
        ms, tops = carry
        qi = t // trips_per_q
        c0 = unroll * (t % trips_per_q)
        last = c0 + unroll == nk
        for j in range(unroll):
            if j + 1 < unroll:
                nxt_tops = scores(qi, c0 + j + 1, (j + 1) % 2)
            else:
                nxt_q = jnp.where(last, jnp.minimum(qi + 1, nq - 1), qi)
                nxt_c = jnp.where(last, 0, c0 + unroll)
                nxt_tops = scores(nxt_q, nxt_c, 0)
            ms = update(c0 + j, j % 2, tops, ms)
            tops = nxt_tops

        @pl.when(last)
        def _():
            finish(qi)

        ms = tuple(jnp.where(last, neg, m) for m in ms)
        return ms, tops

    def body_short(t, carry):
        ms, tops = carry
        q0 = t * (unroll // nk)
        for j in range(unroll):
            qi, c = q0 + j // nk, j % nk
            if j + 1 < unroll:
                nxt_tops = scores(q0 + (j + 1) // nk, (j + 1) % nk, (j + 1) % 2)
            else:
                nxt_tops = scores(jnp.minimum(q0 + unroll // nk, nq - 1), 0, 0)
            ms = update(c, j % 2, tops, ms)
            tops = nxt_tops
            if c == nk - 1:
                finish(qi)
                ms = (neg, neg)
        return ms, tops

    tops0 = scores(0, 0, 0)
    if unroll <= nk:
        trips_per_q = nk // unroll
        lax.fori_loop(0, nq * trips_per_q, body_long, ((neg, neg), tops0))
    else:
        lax.fori_loop(0, nq * nk // unroll, body_short, ((neg, neg), tops0))


def _attention(q, k, vt, lam_vecs, sub_col, lam_init, seq_len):
    H, T, hd = q.shape
    L = seq_len
    B = T // L
    hda = vt.shape[2]
    nk = L // ATT_BK
    bq = min(ATT_BQ, L)
    nq = L // bq
    if nk >= ATT_UNROLL:
        unroll = ATT_UNROLL
        assert nk % unroll == 0
    else:
        unroll = nk * min(nq, ATT_UNROLL_SHORT // nk)
        assert unroll >= nk and (nq * nk) % unroll == 0
    assert unroll % 2 == 0
    seq = lambda b, h: (h, b, 0)
    return pl.pallas_call(
        functools.partial(_attn_kernel, lam_init=lam_init, bq=bq, unroll=unroll),
        grid=(B, H),
        in_specs=[
            pl.BlockSpec(lam_vecs.shape, lambda b, h: (0, 0)),
            pl.BlockSpec(sub_col.shape, lambda b, h: (0, 0)),
            pl.BlockSpec((None, L, hd), seq),
            pl.BlockSpec((None, L, hd), seq),
            pl.BlockSpec((nk, None, hda, ATT_BK), lambda b, h: (b, h, 0, 0)),
        ],
        out_specs=pl.BlockSpec((None, L, hd), seq),
        out_shape=jax.ShapeDtypeStruct((H, T, hd), BF16),
        scratch_shapes=[
            pltpu.VMEM((nq, hd, bq), BF16),
            pltpu.VMEM((nq, hd, bq), BF16),
            pltpu.VMEM((hda, bq), F32),
            pltpu.VMEM((hda, bq), F32),
            pltpu.VMEM((2, 2, ATT_BK, bq), F32),
        ],
        compiler_params=_cparams(("arbitrary", "arbitrary")),
        name="diff_attn",
    )(lam_vecs, sub_col, q, k, vt)


def _ssm_kernel(u_ref, p_ref, pt_ref, pw_ref, mat_ref, r_ref, lre_ref, lim_ref, y_ref,
                ug_ref, yg_ref, w_ref, m_ref, v_ref, sre, sim, hfre, hfim, hbre, hbim, *, cps):
    o = pl.program_id(1)

    @pl.when(o == 0)
    def _():
        _ssm_compute(u_ref, p_ref, pw_ref, mat_ref, r_ref, lre_ref, lim_ref,
                     ug_ref, yg_ref, w_ref, m_ref, v_ref, sre, sim, hfre, hfim, hbre, hbim, cps)

    res = jnp.dot(yg_ref[o], pt_ref[...], preferred_element_type=F32)
    for t8 in range(SUBLANES):
        y_ref[:, t8, :] = res[:, t8 * LANES:(t8 + 1) * LANES]


def _ssm_compute(u_ref, p_ref, pw_ref, mat_ref, r_ref, lre_ref, lim_ref,
                 ug_ref, yg_ref, w_ref, m_ref, v_ref, sre, sim, hfre, hfim, hbre, hbim, cps):
    gb = ug_ref.shape[0]
    nc = ug_ref.shape[1]
    half = LANES // 2
    T = SSM_CHUNK
    n_oct = T // SUBLANES
    oct_lanes = SUBLANES * LANES

    for oc in range(n_oct):
        z = u_ref[:, pl.ds(oc * oct_lanes, oct_lanes)]
        zg = jnp.dot(z, p_ref[...], preferred_element_type=F32).astype(BF16)
        for gi in range(gb):
            ug_ref[gi, :, pl.ds(oc * LANES, LANES)] = zg[:, gi * LANES:(gi + 1) * LANES]

    C = SSM_GROUP
    for gi in range(gb):
        r = r_ref[gi]
        b_re, b_im, c_re, c_im = (mat_ref[gi, i] for i in range(4))
        for s_ in range(T):
            rows = pl.ds(s_ * C, C)
            off = (T - 1 - s_) * C
            m_ref[gi, rows, :] = r[:, off:off + T * C].astype(BF16)
            wr, wi, vr, vi = (pw_ref[gi, i, pl.ds(s_, 1), :] for i in range(4))
            w_ref[gi, rows, pl.ds(0, LANES)] = (wr * b_re - wi * b_im).astype(BF16)
            w_ref[gi, rows, pl.ds(LANES, LANES)] = (wr * b_im + wi * b_re).astype(BF16)
            v_ref[gi, rows, pl.ds(0, LANES)] = (vr * c_re - vi * c_im).astype(BF16)
            v_ref[gi, rows, pl.ds(LANES, LANES)] = (-(vr * c_im + vi * c_re)).astype(BF16)

    for gi in range(gb):
        s = jnp.dot(ug_ref[gi], w_ref[gi], preferred_element_type=F32)
        sre[pl.ds(gi, nc, stride=gb), :] = s[:, :LANES]
        sim[pl.ds(gi, nc, stride=gb), :] = s[:, LANES:]

    zero = jnp.zeros((gb, LANES), F32)
    lre = lre_ref[...]
    lim = lim_ref[...]
    fwd_lane = lax.broadcasted_iota(jnp.int32, (gb, LANES), 1) < half

    def body(j, carry):
        hr, hi = carry
        fresh = (j % cps) == 0
        hr = jnp.where(fresh, 0.0, hr)
        hi = jnp.where(fresh, 0.0, hi)
        jf = pl.multiple_of(j * gb, gb)
        jb = pl.multiple_of((nc - 1 - j) * gb, gb)
        hfre[pl.ds(jf, gb), :] = hr
        hfim[pl.ds(jf, gb), :] = hi
        hbre[pl.ds(jb, gb), :] = hr
        hbim[pl.ds(jb, gb), :] = hi
        sr = jnp.where(fwd_lane, sre[pl.ds(jf, gb), :], sre[pl.ds(jb, gb), :])
        si = jnp.where(fwd_lane, sim[pl.ds(jf, gb), :], sim[pl.ds(jb, gb), :])
        return lre * hr - lim * hi + sr, lre * hi + lim * hr + si

    lax.fori_loop(0, nc, body, (zero, zero))

    fwd_col = lax.broadcasted_iota(jnp.int32, (nc, LANES), 1) < half
    for gi in range(gb):
        hr = jnp.where(fwd_col, hfre[pl.ds(gi, nc, stride=gb), :], hbre[pl.ds(gi, nc, stride=gb), :])
        hi = jnp.where(fwd_col, hfim[pl.ds(gi, nc, stride=gb), :], hbim[pl.ds(gi, nc, stride=gb), :])
        h = jnp.concatenate([hr, hi], axis=1).astype(BF16)
        y = jnp.dot(ug_ref[gi], m_ref[gi], preferred_element_type=F32)
        y = y + lax.dot_general(h, v_ref[gi], (((1,), (1,)), ((), ())),
                                preferred_element_type=F32)
        for oc in range(n_oct):
            yg_ref[oc, :, pl.ds(gi * LANES, LANES)] = y[:, oc * LANES:(oc + 1) * LANES].astype(yg_ref.dtype)


def _lane_regroup_matrix():
    n = SUBLANES * LANES
    i = jnp.arange(n)
    t8, gl, c = i // LANES, (i % LANES) // SSM_GROUP, i % SSM_GROUP
    dest = gl * LANES + t8 * SSM_GROUP + c
    return (dest[:, None] == i[None, :]).astype(BF16)


def _ssm(ub, pw_t, mat_t, r_t, lre, lim, seq_len):
    T = SSM_CHUNK
    nc = ub.shape[0]
    Wd = ub.shape[1] // T
    Ttok = nc * T
    cps = seq_len // T
    gb = SSM_GBLK
    assert gb * SSM_GROUP == LANES and T % SUBLANES == 0 and nc % cps == 0
    n_slab = Wd // LANES
    n_oct = T // SUBLANES
    K = T * SSM_GROUP
    ncols = 2 * pw_t.shape[3]
    pmat = _lane_regroup_matrix()
    once = pl.Buffered(1)
    fixed2 = lambda g, o: (0, 0)
    per_slab = lambda g, o: (g, 0, 0)

    y = pl.pallas_call(
        functools.partial(_ssm_kernel, cps=cps),
        grid=(n_slab, n_oct),
        in_specs=[
            pl.BlockSpec((nc, T * LANES), lambda g, o: (0, g)),
            pl.BlockSpec(pmat.shape, fixed2, pipeline_mode=once),
            pl.BlockSpec(pmat.shape, fixed2, pipeline_mode=once),
            pl.BlockSpec((gb,) + pw_t.shape[1:], lambda g, o: (g, 0, 0, 0)),
            pl.BlockSpec((gb,) + mat_t.shape[1:], lambda g, o: (g, 0, 0, 0)),
            pl.BlockSpec((gb,) + r_t.shape[1:], per_slab),
            pl.BlockSpec((gb, LANES), lambda g, o: (g, 0)),
            pl.BlockSpec((gb, LANES), lambda g, o: (g, 0)),
        ],
        out_specs=pl.BlockSpec((None, nc, SUBLANES, LANES), lambda g, o: (g, 0, o, 0)),
        out_shape=jax.ShapeDtypeStruct((n_slab, nc, T, LANES), F32),
        scratch_shapes=[
            pltpu.VMEM((gb, nc, K), BF16),
            pltpu.VMEM((n_oct, nc, SUBLANES * LANES), BF16),
            pltpu.VMEM((gb, K, ncols), BF16),
            pltpu.VMEM((gb, K, K), BF16),
            pltpu.VMEM((gb, K, ncols), BF16),
        ] + [pltpu.VMEM((nc * gb, LANES), F32)] * 6,
        compiler_params=_cparams(("arbitrary", "arbitrary")),
        name="s5_chunked",
    )(ub, pmat, pmat.T, pw_t, mat_t, r_t, lre, lim)
    return y.reshape(n_slab, Ttok, LANES)


def _ssm_tables(a_re, a_im, log_dt, b_re, b_im, c_re, c_im):
    T = SSM_CHUNK
    G, P = a_re.shape[1], a_re.shape[2]
    C = b_re.shape[3]
    dt = jnp.exp(log_dt)[..., None]
    zr = a_re * dt
    zi = a_im * dt
    tau = jnp.arange(T + 1, dtype=F32)[:, None]
    mag = jnp.exp(zr[:, :, None] * tau)
    pr = mag * jnp.cos(zi[:, :, None] * tau)
    pi = mag * jnp.sin(zi[:, :, None] * tau)
    ab_re, ab_im = pr[:, :, 1], pi[:, :, 1]
    den = a_re * a_re + a_im * a_im
    nr = ab_re - 1.0
    f_re = ((nr * a_re + ab_im * a_im) / den)[:, :, None, :]
    f_im = ((ab_im * a_re - nr * a_im) / den)[:, :, None, :]
    bt_re = b_re.transpose(0, 1, 3, 2)
    bt_im = b_im.transpose(0, 1, 3, 2)
    bbt_re = f_re * bt_re - f_im * bt_im
    bbt_im = f_re * bt_im + f_im * bt_re

    fb = lambda f, b: jnp.concatenate([f, b], axis=-1)
    pw_t = jnp.stack([fb(pr[0, :, T - 1::-1], pr[1, :, :T]), fb(pi[0, :, T - 1::-1], pi[1, :, :T]),
                      fb(pr[0, :, 1:], pr[1, :, :0:-1]), fb(pi[0, :, 1:], pi[1, :, :0:-1])], axis=1)
    mat_t = jnp.stack([fb(bbt_re[0], bbt_re[1]), fb(bbt_im[0], bbt_im[1]),
                       fb(c_re[0], c_re[1]), fb(c_im[0], c_im[1])], axis=1)

    p0r = pr[:, :, :T, None, :]
    p0i = pi[:, :, :T, None, :]
    cp_re = c_re[:, :, None] * p0r - c_im[:, :, None] * p0i
    cp_im = c_re[:, :, None] * p0i + c_im[:, :, None] * p0r
    cb = jnp.einsum('dgtcq,dgkq->dgtck', jnp.concatenate([cp_re, -cp_im], axis=-1),
                    jnp.concatenate([bbt_re, bbt_im], axis=-1), precision=lax.Precision.HIGH)

    kern = jnp.concatenate([cb[1, :, T - 1:0:-1], (cb[0, :, 0] + cb[1, :, 0])[:, None], cb[0, :, 1:],
                            jnp.zeros_like(cb[0, :, :1])], axis=1)
    r_t = kern.transpose(0, 3, 1, 2).reshape(G, C, 2 * T * C)

    lre = jnp.concatenate([pr[0, :, T], pr[1, :, T]], axis=1)
    lim = jnp.concatenate([pi[0, :, T], pi[1, :, T]], axis=1)
    return pw_t, mat_t, r_t, lre, lim


def _layer_norm(r, g, b):
    mu = jnp.mean(r, axis=-1, keepdims=True)
    d = r - mu
    var = jnp.mean(d * d, axis=-1, keepdims=True)
    return d * lax.rsqrt(var + LN_EPS) * g + b


def _mix_kernel(x_ref, a_ref, y_ref, u_ref, d_ref, wglu_ref, wout_ref, g_ref, b_ref, o_ref, ob_ref,
                *, alpha):
    n_heads, tm, hd = a_ref.shape
    wa = n_heads * hd
    for half in range(2):
        rows = pl.ds(half * (tm // 2), tm // 2)
        y = jnp.concatenate([y_ref[sl, rows, :] for sl in range(y_ref.shape[0])], axis=1)
        yy = y + d_ref[...] * u_ref[rows, :]
        z = jax.nn.gelu(yy)
        gate = jax.nn.sigmoid(jnp.dot(z.astype(BF16), wglu_ref[...], preferred_element_type=F32))
        s = (z * gate).astype(BF16)
        a = jnp.concatenate([a_ref[h, rows, :] for h in range(n_heads)], axis=1)
        mixed = jnp.dot(a, wout_ref[pl.ds(0, wa), :], preferred_element_type=F32)
        mixed = mixed + jnp.dot(s, wout_ref[pl.ds(wa, s.shape[1]), :], preferred_element_type=F32)
        r = alpha * x_ref[rows, :] + mixed
        o = _layer_norm(r, g_ref[...], b_ref[...])
        o_ref[rows, :] = o
        ob_ref[rows, :] = o.astype(BF16)


def _mix(x2d, attn, y, u, d, wglu, wout, g, b, alpha, tm=512):
    T, D = x2d.shape
    W = u.shape[1]
    row = lambda i: (i, 0)
    fixed = lambda i: (0, 0)
    once = pl.Buffered(1)
    return pl.pallas_call(
        functools.partial(_mix_kernel, alpha=alpha),
        grid=(T // tm,),
        in_specs=[
            pl.BlockSpec((tm, D), row),
            pl.BlockSpec((attn.shape[0], tm, attn.shape[2]), lambda i: (0, i, 0)),
            pl.BlockSpec((y.shape[0], tm, y.shape[2]), lambda i: (0, i, 0)),
            pl.BlockSpec((tm, W), row),
            pl.BlockSpec((1, W), fixed),
            pl.BlockSpec(wglu.shape, fixed, pipeline_mode=once),
            pl.BlockSpec(wout.shape, fixed, pipeline_mode=once),
            pl.BlockSpec((1, D), fixed),
            pl.BlockSpec((1, D), fixed),
        ],
        out_specs=[pl.BlockSpec((tm, D), row), pl.BlockSpec((tm, D), row)],
        out_shape=[jax.ShapeDtypeStruct((T, D), F32), jax.ShapeDtypeStruct((T, D), BF16)],
        compiler_params=_cparams(("arbitrary",)),
        name="glu_outproj_ln",
    )(x2d, attn, y, u, d, wglu, wout, g, b)


def _ffn_up_kernel(x_ref, wg_ref, wu_ref, h_ref):
    x = x_ref[...]
    g = jnp.dot(x, wg_ref[...], preferred_element_type=F32)
    up = jnp.dot(x, wu_ref[...], preferred_element_type=F32)
    h_ref[...] = (jax.nn.silu(g) * up).astype(h_ref.dtype)


def _ffn_up(xb, wg, wu, tm=1024, tf=512):
    T, D = xb.shape
    F = wg.shape[1]
    return pl.pallas_call(
        _ffn_up_kernel,
        grid=(T // tm, F // tf),
        in_specs=[
            pl.BlockSpec((tm, D), lambda i, j: (i, 0)),
            pl.BlockSpec((D, tf), lambda i, j: (0, j)),
            pl.BlockSpec((D, tf), lambda i, j: (0, j)),
        ],
        out_specs=pl.BlockSpec((tm, tf), lambda i, j: (i, j)),
        out_shape=jax.ShapeDtypeStruct((T, F), BF16),
        compiler_params=_cparams(("arbitrary", "arbitrary")),
        name="ffn_up",
    )(xb, wg, wu)


def _ffn_down_kernel(h_ref, wd_ref, x_ref, g_ref, b_ref, o_ref, *, alpha):
    down = jnp.dot(h_ref[...], wd_ref[...], preferred_element_type=F32)
    o_ref[...] = _layer_norm(alpha * x_ref[...] + down, g_ref[...], b_ref[...])


def _ffn_down(h, wd, x1, g, b, alpha, tm=256):
    T, F = h.shape
    D = wd.shape[1]
    return pl.pallas_call(
        functools.partial(_ffn_down_kernel, alpha=alpha),
        grid=(T // tm,),
        in_specs=[
            pl.BlockSpec((tm, F), lambda i: (i, 0)),
            pl.BlockSpec((F, D), lambda i: (0, 0), pipeline_mode=pl.Buffered(1)),
            pl.BlockSpec((tm, D), lambda i: (i, 0)),
            pl.BlockSpec((1, D), lambda i: (0, 0)),
            pl.BlockSpec((1, D), lambda i: (0, 0)),
        ],
        out_specs=pl.BlockSpec((tm, D), lambda i: (i, 0)),
        out_shape=jax.ShapeDtypeStruct((T, D), F32),
        compiler_params=_cparams(("arbitrary",)),
        name="ffn_down_ln",
    )(h, wd, x1, g, b)


def _rope_tables(L):
    half = ROT_DIM // 2
    inv = ROPE_THETA ** (-jnp.arange(0, ROT_DIM, 2, dtype=F32) / ROT_DIM)
    ang = jnp.arange(L, dtype=F32)[:, None] * inv[None, :]
    cos, sin = jnp.cos(ang), jnp.sin(ang)
    lane = jnp.arange(LANES)
    d = lane % DIFF_HEAD_DIM
    freq = jnp.arange(half)[:, None]
    first = ((d < half)[None, :] & (d[None, :] == freq)).astype(F32)
    second = (((d >= half) & (d < ROT_DIM))[None, :] & (d[None, :] - half == freq)).astype(F32)
    spread = functools.partial(jnp.dot, precision=lax.Precision.HIGHEST)
    c = spread(cos, first + second) + (d >= ROT_DIM).astype(F32)[None, :]
    sa = -spread(sin, first)
    sb = spread(sin, second)
    tab = jnp.stack([c, sa, sb])
    return jnp.stack([tab * (DIFF_HEAD_DIM ** -0.5 * math.log2(math.e)), tab])


def _prepare_layer(p):
    (w_in, lq1, lk1, lq2, lk2, subln_w, a_re, a_im, log_dt, b_re, b_im, c_re, c_im, ssm_d, w_glu,
     w_out, ln1_g, ln1_b, w_gate, w_up, w_down, ln2_g, ln2_b) = p
    W = w_glu.shape[0]
    D = w_in.shape[0]
    row = lambda t, n: t.reshape(1, n).astype(F32)
    return dict(
        w_qku=jnp.concatenate([w_in[:, :2 * W], w_in[:, 3 * W:]], axis=1).astype(BF16),
        wvt=w_in[:, 2 * W:3 * W].T.astype(BF16),
        lam_vecs=jnp.stack([lq1, lk1, lq2, lk2]).astype(F32),
        sub_col=subln_w.astype(F32).reshape(-1, 1),
        ssm=_ssm_tables(a_re, a_im, log_dt, b_re, b_im, c_re, c_im),
        ssm_d=row(ssm_d, W), w_glu=w_glu.astype(BF16), w_out=w_out.astype(BF16),
        ln1_g=row(ln1_g, D), ln1_b=row(ln1_b, D),
        w_gate=w_gate.astype(BF16), w_up=w_up.astype(BF16), w_down=w_down.astype(BF16),
        ln2_g=row(ln2_g, D), ln2_b=row(ln2_b, D),
    )


def _encoder_layer(x, layer_idx, depth, pp):
    B, L, D = x.shape
    T = B * L
    W = pp["w_glu"].shape[0]
    alpha = (2 * depth) ** 0.25
    lam_init = 0.8 - 0.6 * math.exp(-0.3 * layer_idx)

    x2d = x.reshape(T, D)
    q, k, vt, u, ub = _inproj(x2d.astype(BF16), pp["w_qku"], pp["wvt"], _rope_tables(L), L)
    attn = _attention(q, k, vt, pp["lam_vecs"], pp["sub_col"], lam_init, L)
    y = _ssm(ub, *pp["ssm"], L)
    x1, x1b = _mix(x2d, attn, y, u, pp["ssm_d"], pp["w_glu"], pp["w_out"], pp["ln1_g"], pp["ln1_b"], alpha)
    h = _ffn_up(x1b, pp["w_gate"], pp["w_up"])
    out = _ffn_down(h, pp["w_down"], x1, pp["ln2_g"], pp["ln2_b"], alpha)
    return out.reshape(B, L, D)


def kernel(x_prompt, x_sample, w_in, lambda_q1, lambda_k1, lambda_q2, lambda_k2, subln_w, ssm_a_re, ssm_a_im, ssm_log_dt, ssm_b_re, ssm_b_im, ssm_c_re, ssm_c_im, ssm_d, w_glu, w_out, ln1_g, ln1_b, w_gate, w_up, w_down, ln2_g, ln2_b):
    params = (w_in, lambda_q1, lambda_k1, lambda_q2, lambda_k2, subln_w,
              ssm_a_re, ssm_a_im, ssm_log_dt, ssm_b_re, ssm_b_im, ssm_c_re, ssm_c_im, ssm_d, w_glu,
              w_out, ln1_g, ln1_b, w_gate, w_up, w_down, ln2_g, ln2_b)
    depth = w_in.shape[0]
    y_prompt, y_sample = x_prompt, x_sample
    for layer in range(depth):
        lp = _prepare_layer(tuple(t[layer] for t in params))
        y_prompt = _encoder_layer(y_prompt, layer, depth, lp)
        y_sample = _encoder_layer(y_sample, layer, depth, lp)
    return (y_prompt, y_sample)
```

```python
import functools
import math

import jax
import jax.numpy as jnp
from jax import lax
from jax.experimental import pallas as pl
from jax.experimental.pallas import tpu as pltpu

F32 = jnp.float32
BF16 = jnp.bfloat16

DIFF_HEAD_DIM = 64
ROT_DIM = DIFF_HEAD_DIM // 4
ROPE_THETA = 500000.0
SSM_GROUP = 16
SSM_STATE = 64
LN_EPS = 1e-5

LANES = 128
SUBLANES = 8
VMEM_LIMIT_BYTES = 56 * 1024 * 1024

SSM_CHUNK = 32
SSM_GBLK = SUBLANES
ATT_BK = 512
ATT_BQ = 512
ATT_UNROLL = 8
ATT_UNROLL_SHORT = 8
ATT_ONES_ROWS = 16
NEG_BIG = -1e30


def _cparams(sem):
    return pltpu.CompilerParams(dimension_semantics=sem, vmem_limit_bytes=VMEM_LIMIT_BYTES)


def _inproj_kernel(xb_ref, w_ref, wvt_ref, rope_ref, perm_ref, q_ref, k_ref, vt_ref, u_ref, ub_ref):
    j = pl.program_id(1)

    def rope_to(out_ref):
        acc = jnp.dot(xb_ref[...], w_ref[...], preferred_element_type=F32)
        c = rope_ref[0, 0]
        sa = rope_ref[0, 1]
        sb = rope_ref[0, 2]
        for cb in range(acc.shape[1] // LANES):
            xs = acc[:, cb * LANES:(cb + 1) * LANES]
            nxt = pltpu.roll(xs, LANES - ROT_DIM // 2, axis=1)
            prv = pltpu.roll(xs, ROT_DIM // 2, axis=1)
            out_ref[cb] = (xs * c + nxt * sa + prv * sb).astype(out_ref.dtype)

    @pl.when(j == 0)
    def _():
        rope_to(q_ref)

    @pl.when(j == 1)
    def _():
        rope_to(k_ref)

    @pl.when(j == 2)
    def _():
        vt = lax.dot_general(wvt_ref[...], xb_ref[...], (((1,), (1,)), ((), ())),
                             preferred_element_type=F32)
        hd = 2 * DIFF_HEAD_DIM
        ones = jnp.ones((vt_ref.shape[2] - hd, ATT_BK), vt_ref.dtype)
        for cc in range(vt_ref.shape[0]):
            for h in range(vt_ref.shape[1]):
                vt_ref[cc, h, pl.ds(0, hd), :] = (
                    vt[h * hd:(h + 1) * hd, cc * ATT_BK:(cc + 1) * ATT_BK].astype(vt_ref.dtype))
                vt_ref[cc, h, pl.ds(hd, ones.shape[0]), :] = ones

    @pl.when(j == 3)
    def _():
        u = jnp.dot(xb_ref[...], w_ref[...], preferred_element_type=F32)
        u_ref[...] = u
        ut = jnp.dot(perm_ref[...], u.astype(BF16), preferred_element_type=F32).astype(ub_ref.dtype)
        T = SSM_CHUNK
        nch = ub_ref.shape[0]
        for slab in range(u.shape[1] // LANES):
            for t in range(T):
                ub_ref[:, pl.ds((slab * T + t) * LANES, LANES)] = (
                    ut[t * nch:(t + 1) * nch, slab * LANES:(slab + 1) * LANES])


def _inproj(xb2d, w_qku, wvt, rope, seq_len, tm=1024):
    T, D = xb2d.shape
    W = wvt.shape[0]
    hd = 2 * DIFF_HEAD_DIM
    H = W // hd
    hda = hd + ATT_ONES_ROWS
    nseq_blk = seq_len // tm
    grid = (T // tm, 4)
    nch = tm // SSM_CHUNK
    r = jnp.arange(tm)
    perm = ((r % nch) * SSM_CHUNK + r // nch)[:, None] == r[None, :]
    perm = perm.astype(BF16)
    wmap = lambda i, j: (0, jnp.where(j >= 2, j - 1, j))
    return pl.pallas_call(
        _inproj_kernel,
        grid=grid,
        in_specs=[
            pl.BlockSpec((tm, D), lambda i, j: (i, 0)),
            pl.BlockSpec((D, W), wmap),
            pl.BlockSpec((W, D), lambda i, j: (0, 0), pipeline_mode=pl.Buffered(1)),
            pl.BlockSpec((1, 3, tm, LANES), lambda i, j: (jnp.minimum(j, 1), 0, i % nseq_blk, 0)),
            pl.BlockSpec((tm, tm), lambda i, j: (0, 0), pipeline_mode=pl.Buffered(1)),
        ],
        out_specs=[
            pl.BlockSpec((H, tm, hd), lambda i, j: (0, i, 0)),
            pl.BlockSpec((H, tm, hd), lambda i, j: (0, i, 0)),
            pl.BlockSpec((tm // ATT_BK, H, hda, ATT_BK), lambda i, j: (i, 0, 0, 0)),
            pl.BlockSpec((tm, W), lambda i, j: (i, 0)),
            pl.BlockSpec((nch, SSM_CHUNK * W), lambda i, j: (i, 0)),
        ],
        out_shape=[
            jax.ShapeDtypeStruct((H, T, hd), BF16),
            jax.ShapeDtypeStruct((H, T, hd), BF16),
            jax.ShapeDtypeStruct((T // ATT_BK, H, hda, ATT_BK), BF16),
            jax.ShapeDtypeStruct((T, W), F32),
            jax.ShapeDtypeStruct((T // SSM_CHUNK, SSM_CHUNK * W), BF16),
        ],
        compiler_params=_cparams(("arbitrary", "arbitrary")),
        name="inproj",
    )(xb2d, w_qku, wvt, rope, perm)


def _attn_kernel(lam_ref, sub_ref, q_ref, k_ref, vt_ref, o_ref, q1t_ref, q2t_ref, o1_ref, o2_ref, s_ref,
                 *, lam_init, bq, unroll):
    nk = vt_ref.shape[0]
    L, hd = q_ref.shape
    nq = L // bq

    lv = lam_ref[...]
    d1 = jnp.sum(lv[0:1] * lv[1:2], axis=1, keepdims=True)
    d2 = jnp.sum(lv[2:3] * lv[3:4], axis=1, keepdims=True)
    lam = jnp.exp(d1) - jnp.exp(d2) + lam_init

    def transpose_q(qi, carry):
        qt = q_ref[pl.ds(pl.multiple_of(qi * bq, bq), bq), :].astype(F32).T
        row = lax.broadcasted_iota(jnp.int32, qt.shape, 0)
        q1t_ref[qi] = jnp.where(row < DIFF_HEAD_DIM, qt, 0.0).astype(BF16)
        q2t_ref[qi] = jnp.where(row >= DIFF_HEAD_DIM, qt, 0.0).astype(BF16)
        return carry

    lax.fori_loop(0, nq, transpose_q, 0)
    o1_ref[...] = jnp.zeros_like(o1_ref)
    o2_ref[...] = jnp.zeros_like(o2_ref)

    def scores(qi, c, slot):
        kb = k_ref[pl.ds(pl.multiple_of(c * ATT_BK, ATT_BK), ATT_BK), :]
        tops = []
        for br, qt_ref in enumerate((q1t_ref, q2t_ref)):
            s = jnp.dot(kb, qt_ref[qi], preferred_element_type=F32)
            s_ref[slot, br] = s
            tops.append(jnp.max(s, axis=0, keepdims=True))
        return tuple(tops)

    def update(c, slot, tops, ms):
        vtb = vt_ref[c]
        new_ms = []
        for br, o_acc in enumerate((o1_ref, o2_ref)):
            m_new = jnp.maximum(ms[br], tops[br])
            alpha = jnp.exp2(ms[br] - m_new)
            p = jnp.exp2(s_ref[slot, br] - m_new).astype(BF16)
            pv = jnp.dot(vtb, p, preferred_element_type=F32)
            o_acc[...] = alpha * o_acc[...] + pv
            new_ms.append(m_new)
        return tuple(new_ms)

    def finish(qi):
        l1 = o1_ref[pl.ds(hd, 1), :]
        l2 = o2_ref[pl.ds(hd, 1), :]
        o = o1_ref[pl.ds(0, hd), :] / l1 - lam * (o2_ref[pl.ds(0, hd), :] / l2)
        msq = jnp.mean(o * o, axis=0, keepdims=True)
        o = o * lax.rsqrt(msq + LN_EPS) * sub_ref[...]
        o = o * (1.0 - lam_init)
        o_ref[pl.ds(pl.multiple_of(qi * bq, bq), bq), :] = o.T.astype(o_ref.dtype)
        o1_ref[...] = jnp.zeros_like(o1_ref)
        o2_ref[...] = jnp.zeros_like(o2_ref)

    neg = jnp.full((1, bq), NEG_BIG, F32)

    def body_long(t, carry):
        ms, tops = carry
        qi = t // trips_per_q
        c0 = unroll * (t % trips_per_q)
        last = c0 + unroll == nk
        for j in range(unroll):
            if j + 1 < unroll:
                nxt_tops = scores(qi, c0 + j + 1, (j + 1) % 2)
            else:
                nxt_q = jnp.where(last, jnp.minimum(qi + 1, nq - 1), qi)
                nxt_c = jnp.where(last, 0, c0 + unroll)
                nxt_tops = scores(nxt_q, nxt_c, 0)
            ms = update(c0 + j, j % 2, tops, ms)
            tops = nxt_tops

        @pl.when(last)
        def _():
            finish(qi)

        ms = tuple(jnp.where(last, neg, m) for m in ms)
        return ms, tops

    def body_short(t, carry):
        ms, tops = carry
        q0 = t * (unroll // nk)
        for j in range(unroll):
            qi, c = q0 + j // nk, j % nk
            if j + 1 < unroll:
                nxt_tops = scores(q0 + (j + 1) // nk, (j + 1) % nk, (j + 1) % 2)
            else:
                nxt_tops = scores(jnp.minimum(q0 + unroll // nk, nq - 1), 0, 0)
            ms = update(c, j % 2, tops, ms)
            tops = nxt_tops
            if c == nk - 1:
                finish(qi)
                ms = (neg, neg)
        return ms, tops

    tops0 = scores(0, 0, 0)
    if unroll <= nk:
        trips_per_q = nk // unroll
        lax.fori_loop(0, nq * trips_per_q, body_long, ((neg, neg), tops0))
    else:
        lax.fori_loop(0, nq * nk // unroll, body_short, ((neg, neg), tops0))


def _attention(q, k, vt, lam_vecs, sub_col, lam_init, seq_len):
    H, T, hd = q.shape
    L = seq_len
    B = T // L
    hda = vt.shape[2]
    nk = L // ATT_BK
    bq = min(ATT_BQ, L)
    nq = L // bq
    if nk >= ATT_UNROLL:
        unroll = ATT_UNROLL
        assert nk % unroll == 0
    else:
        unroll = nk * min(nq, ATT_UNROLL_SHORT // nk)
        assert unroll >= nk and (nq * nk) % unroll == 0
    assert unroll % 2 == 0
    seq = lambda b, h: (h, b, 0)
    return pl.pallas_call(
        functools.partial(_attn_kernel, lam_init=lam_init, bq=bq, unroll=unroll),
        grid=(B, H),
        in_specs=[
            pl.BlockSpec(lam_vecs.shape, lambda b, h: (0, 0)),
            pl.BlockSpec(sub_col.shape, lambda b, h: (0, 0)),
            pl.BlockSpec((None, L, hd), seq),
            pl.BlockSpec((None, L, hd), seq),
            pl.BlockSpec((nk, None, hda, ATT_BK), lambda b, h: (b, h, 0, 0)),
        ],
        out_specs=pl.BlockSpec((None, L, hd), seq),
        out_shape=jax.ShapeDtypeStruct((H, T, hd), BF16),
        scratch_shapes=[
            pltpu.VMEM((nq, hd, bq), BF16),
            pltpu.VMEM((nq, hd, bq), BF16),
            pltpu.VMEM((hda, bq), F32),
            pltpu.VMEM((hda, bq), F32),
            pltpu.VMEM((2, 2, ATT_BK, bq), F32),
        ],
        compiler_params=_cparams(("arbitrary", "arbitrary")),
        name="diff_attn",
    )(lam_vecs, sub_col, q, k, vt)


def _ssm_kernel(u_ref, p_ref, pt_ref, pw_ref, mat_ref, r_ref, lre_ref, lim_ref, y_ref,
                ug_ref, yg_ref, w_ref, m_ref, v_ref, sre, sim, hfre, hfim, hbre, hbim, *, cps):
    o = pl.program_id(1)

    @pl.when(o == 0)
    def _():
        _ssm_compute(u_ref, p_ref, pw_ref, mat_ref, r_ref, lre_ref, lim_ref,
                     ug_ref, yg_ref, w_ref, m_ref, v_ref, sre, sim, hfre, hfim, hbre, hbim, cps)

    res = jnp.dot(yg_ref[o], pt_ref[...], preferred_element_type=F32)
    for t8 in range(SUBLANES):
        y_ref[:, t8, :] = res[:, t8 * LANES:(t8 + 1) * LANES]


def _ssm_compute(u_ref, p_ref, pw_ref, mat_ref, r_ref, lre_ref, lim_ref,
                 ug_ref, yg_ref, w_ref, m_ref, v_ref, sre, sim, hfre, hfim, hbre, hbim, cps):
    gb = ug_ref.shape[0]
    nc = ug_ref.shape[1]
    half = LANES // 2
    T = SSM_CHUNK
    n_oct = T // SUBLANES
    oct_lanes = SUBLANES * LANES

    for oc in range(n_oct):
        z = u_ref[:, pl.ds(oc * oct_lanes, oct_lanes)]
        zg = jnp.dot(z, p_ref[...], preferred_element_type=F32).astype(BF16)
        for gi in range(gb):
            ug_ref[gi, :, pl.ds(oc * LANES, LANES)] = zg[:, gi * LANES:(gi + 1) * LANES]

    C = SSM_GROUP
    for gi in range(gb):
        r = r_ref[gi]
        b_re, b_im, c_re, c_im = (mat_ref[gi, i] for i in range(4))
        for s_ in range(T):
            rows = pl.ds(s_ * C, C)
            off = (T - 1 - s_) * C
            m_ref[gi, rows, :] = r[:, off:off + T * C].astype(BF16)
            wr, wi, vr, vi = (pw_ref[gi, i, pl.ds(s_, 1), :] for i in range(4))
            w_ref[gi, rows, pl.ds(0, LANES)] = (wr * b_re - wi * b_im).astype(BF16)
            w_ref[gi, rows, pl.ds(LANES, LANES)] = (wr * b_im + wi * b_re).astype(BF16)
            v_ref[gi, rows, pl.ds(0, LANES)] = (vr * c_re - vi * c_im).astype(BF16)
            v_ref[gi, rows, pl.ds(LANES, LANES)] = (-(vr * c_im + vi * c_re)).astype(BF16)

    for gi in range(gb):
        s = jnp.dot(ug_ref[gi], w_ref[gi], preferred_element_type=F32)
        sre[pl.ds(gi, nc, stride=gb), :] = s[:, :LANES]
        sim[pl.ds(gi, nc, stride=gb), :] = s[:, LANES:]

    zero = jnp.zeros((gb, LANES), F32)
    lre = lre_ref[...]
    lim = lim_ref[...]
    fwd_lane = lax.broadcasted_iota(jnp.int32, (gb, LANES), 1) < half

    def body(j, carry):
        hr, hi = carry
        fresh = (j % cps) == 0
        hr = jnp.where(fresh, 0.0, hr)
        hi = jnp.where(fresh, 0.0, hi)
        jf = pl.multiple_of(j * gb, gb)
        jb = pl.multiple_of((nc - 1 - j) * gb, gb)
        hfre[pl.ds(jf, gb), :] = hr
        hfim[pl.ds(jf, gb), :] = hi
        hbre[pl.ds(jb, gb), :] = hr
        hbim[pl.ds(jb, gb), :] = hi
        sr = jnp.where(fwd_lane, sre[pl.ds(jf, gb), :], sre[pl.ds(jb, gb), :])
        si = jnp.where(fwd_lane, sim[pl.ds(jf, gb), :], sim[pl.ds(jb, gb), :])
        return lre * hr - lim * hi + sr, lre * hi + lim * hr + si

    lax.fori_loop(0, nc, body, (zero, zero), unroll=8)

    fwd_col = lax.broadcasted_iota(jnp.int32, (nc, LANES), 1) < half
    for gi in range(gb):
        hr = jnp.where(fwd_col, hfre[pl.ds(gi, nc, stride=gb), :], hbre[pl.ds(gi, nc, stride=gb), :])
        hi = jnp.where(fwd_col, hfim[pl.ds(gi, nc, stride=gb), :], hbim[pl.ds(gi, nc, stride=gb), :])
        h = jnp.concatenate([hr, hi], axis=1).astype(BF16)
        y = jnp.dot(ug_ref[gi], m_ref[gi], preferred_element_type=F32)
        y = y + lax.dot_general(h, v_ref[gi], (((1,), (1,)), ((), ())),
                                preferred_element_type=F32)
        for oc in range(n_oct):
            yg_ref[oc, :, pl.ds(gi * LANES, LANES)] = y[:, oc * LANES:(oc + 1) * LANES].astype(yg_ref.dtype)


def _lane_regroup_matrix():
    n = SUBLANES * LANES
    i = jnp.arange(n)
    t8, gl, c = i // LANES, (i % LANES) // SSM_GROUP, i % SSM_GROUP
    dest = gl * LANES + t8 * SSM_GROUP + c
    return (dest[:, None] == i[None, :]).astype(BF16)


def _ssm(ub, pw_t, mat_t, r_t, lre, lim, seq_len):
    T = SSM_CHUNK
    nc = ub.shape[0]
    Wd = ub.shape[1] // T
    Ttok = nc * T
    cps = seq_len // T
    gb = SSM_GBLK
    assert gb * SSM_GROUP == LANES and T % SUBLANES == 0 and nc % cps == 0
    n_slab = Wd // LANES
    n_oct = T // SUBLANES
    K = T * SSM_GROUP
    ncols = 2 * pw_t.shape[3]
    pmat = _lane_regroup_matrix()
    once = pl.Buffered(1)
    fixed2 = lambda g, o: (0, 0)
    per_slab = lambda g, o: (g, 0, 0)

    y = pl.pallas_call(
        functools.partial(_ssm_kernel, cps=cps),
        grid=(n_slab, n_oct),
        in_specs=[
            pl.BlockSpec((nc, T * LANES), lambda g, o: (0, g)),
            pl.BlockSpec(pmat.shape, fixed2, pipeline_mode=once),
            pl.BlockSpec(pmat.shape, fixed2, pipeline_mode=once),
            pl.BlockSpec((gb,) + pw_t.shape[1:], lambda g, o: (g, 0, 0, 0)),
            pl.BlockSpec((gb,) + mat_t.shape[1:], lambda g, o: (g, 0, 0, 0)),
            pl.BlockSpec((gb,) + r_t.shape[1:], per_slab),
            pl.BlockSpec((gb, LANES), lambda g, o: (g, 0)),
            pl.BlockSpec((gb, LANES), lambda g, o: (g, 0)),
        ],
        out_specs=pl.BlockSpec((None, nc, SUBLANES, LANES), lambda g, o: (g, 0, o, 0)),
        out_shape=jax.ShapeDtypeStruct((n_slab, nc, T, LANES), F32),
        scratch_shapes=[
            pltpu.VMEM((gb, nc, K), BF16),
            pltpu.VMEM((n_oct, nc, SUBLANES * LANES), BF16),
            pltpu.VMEM((gb, K, ncols), BF16),
            pltpu.VMEM((gb, K, K), BF16),
            pltpu.VMEM((gb, K, ncols), BF16),
        ] + [pltpu.VMEM((nc * gb, LANES), F32)] * 6,
        compiler_params=_cparams(("arbitrary", "arbitrary")),
        name="s5_chunked",
    )(ub, pmat, pmat.T, pw_t, mat_t, r_t, lre, lim)
    return y.reshape(n_slab, Ttok, LANES)


def _ssm_tables(a_re, a_im, log_dt, b_re, b_im, c_re, c_im):
    T = SSM_CHUNK
    G, P = a_re.shape[1], a_re.shape[2]
    C = b_re.shape[3]
    dt = jnp.exp(log_dt)[..., None]
    zr = a_re * dt
    zi = a_im * dt
    tau = jnp.arange(T + 1, dtype=F32)[:, None]
    mag = jnp.exp(zr[:, :, None] * tau)
    pr = mag * jnp.cos(zi[:, :, None] * tau)
    pi = mag * jnp.sin(zi[:, :, None] * tau)
    ab_re, ab_im = pr[:, :, 1], pi[:, :, 1]
    den = a_re * a_re + a_im * a_im
    nr = ab_re - 1.0
    f_re = ((nr * a_re + ab_im * a_im) / den)[:, :, None, :]
    f_im = ((ab_im * a_re - nr * a_im) / den)[:, :, None, :]
    bt_re = b_re.transpose(0, 1, 3, 2)
    bt_im = b_im.transpose(0, 1, 3, 2)
    bbt_re = f_re * bt_re - f_im * bt_im
    bbt_im = f_re * bt_im + f_im * bt_re

    fb = lambda f, b: jnp.concatenate([f, b], axis=-1)
    pw_t = jnp.stack([fb(pr[0, :, T - 1::-1], pr[1, :, :T]), fb(pi[0, :, T - 1::-1], pi[1, :, :T]),
                      fb(pr[0, :, 1:], pr[1, :, :0:-1]), fb(pi[0, :, 1:], pi[1, :, :0:-1])], axis=1)
    mat_t = jnp.stack([fb(bbt_re[0], bbt_re[1]), fb(bbt_im[0], bbt_im[1]),
                       fb(c_re[0], c_re[1]), fb(c_im[0], c_im[1])], axis=1)

    p0r = pr[:, :, :T, None, :]
    p0i = pi[:, :, :T, None, :]
    cp_re = c_re[:, :, None] * p0r - c_im[:, :, None] * p0i
    cp_im = c_re[:, :, None] * p0i + c_im[:, :, None] * p0r
    cb = jnp.einsum('dgtcq,dgkq->dgtck', jnp.concatenate([cp_re, -cp_im], axis=-1),
                    jnp.concatenate([bbt_re, bbt_im], axis=-1), precision=lax.Precision.HIGH)

    kern = jnp.concatenate([cb[1, :, T - 1:0:-1], (cb[0, :, 0] + cb[1, :, 0])[:, None], cb[0, :, 1:],
                            jnp.zeros_like(cb[0, :, :1])], axis=1)
    r_t = kern.transpose(0, 3, 1, 2).reshape(G, C, 2 * T * C)

    lre = jnp.concatenate([pr[0, :, T], pr[1, :, T]], axis=1)
    lim = jnp.concatenate([pi[0, :, T], pi[1, :, T]], axis=1)
    return pw_t, mat_t, r_t, lre, lim


def _layer_norm(r, g, b):
    mu = jnp.mean(r, axis=-1, keepdims=True)
    d = r - mu
    var = jnp.mean(d * d, axis=-1, keepdims=True)
    return d * lax.rsqrt(var + LN_EPS) * g + b


def _mix_kernel(x_ref, a_ref, y_ref, u_ref, d_ref, wglu_ref, wout_ref, g_ref, b_ref, o_ref, ob_ref,
                *, alpha):
    n_heads, tm, hd = a_ref.shape
    wa = n_heads * hd
    for half in range(2):
        rows = pl.ds(half * (tm // 2), tm // 2)
        y = jnp.concatenate([y_ref[sl, rows, :] for sl in range(y_ref.shape[0])], axis=1)
        yy = y + d_ref[...] * u_ref[rows, :]
        z = jax.nn.gelu(yy)
        gate = jax.nn.sigmoid(jnp.dot(z.astype(BF16), wglu_ref[...], preferred_element_type=F32))
        s = (z * gate).astype(BF16)
        a = jnp.concatenate([a_ref[h, rows, :] for h in range(n_heads)], axis=1)
        mixed = jnp.dot(a, wout_ref[pl.ds(0, wa), :], preferred_element_type=F32)
        mixed = mixed + jnp.dot(s, wout_ref[pl.ds(wa, s.shape[1]), :], preferred_element_type=F32)
        r = alpha * x_ref[rows, :] + mixed
        o = _layer_norm(r, g_ref[...], b_ref[...])
        o_ref[rows, :] = o
        ob_ref[rows, :] = o.astype(BF16)


def _mix(x2d, attn, y, u, d, wglu, wout, g, b, alpha, tm=512):
    T, D = x2d.shape
    W = u.shape[1]
    row = lambda i: (i, 0)
    fixed = lambda i: (0, 0)
    once = pl.Buffered(1)
    return pl.pallas_call(
        functools.partial(_mix_kernel, alpha=alpha),
        grid=(T // tm,),
        in_specs=[
            pl.BlockSpec((tm, D), row),
            pl.BlockSpec((attn.shape[0], tm, attn.shape[2]), lambda i: (0, i, 0)),
            pl.BlockSpec((y.shape[0], tm, y.shape[2]), lambda i: (0, i, 0)),
            pl.BlockSpec((tm, W), row),
            pl.BlockSpec((1, W), fixed),
            pl.BlockSpec(wglu.shape, fixed, pipeline_mode=once),
            pl.BlockSpec(wout.shape, fixed, pipeline_mode=once),
            pl.BlockSpec((1, D), fixed),
            pl.BlockSpec((1, D), fixed),
        ],
        out_specs=[pl.BlockSpec((tm, D), row), pl.BlockSpec((tm, D), row)],
        out_shape=[jax.ShapeDtypeStruct((T, D), F32), jax.ShapeDtypeStruct((T, D), BF16)],
        compiler_params=_cparams(("arbitrary",)),
        name="glu_outproj_ln",
    )(x2d, attn, y, u, d, wglu, wout, g, b)


def _ffn_up_kernel(x_ref, wg_ref, wu_ref, h_ref):
    x = x_ref[...]
    g = jnp.dot(x, wg_ref[...], preferred_element_type=F32)
    up = jnp.dot(x, wu_ref[...], preferred_element_type=F32)
    h_ref[...] = (jax.nn.silu(g) * up).astype(h_ref.dtype)


def _ffn_up(xb, wg, wu, tm=1024, tf=512):
    T, D = xb.shape
    F = wg.shape[1]
    return pl.pallas_call(
        _ffn_up_kernel,
        grid=(T // tm, F // tf),
        in_specs=[
            pl.BlockSpec((tm, D), lambda i, j: (i, 0)),
            pl.BlockSpec((D, tf), lambda i, j: (0, j)),
            pl.BlockSpec((D, tf), lambda i, j: (0, j)),
        ],
        out_specs=pl.BlockSpec((tm, tf), lambda i, j: (i, j)),
        out_shape=jax.ShapeDtypeStruct((T, F), BF16),
        compiler_params=_cparams(("arbitrary", "arbitrary")),
        name="ffn_up",
    )(xb, wg, wu)


def _ffn_down_kernel(h_ref, wd_ref, x_ref, g_ref, b_ref, o_ref, *, alpha):
    down = jnp.dot(h_ref[...], wd_ref[...], preferred_element_type=F32)
    o_ref[...] = _layer_norm(alpha * x_ref[...] + down, g_ref[...], b_ref[...])


def _ffn_down(h, wd, x1, g, b, alpha, tm=256):
    T, F = h.shape
    D = wd.shape[1]
    return pl.pallas_call(
        functools.partial(_ffn_down_kernel, alpha=alpha),
        grid=(T // tm,),
        in_specs=[
            pl.BlockSpec((tm, F), lambda i: (i, 0)),
            pl.BlockSpec((F, D), lambda i: (0, 0), pipeline_mode=pl.Buffered(1)),
            pl.BlockSpec((tm, D), lambda i: (i, 0)),
            pl.BlockSpec((1, D), lambda i: (0, 0)),
            pl.BlockSpec((1, D), lambda i: (0, 0)),
        ],
        out_specs=pl.BlockSpec((tm, D), lambda i: (i, 0)),
        out_shape=jax.ShapeDtypeStruct((T, D), F32),
        compiler_params=_cparams(("arbitrary",)),
        name="ffn_down_ln",
    )(h, wd, x1, g, b)


def _rope_tables(L):
    half = ROT_DIM // 2
    inv = ROPE_THETA ** (-jnp.arange(0, ROT_DIM, 2, dtype=F32) / ROT_DIM)
    ang = jnp.arange(L, dtype=F32)[:, None] * inv[None, :]
    cos, sin = jnp.cos(ang), jnp.sin(ang)
    lane = jnp.arange(LANES)
    d = lane % DIFF_HEAD_DIM
    freq = jnp.arange(half)[:, None]
    first = ((d < half)[None, :] & (d[None, :] == freq)).astype(F32)
    second = (((d >= half) & (d < ROT_DIM))[None, :] & (d[None, :] - half == freq)).astype(F32)
    spread = functools.partial(jnp.dot, precision=lax.Precision.HIGHEST)
    c = spread(cos, first + second) + (d >= ROT_DIM).astype(F32)[None, :]
    sa = -spread(sin, first)
    sb = spread(sin, second)
    tab = jnp.stack([c, sa, sb])
    return jnp.stack([tab * (DIFF_HEAD_DIM ** -0.5 * math.log2(math.e)), tab])


def _prepare_layer(p):
    (w_in, lq1, lk1, lq2, lk2, subln_w, a_re, a_im, log_dt, b_re, b_im, c_re, c_im, ssm_d, w_glu,
     w_out, ln1_g, ln1_b, w_gate, w_up, w_down, ln2_g, ln2_b) = p
    W = w_glu.shape[0]
    D = w_in.shape[0]
    row = lambda t, n: t.reshape(1, n).astype(F32)
    return dict(
        w_qku=jnp.concatenate([w_in[:, :2 * W], w_in[:, 3 * W:]], axis=1).astype(BF16),
        wvt=w_in[:, 2 * W:3 * W].T.astype(BF16),
        lam_vecs=jnp.stack([lq1, lk1, lq2, lk2]).astype(F32),
        sub_col=subln_w.astype(F32).reshape(-1, 1),
        ssm=_ssm_tables(a_re, a_im, log_dt, b_re, b_im, c_re, c_im),
        ssm_d=row(ssm_d, W), w_glu=w_glu.astype(BF16), w_out=w_out.astype(BF16),
        ln1_g=row(ln1_g, D), ln1_b=row(ln1_b, D),
        w_gate=w_gate.astype(BF16), w_up=w_up.astype(BF16), w_down=w_down.astype(BF16),
        ln2_g=row(ln2_g, D), ln2_b=row(ln2_b, D),
    )


def _encoder_layer(x, layer_idx, depth, pp):
    B, L, D = x.shape
    T = B * L
    W = pp["w_glu"].shape[0]
    alpha = (2 * depth) ** 0.25
    lam_init = 0.8 - 0.6 * math.exp(-0.3 * layer_idx)

    x2d = x.reshape(T, D)
    q, k, vt, u, ub = _inproj(x2d.astype(BF16), pp["w_qku"], pp["wvt"], _rope_tables(L), L)
    attn = _attention(q, k, vt, pp["lam_vecs"], pp["sub_col"], lam_init, L)
    y = _ssm(ub, *pp["ssm"], L)
    x1, x1b = _mix(x2d, attn, y, u, pp["ssm_d"], pp["w_glu"], pp["w_out"], pp["ln1_g"], pp["ln1_b"], alpha)
    h = _ffn_up(x1b, pp["w_gate"], pp["w_up"])
    out = _ffn_down(h, pp["w_down"], x1, pp["ln2_g"], pp["ln2_b"], alpha)
    return out.reshape(B, L, D)


def kernel(x_prompt, x_sample, w_in, lambda_q1, lambda_k1, lambda_q2, lambda_k2, subln_w, ssm_a_re, ssm_a_im, ssm_log_dt, ssm_b_re, ssm_b_im, ssm_c_re, ssm_c_im, ssm_d, w_glu, w_out, ln1_g, ln1_b, w_gate, w_up, w_down, ln2_g, ln2_b):
    params = (w_in, lambda_q1, lambda_k1, lambda_q2, lambda_k2, subln_w,
              ssm_a_re, ssm_a_im, ssm_log_dt, ssm_b_re, ssm_b_im, ssm_c_re, ssm_c_im, ssm_d, w_glu,
              w_out, ln1_g, ln1_b, w_gate, w_up, w_down, ln2_g, ln2_b)
    depth = w_in.shape[0]
    y_prompt, y_sample = x_prompt, x_sample
    for layer in range(depth):
        lp = _prepare_layer(tuple(t[layer] for t in params))
        y_prompt = _encoder_layer(y_prompt, layer, depth, lp)
        y_sample = _encoder_layer(y_sample, layer, depth, lp)
    return (y_prompt, y_sample)
```

```python
import functools
import math

import jax
import jax.numpy as jnp
from jax import lax
from jax.experimental import pallas as pl
from jax.experimental.pallas import tpu as pltpu

F32 = jnp.float32
BF16 = jnp.bfloat16

DIFF_HEAD_DIM = 64
ROT_DIM = DIFF_HEAD_DIM // 4
ROPE_THETA = 500000.0
SSM_GROUP = 16
SSM_STATE = 64
LN_EPS = 1e-5

LANES = 128
SUBLANES = 8
VMEM_LIMIT_BYTES = 56 * 1024 * 1024

SSM_CHUNK = 32
SSM_GBLK = SUBLANES
ATT_BK = 512
ATT_BQ = 512
ATT_UNROLL = 8
ATT_UNROLL_SHORT = 8
ATT_ONES_ROWS = 16
NEG_BIG = -1e30


def _cparams(sem):
    return pltpu.CompilerParams(dimension_semantics=sem, vmem_limit_bytes=VMEM_LIMIT_BYTES)


def _inproj_kernel(xb_ref, w_ref, wvt_ref, rope_ref, perm_ref, q_ref, k_ref, vt_ref, u_ref, ub_ref):
    j = pl.program_id(1)

    def rope_to(out_ref):
        acc = jnp.dot(xb_ref[...], w_ref[...], preferred_element_type=F32)
        c = rope_ref[0, 0]
        sa = rope_ref[0, 1]
        sb = rope_ref[0, 2]
        for cb in range(acc.shape[1] // LANES):
            xs = acc[:, cb * LANES:(cb + 1) * LANES]
            nxt = pltpu.roll(xs, LANES - ROT_DIM // 2, axis=1)
            prv = pltpu.roll(xs, ROT_DIM // 2, axis=1)
            out_ref[cb] = (xs * c + nxt * sa + prv * sb).astype(out_ref.dtype)

    @pl.when(j == 0)
    def _():
        rope_to(q_ref)

    @pl.when(j == 1)
    def _():
        rope_to(k_ref)

    @pl.when(j == 2)
    def _():
        vt = lax.dot_general(wvt_ref[...], xb_ref[...], (((1,), (1,)), ((), ())),
                             preferred_element_type=F32)
        hd = 2 * DIFF_HEAD_DIM
        ones = jnp.ones((vt_ref.shape[2] - hd, ATT_BK), vt_ref.dtype)
        for cc in range(vt_ref.shape[0]):
            for h in range(vt_ref.shape[1]):
                vt_ref[cc, h, pl.ds(0, hd), :] = (
                    vt[h * hd:(h + 1) * hd, cc * ATT_BK:(cc + 1) * ATT_BK].astype(vt_ref.dtype))
                vt_ref[cc, h, pl.ds(hd, ones.shape[0]), :] = ones

    @pl.when(j == 3)
    def _():
        u = jnp.dot(xb_ref[...], w_ref[...], preferred_element_type=F32)
        u_ref[...] = u
        ut = jnp.dot(perm_ref[...], u.astype(BF16), preferred_element_type=F32).astype(ub_ref.dtype)
        T = SSM_CHUNK
        nch = ub_ref.shape[0]
        for slab in range(u.shape[1] // LANES):
            for t in range(T):
                ub_ref[:, pl.ds((slab * T + t) * LANES, LANES)] = (
                    ut[t * nch:(t + 1) * nch, slab * LANES:(slab + 1) * LANES])


def _inproj(xb2d, w_qku, wvt, rope, seq_len, tm=1024):
    T, D = xb2d.shape
    W = wvt.shape[0]
    hd = 2 * DIFF_HEAD_DIM
    H = W // hd
    hda = hd + ATT_ONES_ROWS
    nseq_blk = seq_len // tm
    grid = (T // tm, 4)
    nch = tm // SSM_CHUNK
    r = jnp.arange(tm)
    perm = ((r % nch) * SSM_CHUNK + r // nch)[:, None] == r[None, :]
    perm = perm.astype(BF16)
    wmap = lambda i, j: (0, jnp.where(j >= 2, j - 1, j))
    return pl.pallas_call(
        _inproj_kernel,
        grid=grid,
        in_specs=[
            pl.BlockSpec((tm, D), lambda i, j: (i, 0)),
            pl.BlockSpec((D, W), wmap),
            pl.BlockSpec((W, D), lambda i, j: (0, 0), pipeline_mode=pl.Buffered(1)),
            pl.BlockSpec((1, 3, tm, LANES), lambda i, j: (jnp.minimum(j, 1), 0, i % nseq_blk, 0)),
            pl.BlockSpec((tm, tm), lambda i, j: (0, 0), pipeline_mode=pl.Buffered(1)),
        ],
        out_specs=[
            pl.BlockSpec((H, tm, hd), lambda i, j: (0, i, 0)),
            pl.BlockSpec((H, tm, hd), lambda i, j: (0, i, 0)),
            pl.BlockSpec((tm // ATT_BK, H, hda, ATT_BK), lambda i, j: (i, 0, 0, 0)),
            pl.BlockSpec((tm, W), lambda i, j: (i, 0)),
            pl.BlockSpec((nch, SSM_CHUNK * W), lambda i, j: (i, 0)),
        ],
        out_shape=[
            jax.ShapeDtypeStruct((H, T, hd), BF16),
            jax.ShapeDtypeStruct((H, T, hd), BF16),
            jax.ShapeDtypeStruct((T // ATT_BK, H, hda, ATT_BK), BF16),
            jax.ShapeDtypeStruct((T, W), F32),
            jax.ShapeDtypeStruct((T // SSM_CHUNK, SSM_CHUNK * W), BF16),
        ],
        compiler_params=_cparams(("arbitrary", "arbitrary")),
        name="inproj",
    )(xb2d, w_qku, wvt, rope, perm)


def _attn_kernel(lam_ref, sub_ref, q_ref, k_ref, vt_ref, o_ref, q1t_ref, q2t_ref, o1_ref, o2_ref, s_ref,
                 *, lam_init, bq, unroll):
    nk = vt_ref.shape[0]
    L, hd = q_ref.shape
    nq = L // bq

    lv = lam_ref[...]
    d1 = jnp.sum(lv[0:1] * lv[1:2], axis=1, keepdims=True)
    d2 = jnp.sum(lv[2:3] * lv[3:4], axis=1, keepdims=True)
    lam = jnp.exp(d1) - jnp.exp(d2) + lam_init

    def transpose_q(qi, carry):
        qt = q_ref[pl.ds(pl.multiple_of(qi * bq, bq), bq), :].astype(F32).T
        row = lax.broadcasted_iota(jnp.int32, qt.shape, 0)
        q1t_ref[qi] = jnp.where(row < DIFF_HEAD_DIM, qt, 0.0).astype(BF16)
        q2t_ref[qi] = jnp.where(row >= DIFF_HEAD_DIM, qt, 0.0).astype(BF16)
        return carry

    lax.fori_loop(0, nq, transpose_q, 0)
    o1_ref[...] = jnp.zeros_like(o1_ref)
    o2_ref[...] = jnp.zeros_like(o2_ref)

    def scores(qi, c, slot):
        kb = k_ref[pl.ds(pl.multiple_of(c * ATT_BK, ATT_BK), ATT_BK), :]
        tops = []
        for br, qt_ref in enumerate((q1t_ref, q2t_ref)):
            s = jnp.dot(kb, qt_ref[qi], preferred_element_type=F32)
            s_ref[slot, br] = s
            tops.append(jnp.max(s, axis=0, keepdims=True))
        return tuple(tops)

    def update(c, slot, tops, ms):
        vtb = vt_ref[c]
        new_ms = []
        for br, o_acc in enumerate((o1_ref, o2_ref)):
            m_new = jnp.maximum(ms[br], tops[br])
            alpha = jnp.exp2(ms[br] - m_new)
            p = jnp.exp2(s_ref[slot, br] - m_new).astype(BF16)
            pv = jnp.dot(vtb, p, preferred_element_type=F32)
            o_acc[...] = alpha * o_acc[...] + pv
            new_ms.append(m_new)
        return tuple(new_ms)

    def finish(qi):
        l1 = o1_ref[pl.ds(hd, 1), :]
        l2 = o2_ref[pl.ds(hd, 1), :]
        o = o1_ref[pl.ds(0, hd), :] / l1 - lam * (o2_ref[pl.ds(0, hd), :] / l2)
        msq = jnp.mean(o * o, axis=0, keepdims=True)
        o = o * lax.rsqrt(msq + LN_EPS) * sub_ref[...]
        o = o * (1.0 - lam_init)
        o_ref[pl.ds(pl.multiple_of(qi * bq, bq), bq), :] = o.T.astype(o_ref.dtype)
        o1_ref[...] = jnp.zeros_like(o1_ref)
        o2_ref[...] = jnp.zeros_like(o2_ref)

    neg = jnp.full((1, bq), NEG_BIG, F32)

    def body_long(t, carry):
        ms, tops = carry
        qi = t // trips_per_q
        c0 = unroll * (t % trips_per_q)
        last = c0 + unroll == nk
        for j in range(unroll):
            if j + 1 < unroll:
                nxt_tops = scores(qi, c0 + j + 1, (j + 1) % 2)
            else:
                nxt_q = jnp.where(last, jnp.minimum(qi + 1, nq - 1), qi)
                nxt_c = jnp.where(last, 0, c0 + unroll)
                nxt_tops = scores(nxt_q, nxt_c, 0)
            ms = update(c0 + j, j % 2, tops, ms)
            tops = nxt_tops

        @pl.when(last)
        def _():
            finish(qi)

        ms = tuple(jnp.where(last, neg, m) for m in ms)
        return ms, tops

    def body_short(t, carry):
        ms, tops = carry
        q0 = t * (unroll // nk)
        for j in range(unroll):
            qi, c = q0 + j // nk, j % nk
            if j + 1 < unroll:
                nxt_tops = scores(q0 + (j + 1) // nk, (j + 1) % nk, (j + 1) % 2)
            else:
                nxt_tops = scores(jnp.minimum(q0 + unroll // nk, nq - 1), 0, 0)
            ms = update(c, j % 2, tops, ms)
            tops = nxt_tops
            if c == nk - 1:
                finish(qi)
                ms = (neg, neg)
        return ms, tops

    tops0 = scores(0, 0, 0)
    if unroll <= nk:
        trips_per_q = nk // unroll
        lax.fori_loop(0, nq * trips_per_q, body_long, ((neg, neg), tops0))
    else:
        lax.fori_loop(0, nq * nk // unroll, body_short, ((neg, neg), tops0))


def _attention(q, k, vt, lam_vecs, sub_col, lam_init, seq_len):
    H, T, hd = q.shape
    L = seq_len
    B = T // L
    hda = vt.shape[2]
    nk = L // ATT_BK
    bq = min(ATT_BQ, L)
    nq = L // bq
    if nk >= ATT_UNROLL:
        unroll = ATT_UNROLL
        assert nk % unroll == 0
    else:
        unroll = nk * min(nq, ATT_UNROLL_SHORT // nk)
        assert unroll >= nk and (nq * nk) % unroll == 0
    assert unroll % 2 == 0
    seq = lambda b, h: (h, b, 0)
    return pl.pallas_call(
        functools.partial(_attn_kernel, lam_init=lam_init, bq=bq, unroll=unroll),
        grid=(B, H),
        in_specs=[
            pl.BlockSpec(lam_vecs.shape, lambda b, h: (0, 0)),
            pl.BlockSpec(sub_col.shape, lambda b, h: (0, 0)),
            pl.BlockSpec((None, L, hd), seq),
            pl.BlockSpec((None, L, hd), seq),
            pl.BlockSpec((nk, None, hda, ATT_BK), lambda b, h: (b, h, 0, 0)),
        ],
        out_specs=pl.BlockSpec((None, L, hd), seq),
        out_shape=jax.ShapeDtypeStruct((H, T, hd), BF16),
        scratch_shapes=[
            pltpu.VMEM((nq, hd, bq), BF16),
            pltpu.VMEM((nq, hd, bq), BF16),
            pltpu.VMEM((hda, bq), F32),
            pltpu.VMEM((hda, bq), F32),
            pltpu.VMEM((2, 2, ATT_BK, bq), F32),
        ],
        compiler_params=_cparams(("arbitrary", "arbitrary")),
        name="diff_attn",
    )(lam_vecs, sub_col, q, k, vt)


def _ssm_kernel(u_ref, p_ref, pt_ref, pw_ref, mat_ref, r_ref, lre_ref, lim_ref, y_ref,
                ug_ref, yg_ref, w_ref, m_ref, v_ref, sre, sim, hfre, hfim, hbre, hbim, *, cps):
    _ssm_compute(u_ref, p_ref, pw_ref, mat_ref, r_ref, lre_ref, lim_ref,
                 ug_ref, yg_ref, w_ref, m_ref, v_ref, sre, sim, hfre, hfim, hbre, hbim, cps)

    for oc in range(yg_ref.shape[0]):
        res = jnp.dot(yg_ref[oc], pt_ref[...], preferred_element_type=F32)
        for t8 in range(SUBLANES):
            y_ref[:, oc * SUBLANES + t8, :] = res[:, t8 * LANES:(t8 + 1) * LANES]


def _ssm_compute(u_ref, p_ref, pw_ref, mat_ref, r_ref, lre_ref, lim_ref,
                 ug_ref, yg_ref, w_ref, m_ref, v_ref, sre, sim, hfre, hfim, hbre, hbim, cps):
    gb = ug_ref.shape[0]
    nc = ug_ref.shape[1]
    half = LANES // 2
    T = SSM_CHUNK
    n_oct = T // SUBLANES
    oct_lanes = SUBLANES * LANES

    for oc in range(n_oct):
        z = u_ref[:, pl.ds(oc * oct_lanes, oct_lanes)]
        zg = jnp.dot(z, p_ref[...], preferred_element_type=F32).astype(BF16)
        for gi in range(gb):
            ug_ref[gi, :, pl.ds(oc * LANES, LANES)] = zg[:, gi * LANES:(gi + 1) * LANES]

    C = SSM_GROUP
    for gi in range(gb):
        r = r_ref[gi]
        b_re, b_im, c_re, c_im = (mat_ref[gi, i] for i in range(4))
        for s_ in range(T):
            rows = pl.ds(s_ * C, C)
            off = (T - 1 - s_) * C
            m_ref[gi, rows, :] = r[:, off:off + T * C].astype(BF16)
            wr, wi, vr, vi = (pw_ref[gi, i, pl.ds(s_, 1), :] for i in range(4))
            w_ref[gi, rows, pl.ds(0, LANES)] = (wr * b_re - wi * b_im).astype(BF16)
            w_ref[gi, rows, pl.ds(LANES, LANES)] = (wr * b_im + wi * b_re).astype(BF16)
            v_ref[gi, rows, pl.ds(0, LANES)] = (vr * c_re - vi * c_im).astype(BF16)
            v_ref[gi, rows, pl.ds(LANES, LANES)] = (-(vr * c_im + vi * c_re)).astype(BF16)

    for gi in range(gb):
        s = jnp.dot(ug_ref[gi], w_ref[gi], preferred_element_type=F32)
        sre[pl.ds(gi, nc, stride=gb), :] = s[:, :LANES]
        sim[pl.ds(gi, nc, stride=gb), :] = s[:, LANES:]

    zero = jnp.zeros((gb, LANES), F32)
    lre = lre_ref[...]
    lim = lim_ref[...]
    fwd_lane = lax.broadcasted_iota(jnp.int32, (gb, LANES), 1) < half

    def body(j, carry):
        hr, hi = carry
        fresh = (j % cps) == 0
        hr = jnp.where(fresh, 0.0, hr)
        hi = jnp.where(fresh, 0.0, hi)
        jf = pl.multiple_of(j * gb, gb)
        jb = pl.multiple_of((nc - 1 - j) * gb, gb)
        hfre[pl.ds(jf, gb), :] = hr
        hfim[pl.ds(jf, gb), :] = hi
        hbre[pl.ds(jb, gb), :] = hr
        hbim[pl.ds(jb, gb), :] = hi
        sr = jnp.where(fwd_lane, sre[pl.ds(jf, gb), :], sre[pl.ds(jb, gb), :])
        si = jnp.where(fwd_lane, sim[pl.ds(jf, gb), :], sim[pl.ds(jb, gb), :])
        return lre * hr - lim * hi + sr, lre * hi + lim * hr + si

    lax.fori_loop(0, nc, body, (zero, zero), unroll=8)

    fwd_col = lax.broadcasted_iota(jnp.int32, (nc, LANES), 1) < half
    for gi in range(gb):
        hr = jnp.where(fwd_col, hfre[pl.ds(gi, nc, stride=gb), :], hbre[pl.ds(gi, nc, stride=gb), :])
        hi = jnp.where(fwd_col, hfim[pl.ds(gi, nc, stride=gb), :], hbim[pl.ds(gi, nc, stride=gb), :])
        h = jnp.concatenate([hr, hi], axis=1).astype(BF16)
        y = jnp.dot(ug_ref[gi], m_ref[gi], preferred_element_type=F32)
        y = y + lax.dot_general(h, v_ref[gi], (((1,), (1,)), ((), ())),
                                preferred_element_type=F32)
        for oc in range(n_oct):
            yg_ref[oc, :, pl.ds(gi * LANES, LANES)] = y[:, oc * LANES:(oc + 1) * LANES].astype(yg_ref.dtype)


def _lane_regroup_matrix():
    n = SUBLANES * LANES
    i = jnp.arange(n)
    t8, gl, c = i // LANES, (i % LANES) // SSM_GROUP, i % SSM_GROUP
    dest = gl * LANES + t8 * SSM_GROUP + c
    return (dest[:, None] == i[None, :]).astype(BF16)


def _ssm(ub, pw_t, mat_t, r_t, lre, lim, seq_len):
    T = SSM_CHUNK
    nc = ub.shape[0]
    Wd = ub.shape[1] // T
    Ttok = nc * T
    cps = seq_len // T
    gb = SSM_GBLK
    assert gb * SSM_GROUP == LANES and T % SUBLANES == 0 and nc % cps == 0
    n_slab = Wd // LANES
    n_oct = T // SUBLANES
    K = T * SSM_GROUP
    ncols = 2 * pw_t.shape[3]
    pmat = _lane_regroup_matrix()
    once = pl.Buffered(1)
    fixed2 = lambda g: (0, 0)
    per_slab = lambda g: (g, 0, 0)

    y = pl.pallas_call(
        functools.partial(_ssm_kernel, cps=cps),
        grid=(n_slab,),
        in_specs=[
            pl.BlockSpec((nc, T * LANES), lambda g: (0, g)),
            pl.BlockSpec(pmat.shape, fixed2, pipeline_mode=once),
            pl.BlockSpec(pmat.shape, fixed2, pipeline_mode=once),
            pl.BlockSpec((gb,) + pw_t.shape[1:], lambda g: (g, 0, 0, 0)),
            pl.BlockSpec((gb,) + mat_t.shape[1:], lambda g: (g, 0, 0, 0)),
            pl.BlockSpec((gb,) + r_t.shape[1:], per_slab),
            pl.BlockSpec((gb, LANES), lambda g: (g, 0)),
            pl.BlockSpec((gb, LANES), lambda g: (g, 0)),
        ],
        out_specs=pl.BlockSpec((None, nc, T, LANES), lambda g: (g, 0, 0, 0), pipeline_mode=once),
        out_shape=jax.ShapeDtypeStruct((n_slab, nc, T, LANES), F32),
        scratch_shapes=[
            pltpu.VMEM((gb, nc, K), BF16),
            pltpu.VMEM((n_oct, nc, SUBLANES * LANES), BF16),
            pltpu.VMEM((gb, K, ncols), BF16),
            pltpu.VMEM((gb, K, K), BF16),
            pltpu.VMEM((gb, K, ncols), BF16),
        ] + [pltpu.VMEM((nc * gb, LANES), F32)] * 6,
        compiler_params=_cparams(("arbitrary",)),
        name="s5_chunked",
    )(ub, pmat, pmat.T, pw_t, mat_t, r_t, lre, lim)
    return y.reshape(n_slab, Ttok, LANES)


def _ssm_tables(a_re, a_im, log_dt, b_re, b_im, c_re, c_im):
    T = SSM_CHUNK
    G, P = a_re.shape[1], a_re.shape[2]
    C = b_re.shape[3]
    dt = jnp.exp(log_dt)[..., None]
    zr = a_re * dt
    zi = a_im * dt
    tau = jnp.arange(T + 1, dtype=F32)[:, None]
    mag = jnp.exp(zr[:, :, None] * tau)
    pr = mag * jnp.cos(zi[:, :, None] * tau)
    pi = mag * jnp.sin(zi[:, :, None] * tau)
    ab_re, ab_im = pr[:, :, 1], pi[:, :, 1]
    den = a_re * a_re + a_im * a_im
    nr = ab_re - 1.0
    f_re = ((nr * a_re + ab_im * a_im) / den)[:, :, None, :]
    f_im = ((ab_im * a_re - nr * a_im) / den)[:, :, None, :]
    bt_re = b_re.transpose(0, 1, 3, 2)
    bt_im = b_im.transpose(0, 1, 3, 2)
    bbt_re = f_re * bt_re - f_im * bt_im
    bbt_im = f_re * bt_im + f_im * bt_re

    fb = lambda f, b: jnp.concatenate([f, b], axis=-1)
    pw_t = jnp.stack([fb(pr[0, :, T - 1::-1], pr[1, :, :T]), fb(pi[0, :, T - 1::-1], pi[1, :, :T]),
                      fb(pr[0, :, 1:], pr[1, :, :0:-1]), fb(pi[0, :, 1:], pi[1, :, :0:-1])], axis=1)
    mat_t = jnp.stack([fb(bbt_re[0], bbt_re[1]), fb(bbt_im[0], bbt_im[1]),
                       fb(c_re[0], c_re[1]), fb(c_im[0], c_im[1])], axis=1)

    p0r = pr[:, :, :T, None, :]
    p0i = pi[:, :, :T, None, :]
    cp_re = c_re[:, :, None] * p0r - c_im[:, :, None] * p0i
    cp_im = c_re[:, :, None] * p0i + c_im[:, :, None] * p0r
    cb = jnp.einsum('dgtcq,dgkq->dgtck', jnp.concatenate([cp_re, -cp_im], axis=-1),
                    jnp.concatenate([bbt_re, bbt_im], axis=-1), precision=lax.Precision.HIGH)

    kern = jnp.concatenate([cb[1, :, T - 1:0:-1], (cb[0, :, 0] + cb[1, :, 0])[:, None], cb[0, :, 1:],
                            jnp.zeros_like(cb[0, :, :1])], axis=1)
    r_t = kern.transpose(0, 3, 1, 2).reshape(G, C, 2 * T * C)

    lre = jnp.concatenate([pr[0, :, T], pr[1, :, T]], axis=1)
    lim = jnp.concatenate([pi[0, :, T], pi[1, :, T]], axis=1)
    return pw_t, mat_t, r_t, lre, lim


def _layer_norm(r, g, b):
    mu = jnp.mean(r, axis=-1, keepdims=True)
    d = r - mu
    var = jnp.mean(d * d, axis=-1, keepdims=True)
    return d * lax.rsqrt(var + LN_EPS) * g + b


def _mix_kernel(x_ref, a_ref, y_ref, u_ref, d_ref, wglu_ref, wout_ref, g_ref, b_ref, o_ref, ob_ref,
                *, alpha):
    n_heads, tm, hd = a_ref.shape
    wa = n_heads * hd
    for half in range(2):
        rows = pl.ds(half * (tm // 2), tm // 2)
        y = jnp.concatenate([y_ref[sl, rows, :] for sl in range(y_ref.shape[0])], axis=1)
        yy = y + d_ref[...] * u_ref[rows, :]
        z = jax.nn.gelu(yy)
        gate = jax.nn.sigmoid(jnp.dot(z.astype(BF16), wglu_ref[...], preferred_element_type=F32))
        s = (z * gate).astype(BF16)
        a = jnp.concatenate([a_ref[h, rows, :] for h in range(n_heads)], axis=1)
        mixed = jnp.dot(a, wout_ref[pl.ds(0, wa), :], preferred_element_type=F32)
        mixed = mixed + jnp.dot(s, wout_ref[pl.ds(wa, s.shape[1]), :], preferred_element_type=F32)
        r = alpha * x_ref[rows, :] + mixed
        o = _layer_norm(r, g_ref[...], b_ref[...])
        o_ref[rows, :] = o
        ob_ref[rows, :] = o.astype(BF16)


def _mix(x2d, attn, y, u, d, wglu, wout, g, b, alpha, tm=512):
    T, D = x2d.shape
    W = u.shape[1]
    row = lambda i: (i, 0)
    fixed = lambda i: (0, 0)
    once = pl.Buffered(1)
    return pl.pallas_call(
        functools.partial(_mix_kernel, alpha=alpha),
        grid=(T // tm,),
        in_specs=[
            pl.BlockSpec((tm, D), row),
            pl.BlockSpec((attn.shape[0], tm, attn.shape[2]), lambda i: (0, i, 0)),
            pl.BlockSpec((y.shape[0], tm, y.shape[2]), lambda i: (0, i, 0)),
            pl.BlockSpec((tm, W), row),
            pl.BlockSpec((1, W), fixed),
            pl.BlockSpec(wglu.shape, fixed, pipeline_mode=once),
            pl.BlockSpec(wout.shape, fixed, pipeline_mode=once),
            pl.BlockSpec((1, D), fixed),
            pl.BlockSpec((1, D), fixed),
        ],
        out_specs=[pl.BlockSpec((tm, D), row), pl.BlockSpec((tm, D), row)],
        out_shape=[jax.ShapeDtypeStruct((T, D), F32), jax.ShapeDtypeStruct((T, D), BF16)],
        compiler_params=_cparams(("arbitrary",)),
        name="glu_outproj_ln",
    )(x2d, attn, y, u, d, wglu, wout, g, b)


def _ffn_up_kernel(x_ref, wg_ref, wu_ref, h_ref):
    x = x_ref[...]
    g = jnp.dot(x, wg_ref[...], preferred_element_type=F32)
    up = jnp.dot(x, wu_ref[...], preferred_element_type=F32)
    h_ref[...] = (jax.nn.silu(g) * up).astype(h_ref.dtype)


def _ffn_up(xb, wg, wu, tm=1024, tf=512):
    T, D = xb.shape
    F = wg.shape[1]
    return pl.pallas_call(
        _ffn_up_kernel,
        grid=(T // tm, F // tf),
        in_specs=[
            pl.BlockSpec((tm, D), lambda i, j: (i, 0)),
            pl.BlockSpec((D, tf), lambda i, j: (0, j)),
            pl.BlockSpec((D, tf), lambda i, j: (0, j)),
        ],
        out_specs=pl.BlockSpec((tm, tf), lambda i, j: (i, j)),
        out_shape=jax.ShapeDtypeStruct((T, F), BF16),
        compiler_params=_cparams(("arbitrary", "arbitrary")),
        name="ffn_up",
    )(xb, wg, wu)


def _ffn_down_kernel(h_ref, wd_ref, x_ref, g_ref, b_ref, o_ref, *, alpha):
    down = jnp.dot(h_ref[...], wd_ref[...], preferred_element_type=F32)
    o_ref[...] = _layer_norm(alpha * x_ref[...] + down, g_ref[...], b_ref[...])


def _ffn_down(h, wd, x1, g, b, alpha, tm=256):
    T, F = h.shape
    D = wd.shape[1]
    return pl.pallas_call(
        functools.partial(_ffn_down_kernel, alpha=alpha),
        grid=(T // tm,),
        in_specs=[
            pl.BlockSpec((tm, F), lambda i: (i, 0)),
            pl.BlockSpec((F, D), lambda i: (0, 0), pipeline_mode=pl.Buffered(1)),
            pl.BlockSpec((tm, D), lambda i: (i, 0)),
            pl.BlockSpec((1, D), lambda i: (0, 0)),
            pl.BlockSpec((1, D), lambda i: (0, 0)),
        ],
        out_specs=pl.BlockSpec((tm, D), lambda i: (i, 0)),
        out_shape=jax.ShapeDtypeStruct((T, D), F32),
        compiler_params=_cparams(("arbitrary",)),
        name="ffn_down_ln",
    )(h, wd, x1, g, b)


def _rope_tables(L):
    half = ROT_DIM // 2
    inv = ROPE_THETA ** (-jnp.arange(0, ROT_DIM, 2, dtype=F32) / ROT_DIM)
    ang = jnp.arange(L, dtype=F32)[:, None] * inv[None, :]
    cos, sin = jnp.cos(ang), jnp.sin(ang)
    lane = jnp.arange(LANES)
    d = lane % DIFF_HEAD_DIM
    freq = jnp.arange(half)[:, None]
    first = ((d < half)[None, :] & (d[None, :] == freq)).astype(F32)
    second = (((d >= half) & (d < ROT_DIM))[None, :] & (d[None, :] - half == freq)).astype(F32)
    spread = functools.partial(jnp.dot, precision=lax.Precision.HIGHEST)
    c = spread(cos, first + second) + (d >= ROT_DIM).astype(F32)[None, :]
    sa = -spread(sin, first)
    sb = spread(sin, second)
    tab = jnp.stack([c, sa, sb])
    return jnp.stack([tab * (DIFF_HEAD_DIM ** -0.5 * math.log2(math.e)), tab])


def _prepare_layer(p):
    (w_in, lq1, lk1, lq2, lk2, subln_w, a_re, a_im, log_dt, b_re, b_im, c_re, c_im, ssm_d, w_glu,
     w_out, ln1_g, ln1_b, w_gate, w_up, w_down, ln2_g, ln2_b) = p
    W = w_glu.shape[0]
    D = w_in.shape[0]
    row = lambda t, n: t.reshape(1, n).astype(F32)
    return dict(
        w_qku=jnp.concatenate([w_in[:, :2 * W], w_in[:, 3 * W:]], axis=1).astype(BF16),
        wvt=w_in[:, 2 * W:3 * W].T.astype(BF16),
        lam_vecs=jnp.stack([lq1, lk1, lq2, lk2]).astype(F32),
        sub_col=subln_w.astype(F32).reshape(-1, 1),
        ssm=_ssm_tables(a_re, a_im, log_dt, b_re, b_im, c_re, c_im),
        ssm_d=row(ssm_d, W), w_glu=w_glu.astype(BF16), w_out=w_out.astype(BF16),
        ln1_g=row(ln1_g, D), ln1_b=row(ln1_b, D),
        w_gate=w_gate.astype(BF16), w_up=w_up.astype(BF16), w_down=w_down.astype(BF16),
        ln2_g=row(ln2_g, D), ln2_b=row(ln2_b, D),
    )


def _encoder_layer(x, layer_idx, depth, pp):
    B, L, D = x.shape
    T = B * L
    W = pp["w_glu"].shape[0]
    alpha = (2 * depth) ** 0.25
    lam_init = 0.8 - 0.6 * math.exp(-0.3 * layer_idx)

    x2d = x.reshape(T, D)
    q, k, vt, u, ub = _inproj(x2d.astype(BF16), pp["w_qku"], pp["wvt"], _rope_tables(L), L)
    attn = _attention(q, k, vt, pp["lam_vecs"], pp["sub_col"], lam_init, L)
    y = _ssm(ub, *pp["ssm"], L)
    x1, x1b = _mix(x2d, attn, y, u, pp["ssm_d"], pp["w_glu"], pp["w_out"], pp["ln1_g"], pp["ln1_b"], alpha)
    h = _ffn_up(x1b, pp["w_gate"], pp["w_up"])
    out = _ffn_down(h, pp["w_down"], x1, pp["ln2_g"], pp["ln2_b"], alpha)
    return out.reshape(B, L, D)


def kernel(x_prompt, x_sample, w_in, lambda_q1, lambda_k1, lambda_q2, lambda_k2, subln_w, ssm_a_re, ssm_a_im, ssm_log_dt, ssm_b_re, ssm_b_im, ssm_c_re, ssm_c_im, ssm_d, w_glu, w_out, ln1_g, ln1_b, w_gate, w_up, w_down, ln2_g, ln2_b):
    params = (w_in, lambda_q1, lambda_k1, lambda_q2, lambda_k2, subln_w,
              ssm_a_re, ssm_a_im, ssm_log_dt, ssm_b_re, ssm_b_im, ssm_c_re, ssm_c_im, ssm_d, w_glu,
              w_out, ln1_g, ln1_b, w_gate, w_up, w_down, ln2_g, ln2_b)
    depth = w_in.shape[0]
    y_prompt, y_sample = x_prompt, x_sample
    for layer in range(depth):
        lp = _prepare_layer(tuple(t[layer] for t in params))
        y_prompt = _encoder_layer(y_prompt, layer, depth, lp)
        y_sample = _encoder_layer(y_sample, layer, depth, lp)
    return (y_prompt, y_sample)
```

```python
import functools
import math

import jax
import jax.numpy as jnp
from jax import lax
from jax.experimental import pallas as pl
from jax.experimental.pallas import tpu as pltpu

F32 = jnp.float32
BF16 = jnp.bfloat16

DIFF_HEAD_DIM = 64
ROT_DIM = DIFF_HEAD_DIM // 4
ROPE_THETA = 500000.0
SSM_GROUP = 16
SSM_STATE = 64
LN_EPS = 1e-5

LANES = 128
SUBLANES = 8
VMEM_LIMIT_BYTES = 56 * 1024 * 1024

SSM_CHUNK = 32
SSM_GBLK = SUBLANES
ATT_BK = 512
ATT_BQ = 512
ATT_UNROLL = 8
ATT_UNROLL_SHORT = 8
ATT_ONES_ROWS = 16
NEG_BIG = -1e30


def _cparams(sem):
    return pltpu.CompilerParams(dimension_semantics=sem, vmem_limit_bytes=VMEM_LIMIT_BYTES)


def _inproj_kernel(xb_ref, w_ref, wvt_ref, rope_ref, perm_ref, q_ref, k_ref, vt_ref, u_ref, ub_ref):
    j = pl.program_id(1)

    def rope_to(out_ref, scale):
        acc = jnp.dot(xb_ref[...], w_ref[...], preferred_element_type=F32)
        c = rope_ref[0] * scale
        sa = rope_ref[1] * scale
        sb = rope_ref[2] * scale
        for cb in range(acc.shape[1] // LANES):
            xs = acc[:, cb * LANES:(cb + 1) * LANES]
            nxt = pltpu.roll(xs, LANES - ROT_DIM // 2, axis=1)
            prv = pltpu.roll(xs, ROT_DIM // 2, axis=1)
            out_ref[cb] = (xs * c + nxt * sa + prv * sb).astype(out_ref.dtype)

    @pl.when(j == 0)
    def _():
        rope_to(q_ref, DIFF_HEAD_DIM ** -0.5 * math.log2(math.e))

    @pl.when(j == 1)
    def _():
        rope_to(k_ref, 1.0)

    @pl.when(j == 2)
    def _():
        vt = lax.dot_general(wvt_ref[...], xb_ref[...], (((1,), (1,)), ((), ())),
                             preferred_element_type=F32)
        hd = 2 * DIFF_HEAD_DIM
        ones = jnp.ones((vt_ref.shape[2] - hd, ATT_BK), vt_ref.dtype)
        for cc in range(vt_ref.shape[0]):
            for h in range(vt_ref.shape[1]):
                vt_ref[cc, h, pl.ds(0, hd), :] = (
                    vt[h * hd:(h + 1) * hd, cc * ATT_BK:(cc + 1) * ATT_BK].astype(vt_ref.dtype))
                vt_ref[cc, h, pl.ds(hd, ones.shape[0]), :] = ones

    @pl.when(j == 3)
    def _():
        u = jnp.dot(xb_ref[...], w_ref[...], preferred_element_type=F32)
        u_ref[...] = u
        ut = jnp.dot(perm_ref[...], u.astype(BF16), preferred_element_type=F32).astype(ub_ref.dtype)
        T = SSM_CHUNK
        nch = ub_ref.shape[0]
        for slab in range(u.shape[1] // LANES):
            for t in range(T):
                ub_ref[:, pl.ds((slab * T + t) * LANES, LANES)] = (
                    ut[t * nch:(t + 1) * nch, slab * LANES:(slab + 1) * LANES])


def _inproj(xb2d, w_qku, wvt, rope, seq_len, tm=1024):
    T, D = xb2d.shape
    W = wvt.shape[0]
    hd = 2 * DIFF_HEAD_DIM
    H = W // hd
    hda = hd + ATT_ONES_ROWS
    nseq_blk = seq_len // tm
    grid = (T // tm, 4)
    nch = tm // SSM_CHUNK
    r = jnp.arange(tm)
    perm = ((r % nch) * SSM_CHUNK + r // nch)[:, None] == r[None, :]
    perm = perm.astype(BF16)
    wmap = lambda i, j: (0, jnp.where(j >= 2, j - 1, j))
    return pl.pallas_call(
        _inproj_kernel,
        grid=grid,
        in_specs=[
            pl.BlockSpec((tm, D), lambda i, j: (i, 0)),
            pl.BlockSpec((D, W), wmap),
            pl.BlockSpec((W, D), lambda i, j: (0, 0), pipeline_mode=pl.Buffered(1)),
            pl.BlockSpec((3, tm, LANES), lambda i, j: (0, i % nseq_blk, 0)),
            pl.BlockSpec((tm, tm), lambda i, j: (0, 0), pipeline_mode=pl.Buffered(1)),
        ],
        out_specs=[
            pl.BlockSpec((H, tm, hd), lambda i, j: (0, i, 0)),
            pl.BlockSpec((H, tm, hd), lambda i, j: (0, i, 0)),
            pl.BlockSpec((tm // ATT_BK, H, hda, ATT_BK), lambda i, j: (i, 0, 0, 0)),
            pl.BlockSpec((tm, W), lambda i, j: (i, 0)),
            pl.BlockSpec((nch, SSM_CHUNK * W), lambda i, j: (i, 0)),
        ],
        out_shape=[
            jax.ShapeDtypeStruct((H, T, hd), BF16),
            jax.ShapeDtypeStruct((H, T, hd), BF16),
            jax.ShapeDtypeStruct((T // ATT_BK, H, hda, ATT_BK), BF16),
            jax.ShapeDtypeStruct((T, W), F32),
            jax.ShapeDtypeStruct((T // SSM_CHUNK, SSM_CHUNK * W), BF16),
        ],
        compiler_params=_cparams(("arbitrary", "arbitrary")),
        name="inproj",
    )(xb2d, w_qku, wvt, rope, perm)


def _attn_kernel(lam_ref, sub_ref, q_ref, k_ref, vt_ref, o_ref, q1t_ref, q2t_ref, o1_ref, o2_ref, s_ref,
                 *, lam_init, bq, unroll):
    nk = vt_ref.shape[0]
    L, hd = q_ref.shape
    nq = L // bq

    lv = lam_ref[...]
    d1 = jnp.sum(lv[0:1] * lv[1:2], axis=1, keepdims=True)
    d2 = jnp.sum(lv[2:3] * lv[3:4], axis=1, keepdims=True)
    lam = jnp.exp(d1) - jnp.exp(d2) + lam_init

    def transpose_q(qi, carry):
        qt = q_ref[pl.ds(pl.multiple_of(qi * bq, bq), bq), :].astype(F32).T
        row = lax.broadcasted_iota(jnp.int32, qt.shape, 0)
        q1t_ref[qi] = jnp.where(row < DIFF_HEAD_DIM, qt, 0.0).astype(BF16)
        q2t_ref[qi] = jnp.where(row >= DIFF_HEAD_DIM, qt, 0.0).astype(BF16)
        return carry

    lax.fori_loop(0, nq, transpose_q, 0)
    o1_ref[...] = jnp.zeros_like(o1_ref)
    o2_ref[...] = jnp.zeros_like(o2_ref)

    def scores(qi, c, slot):
        kb = k_ref[pl.ds(pl.multiple_of(c * ATT_BK, ATT_BK), ATT_BK), :]
        tops = []
        for br, qt_ref in enumerate((q1t_ref, q2t_ref)):
            s = jnp.dot(kb, qt_ref[qi], preferred_element_type=F32)
            s_ref[slot, br] = s
            tops.append(jnp.max(s, axis=0, keepdims=True))
        return tuple(tops)

    def update(c, slot, tops, ms):
        vtb = vt_ref[c]
        new_ms = []
        for br, o_acc in enumerate((o1_ref, o2_ref)):
            m_new = jnp.maximum(ms[br], tops[br])
            alpha = jnp.exp2(ms[br] - m_new)
            p = jnp.exp2(s_ref[slot, br] - m_new).astype(BF16)
            pv = jnp.dot(vtb, p, preferred_element_type=F32)
            o_acc[...] = alpha * o_acc[...] + pv
            new_ms.append(m_new)
        return tuple(new_ms)

    def finish(qi):
        l1 = o1_ref[pl.ds(hd, 1), :]
        l2 = o2_ref[pl.ds(hd, 1), :]
        o = o1_ref[pl.ds(0, hd), :] / l1 - lam * (o2_ref[pl.ds(0, hd), :] / l2)
        msq = jnp.mean(o * o, axis=0, keepdims=True)
        o = o * lax.rsqrt(msq + LN_EPS) * sub_ref[...]
        o = o * (1.0 - lam_init)
        o_ref[pl.ds(pl.multiple_of(qi * bq, bq), bq), :] = o.T.astype(o_ref.dtype)
        o1_ref[...] = jnp.zeros_like(o1_ref)
        o2_ref[...] = jnp.zeros_like(o2_ref)

    neg = jnp.full((1, bq), NEG_BIG, F32)

    def body_long(t, carry):
        ms, tops = carry
        qi = t // trips_per_q
        c0 = unroll * (t % trips_per_q)
        last = c0 + unroll == nk
        for j in range(unroll):
            if j + 1 < unroll:
                nxt_tops = scores(qi, c0 + j + 1, (j + 1) % 2)
            else:
                nxt_q = jnp.where(last, jnp.minimum(qi + 1, nq - 1), qi)
                nxt_c = jnp.where(last, 0, c0 + unroll)
                nxt_tops = scores(nxt_q, nxt_c, 0)
            ms = update(c0 + j, j % 2, tops, ms)
            tops = nxt_tops

        @pl.when(last)
        def _():
            finish(qi)

        ms = tuple(jnp.where(last, neg, m) for m in ms)
        return ms, tops

    def body_short(t, carry):
        ms, tops = carry
        q0 = t * (unroll // nk)
        for j in range(unroll):
            qi, c = q0 + j // nk, j % nk
            if j + 1 < unroll:
                nxt_tops = scores(q0 + (j + 1) // nk, (j + 1) % nk, (j + 1) % 2)
            else:
                nxt_tops = scores(jnp.minimum(q0 + unroll // nk, nq - 1), 0, 0)
            ms = update(c, j % 2, tops, ms)
            tops = nxt_tops
            if c == nk - 1:
                finish(qi)
                ms = (neg, neg)
        return ms, tops

    tops0 = scores(0, 0, 0)
    if unroll <= nk:
        trips_per_q = nk // unroll
        lax.fori_loop(0, nq * trips_per_q, body_long, ((neg, neg), tops0))
    else:
        lax.fori_loop(0, nq * nk // unroll, body_short, ((neg, neg), tops0))


def _attention(q, k, vt, lam_vecs, sub_col, lam_init, seq_len):
    H, T, hd = q.shape
    L = seq_len
    B = T // L
    hda = vt.shape[2]
    nk = L // ATT_BK
    bq = min(ATT_BQ, L)
    nq = L // bq
    if nk >= ATT_UNROLL:
        unroll = ATT_UNROLL
        assert nk % unroll == 0
    else:
        unroll = nk * min(nq, ATT_UNROLL_SHORT // nk)
        assert unroll >= nk and (nq * nk) % unroll == 0
    assert unroll % 2 == 0
    seq = lambda b, h: (h, b, 0)
    return pl.pallas_call(
        functools.partial(_attn_kernel, lam_init=lam_init, bq=bq, unroll=unroll),
        grid=(B, H),
        in_specs=[
            pl.BlockSpec(lam_vecs.shape, lambda b, h: (0, 0)),
            pl.BlockSpec(sub_col.shape, lambda b, h: (0, 0)),
            pl.BlockSpec((None, L, hd), seq),
            pl.BlockSpec((None, L, hd), seq),
            pl.BlockSpec((nk, None, hda, ATT_BK), lambda b, h: (b, h, 0, 0)),
        ],
        out_specs=pl.BlockSpec((None, L, hd), seq),
        out_shape=jax.ShapeDtypeStruct((H, T, hd), BF16),
        scratch_shapes=[
            pltpu.VMEM((nq, hd, bq), BF16),
            pltpu.VMEM((nq, hd, bq), BF16),
            pltpu.VMEM((hda, bq), F32),
            pltpu.VMEM((hda, bq), F32),
            pltpu.VMEM((2, 2, ATT_BK, bq), F32),
        ],
        compiler_params=_cparams(("arbitrary", "arbitrary")),
        name="diff_attn",
    )(lam_vecs, sub_col, q, k, vt)


def _ssm_kernel(u_ref, p_ref, pt_ref, pw_ref, mat_ref, r_ref, lre_ref, lim_ref, y_ref,
                ug_ref, yg_ref, w_ref, m_ref, v_ref, sre, sim, hfre, hfim, hbre, hbim, *, cps):
    o = pl.program_id(1)

    @pl.when(o == 0)
    def _():
        _ssm_compute(u_ref, p_ref, pw_ref, mat_ref, r_ref, lre_ref, lim_ref,
                     ug_ref, yg_ref, w_ref, m_ref, v_ref, sre, sim, hfre, hfim, hbre, hbim, cps)

    res = jnp.dot(yg_ref[o], pt_ref[...], preferred_element_type=F32)
    for t8 in range(SUBLANES):
        y_ref[:, t8, :] = res[:, t8 * LANES:(t8 + 1) * LANES]


def _ssm_compute(u_ref, p_ref, pw_ref, mat_ref, r_ref, lre_ref, lim_ref,
                 ug_ref, yg_ref, w_ref, m_ref, v_ref, sre, sim, hfre, hfim, hbre, hbim, cps):
    gb = ug_ref.shape[0]
    nc = ug_ref.shape[1]
    half = LANES // 2
    T = SSM_CHUNK
    n_oct = T // SUBLANES
    oct_lanes = SUBLANES * LANES

    for oc in range(n_oct):
        z = u_ref[:, pl.ds(oc * oct_lanes, oct_lanes)]
        zg = jnp.dot(z, p_ref[...], preferred_element_type=F32).astype(BF16)
        for gi in range(gb):
            ug_ref[gi, :, pl.ds(oc * LANES, LANES)] = zg[:, gi * LANES:(gi + 1) * LANES]

    C = SSM_GROUP
    for gi in range(gb):
        r = r_ref[gi]
        b_re, b_im, c_re, c_im = (mat_ref[gi, i] for i in range(4))
        for s_ in range(T):
            rows = pl.ds(s_ * C, C)
            off = (T - 1 - s_) * C
            m_ref[gi, rows, :] = r[:, off:off + T * C].astype(BF16)
            wr, wi, vr, vi = (pw_ref[gi, i, pl.ds(s_, 1), :] for i in range(4))
            w_ref[gi, rows, pl.ds(0, LANES)] = (wr * b_re - wi * b_im).astype(BF16)
            w_ref[gi, rows, pl.ds(LANES, LANES)] = (wr * b_im + wi * b_re).astype(BF16)
            v_ref[gi, rows, pl.ds(0, LANES)] = (vr * c_re - vi * c_im).astype(BF16)
            v_ref[gi, rows, pl.ds(LANES, LANES)] = (-(vr * c_im + vi * c_re)).astype(BF16)

    for gi in range(gb):
        s = jnp.dot(ug_ref[gi], w_ref[gi], preferred_element_type=F32)
        sre[pl.ds(gi, nc, stride=gb), :] = s[:, :LANES]
        sim[pl.ds(gi, nc, stride=gb), :] = s[:, LANES:]

    zero = jnp.zeros((gb, LANES), F32)
    lre = lre_ref[...]
    lim = lim_ref[...]
    fwd_lane = lax.broadcasted_iota(jnp.int32, (gb, LANES), 1) < half

    def body(j, carry):
        hr, hi = carry
        fresh = (j % cps) == 0
        hr = jnp.where(fresh, 0.0, hr)
        hi = jnp.where(fresh, 0.0, hi)
        jf = pl.multiple_of(j * gb, gb)
        jb = pl.multiple_of((nc - 1 - j) * gb, gb)
        hfre[pl.ds(jf, gb), :] = hr
        hfim[pl.ds(jf, gb), :] = hi
        hbre[pl.ds(jb, gb), :] = hr
        hbim[pl.ds(jb, gb), :] = hi
        sr = jnp.where(fwd_lane, sre[pl.ds(jf, gb), :], sre[pl.ds(jb, gb), :])
        si = jnp.where(fwd_lane, sim[pl.ds(jf, gb), :], sim[pl.ds(jb, gb), :])
        return lre * hr - lim * hi + sr, lre * hi + lim * hr + si

    lax.fori_loop(0, nc, body, (zero, zero), unroll=8)

    fwd_col = lax.broadcasted_iota(jnp.int32, (nc, LANES), 1) < half
    for gi in range(gb):
        hr = jnp.where(fwd_col, hfre[pl.ds(gi, nc, stride=gb), :], hbre[pl.ds(gi, nc, stride=gb), :])
        hi = jnp.where(fwd_col, hfim[pl.ds(gi, nc, stride=gb), :], hbim[pl.ds(gi, nc, stride=gb), :])
        h = jnp.concatenate([hr, hi], axis=1).astype(BF16)
        y = jnp.dot(ug_ref[gi], m_ref[gi], preferred_element_type=F32)
        y = y + lax.dot_general(h, v_ref[gi], (((1,), (1,)), ((), ())),
                                preferred_element_type=F32)
        for oc in range(n_oct):
            yg_ref[oc, :, pl.ds(gi * LANES, LANES)] = y[:, oc * LANES:(oc + 1) * LANES].astype(yg_ref.dtype)


def _lane_regroup_matrix():
    n = SUBLANES * LANES
    i = jnp.arange(n)
    t8, gl, c = i // LANES, (i % LANES) // SSM_GROUP, i % SSM_GROUP
    dest = gl * LANES + t8 * SSM_GROUP + c
    return (dest[:, None] == i[None, :]).astype(BF16)


def _ssm(ub, pw_t, mat_t, r_t, lre, lim, seq_len):
    T = SSM_CHUNK
    nc = ub.shape[0]
    Wd = ub.shape[1] // T
    Ttok = nc * T
    cps = seq_len // T
    gb = SSM_GBLK
    assert gb * SSM_GROUP == LANES and T % SUBLANES == 0 and nc % cps == 0
    n_slab = Wd // LANES
    n_oct = T // SUBLANES
    K = T * SSM_GROUP
    ncols = 2 * pw_t.shape[3]
    pmat = _lane_regroup_matrix()
    once = pl.Buffered(1)
    fixed2 = lambda g, o: (0, 0)
    per_slab = lambda g, o: (g, 0, 0)

    y = pl.pallas_call(
        functools.partial(_ssm_kernel, cps=cps),
        grid=(n_slab, n_oct),
        in_specs=[
            pl.BlockSpec((nc, T * LANES), lambda g, o: (0, g)),
            pl.BlockSpec(pmat.shape, fixed2, pipeline_mode=once),
            pl.BlockSpec(pmat.shape, fixed2, pipeline_mode=once),
            pl.BlockSpec((gb,) + pw_t.shape[1:], lambda g, o: (g, 0, 0, 0)),
            pl.BlockSpec((gb,) + mat_t.shape[1:], lambda g, o: (g, 0, 0, 0)),
            pl.BlockSpec((gb,) + r_t.shape[1:], per_slab),
            pl.BlockSpec((gb, LANES), lambda g, o: (g, 0)),
            pl.BlockSpec((gb, LANES), lambda g, o: (g, 0)),
        ],
        out_specs=pl.BlockSpec((None, nc, SUBLANES, LANES), lambda g, o: (g, 0, o, 0)),
        out_shape=jax.ShapeDtypeStruct((n_slab, nc, T, LANES), F32),
        scratch_shapes=[
            pltpu.VMEM((gb, nc, K), BF16),
            pltpu.VMEM((n_oct, nc, SUBLANES * LANES), BF16),
            pltpu.VMEM((gb, K, ncols), BF16),
            pltpu.VMEM((gb, K, K), BF16),
            pltpu.VMEM((gb, K, ncols), BF16),
        ] + [pltpu.VMEM((nc * gb, LANES), F32)] * 6,
        compiler_params=_cparams(("arbitrary", "arbitrary")),
        name="s5_chunked",
    )(ub, pmat, pmat.T, pw_t, mat_t, r_t, lre, lim)
    return y.reshape(n_slab, Ttok, LANES)


def _ssm_tables(a_re, a_im, log_dt, b_re, b_im, c_re, c_im):
    T = SSM_CHUNK
    G, P = a_re.shape[1], a_re.shape[2]
    C = b_re.shape[3]
    dt = jnp.exp(log_dt)[..., None]
    zr = a_re * dt
    zi = a_im * dt
    tau = jnp.arange(T + 1, dtype=F32)[:, None]
    mag = jnp.exp(zr[:, :, None] * tau)
    pr = mag * jnp.cos(zi[:, :, None] * tau)
    pi = mag * jnp.sin(zi[:, :, None] * tau)
    ab_re, ab_im = pr[:, :, 1], pi[:, :, 1]
    den = a_re * a_re + a_im * a_im
    nr = ab_re - 1.0
    f_re = ((nr * a_re + ab_im * a_im) / den)[:, :, None, :]
    f_im = ((ab_im * a_re - nr * a_im) / den)[:, :, None, :]
    bt_re = b_re.transpose(0, 1, 3, 2)
    bt_im = b_im.transpose(0, 1, 3, 2)
    bbt_re = f_re * bt_re - f_im * bt_im
    bbt_im = f_re * bt_im + f_im * bt_re

    fb = lambda f, b: jnp.concatenate([f, b], axis=-1)
    pw_t = jnp.stack([fb(pr[0, :, T - 1::-1], pr[1, :, :T]), fb(pi[0, :, T - 1::-1], pi[1, :, :T]),
                      fb(pr[0, :, 1:], pr[1, :, :0:-1]), fb(pi[0, :, 1:], pi[1, :, :0:-1])], axis=1)
    mat_t = jnp.stack([fb(bbt_re[0], bbt_re[1]), fb(bbt_im[0], bbt_im[1]),
                       fb(c_re[0], c_re[1]), fb(c_im[0], c_im[1])], axis=1)

    p0r = pr[:, :, :T, None, :]
    p0i = pi[:, :, :T, None, :]
    cp_re = c_re[:, :, None] * p0r - c_im[:, :, None] * p0i
    cp_im = c_re[:, :, None] * p0i + c_im[:, :, None] * p0r
    cb = jnp.einsum('dgtcq,dgkq->dgtck', jnp.concatenate([cp_re, -cp_im], axis=-1),
                    jnp.concatenate([bbt_re, bbt_im], axis=-1), precision=lax.Precision.HIGH)

    kern = jnp.concatenate([cb[1, :, T - 1:0:-1], (cb[0, :, 0] + cb[1, :, 0])[:, None], cb[0, :, 1:],
                            jnp.zeros_like(cb[0, :, :1])], axis=1)
    r_t = kern.transpose(0, 3, 1, 2).reshape(G, C, 2 * T * C)

    lre = jnp.concatenate([pr[0, :, T], pr[1, :, T]], axis=1)
    lim = jnp.concatenate([pi[0, :, T], pi[1, :, T]], axis=1)
    return pw_t, mat_t, r_t, lre, lim


def _layer_norm(r, g, b):
    mu = jnp.mean(r, axis=-1, keepdims=True)
    d = r - mu
    var = jnp.mean(d * d, axis=-1, keepdims=True)
    return d * lax.rsqrt(var + LN_EPS) * g + b


def _mix_kernel(x_ref, a_ref, y_ref, u_ref, d_ref, wglu_ref, wout_ref, g_ref, b_ref, o_ref, ob_ref,
                *, alpha):
    n_heads, tm, hd = a_ref.shape
    wa = n_heads * hd
    for half in range(2):
        rows = pl.ds(half * (tm // 2), tm // 2)
        y = jnp.concatenate([y_ref[sl, rows, :] for sl in range(y_ref.shape[0])], axis=1)
        yy = y + d_ref[...] * u_ref[rows, :]
        z = jax.nn.gelu(yy)
        gate = jax.nn.sigmoid(jnp.dot(z.astype(BF16), wglu_ref[...], preferred_element_type=F32))
        s = (z * gate).astype(BF16)
        a = jnp.concatenate([a_ref[h, rows, :] for h in range(n_heads)], axis=1)
        mixed = jnp.dot(a, wout_ref[pl.ds(0, wa), :], preferred_element_type=F32)
        mixed = mixed + jnp.dot(s, wout_ref[pl.ds(wa, s.shape[1]), :], preferred_element_type=F32)
        r = alpha * x_ref[rows, :] + mixed
        o = _layer_norm(r, g_ref[...], b_ref[...])
        o_ref[rows, :] = o
        ob_ref[rows, :] = o.astype(BF16)


def _mix(x2d, attn, y, u, d, wglu, wout, g, b, alpha, tm=512):
    T, D = x2d.shape
    W = u.shape[1]
    row = lambda i: (i, 0)
    fixed = lambda i: (0, 0)
    once = pl.Buffered(1)
    return pl.pallas_call(
        functools.partial(_mix_kernel, alpha=alpha),
        grid=(T // tm,),
        in_specs=[
            pl.BlockSpec((tm, D), row),
            pl.BlockSpec((attn.shape[0], tm, attn.shape[2]), lambda i: (0, i, 0)),
            pl.BlockSpec((y.shape[0], tm, y.shape[2]), lambda i: (0, i, 0)),
            pl.BlockSpec((tm, W), row),
            pl.BlockSpec((1, W), fixed),
            pl.BlockSpec(wglu.shape, fixed, pipeline_mode=once),
            pl.BlockSpec(wout.shape, fixed, pipeline_mode=once),
            pl.BlockSpec((1, D), fixed),
            pl.BlockSpec((1, D), fixed),
        ],
        out_specs=[pl.BlockSpec((tm, D), row), pl.BlockSpec((tm, D), row)],
        out_shape=[jax.ShapeDtypeStruct((T, D), F32), jax.ShapeDtypeStruct((T, D), BF16)],
        compiler_params=_cparams(("arbitrary",)),
        name="glu_outproj_ln",
    )(x2d, attn, y, u, d, wglu, wout, g, b)


def _ffn_up_kernel(x_ref, wg_ref, wu_ref, h_ref):
    x = x_ref[...]
    g = jnp.dot(x, wg_ref[...], preferred_element_type=F32)
    up = jnp.dot(x, wu_ref[...], preferred_element_type=F32)
    h_ref[...] = (jax.nn.silu(g) * up).astype(h_ref.dtype)


def _ffn_up(xb, wg, wu, tm=1024, tf=512):
    T, D = xb.shape
    F = wg.shape[1]
    return pl.pallas_call(
        _ffn_up_kernel,
        grid=(T // tm, F // tf),
        in_specs=[
            pl.BlockSpec((tm, D), lambda i, j: (i, 0)),
            pl.BlockSpec((D, tf), lambda i, j: (0, j)),
            pl.BlockSpec((D, tf), lambda i, j: (0, j)),
        ],
        out_specs=pl.BlockSpec((tm, tf), lambda i, j: (i, j)),
        out_shape=jax.ShapeDtypeStruct((T, F), BF16),
        compiler_params=_cparams(("arbitrary", "arbitrary")),
        name="ffn_up",
    )(xb, wg, wu)


def _ffn_down_kernel(h_ref, wd_ref, x_ref, g_ref, b_ref, o_ref, *, alpha):
    down = jnp.dot(h_ref[...], wd_ref[...], preferred_element_type=F32)
    o_ref[...] = _layer_norm(alpha * x_ref[...] + down, g_ref[...], b_ref[...])


def _ffn_down(h, wd, x1, g, b, alpha, tm=256):
    T, F = h.shape
    D = wd.shape[1]
    return pl.pallas_call(
        functools.partial(_ffn_down_kernel, alpha=alpha),
        grid=(T // tm,),
        in_specs=[
            pl.BlockSpec((tm, F), lambda i: (i, 0)),
            pl.BlockSpec((F, D), lambda i: (0, 0), pipeline_mode=pl.Buffered(1)),
            pl.BlockSpec((tm, D), lambda i: (i, 0)),
            pl.BlockSpec((1, D), lambda i: (0, 0)),
            pl.BlockSpec((1, D), lambda i: (0, 0)),
        ],
        out_specs=pl.BlockSpec((tm, D), lambda i: (i, 0)),
        out_shape=jax.ShapeDtypeStruct((T, D), F32),
        compiler_params=_cparams(("arbitrary",)),
        name="ffn_down_ln",
    )(h, wd, x1, g, b)


def _rope_tables(L):
    half = ROT_DIM // 2
    inv = ROPE_THETA ** (-jnp.arange(0, ROT_DIM, 2, dtype=F32) / ROT_DIM)
    ang = jnp.arange(L, dtype=F32)[:, None] * inv[None, :]
    cos, sin = jnp.cos(ang), jnp.sin(ang)
    lane = jnp.arange(LANES)
    d = lane % DIFF_HEAD_DIM
    freq = jnp.arange(half)[:, None]
    first = ((d < half)[None, :] & (d[None, :] == freq)).astype(F32)
    second = (((d >= half) & (d < ROT_DIM))[None, :] & (d[None, :] - half == freq)).astype(F32)
    spread = functools.partial(jnp.dot, precision=lax.Precision.HIGHEST)
    c = spread(cos, first + second) + (d >= ROT_DIM).astype(F32)[None, :]
    sa = -spread(sin, first)
    sb = spread(sin, second)
    return jnp.stack([c, sa, sb])


def _prepare_layer(p):
    (w_in, lq1, lk1, lq2, lk2, subln_w, a_re, a_im, log_dt, b_re, b_im, c_re, c_im, ssm_d, w_glu,
     w_out, ln1_g, ln1_b, w_gate, w_up, w_down, ln2_g, ln2_b) = p
    W = w_glu.shape[0]
    D = w_in.shape[0]
    row = lambda t, n: t.reshape(1, n).astype(F32)
    return dict(
        w_qku=jnp.concatenate([w_in[:, :2 * W], w_in[:, 3 * W:]], axis=1).astype(BF16),
        wvt=w_in[:, 2 * W:3 * W].T.astype(BF16),
        lam_vecs=jnp.stack([lq1, lk1, lq2, lk2]).astype(F32),
        sub_col=subln_w.astype(F32).reshape(-1, 1),
        ssm=_ssm_tables(a_re, a_im, log_dt, b_re, b_im, c_re, c_im),
        ssm_d=row(ssm_d, W), w_glu=w_glu.astype(BF16), w_out=w_out.astype(BF16),
        ln1_g=row(ln1_g, D), ln1_b=row(ln1_b, D),
        w_gate=w_gate.astype(BF16), w_up=w_up.astype(BF16), w_down=w_down.astype(BF16),
        ln2_g=row(ln2_g, D), ln2_b=row(ln2_b, D),
    )


def _encoder_layer(x, layer_idx, depth, pp):
    B, L, D = x.shape
    T = B * L
    W = pp["w_glu"].shape[0]
    alpha = (2 * depth) ** 0.25
    lam_init = 0.8 - 0.6 * math.exp(-0.3 * layer_idx)

    x2d = x.reshape(T, D)
    q, k, vt, u, ub = _inproj(x2d.astype(BF16), pp["w_qku"], pp["wvt"], _rope_tables(L), L)
    attn = _attention(q, k, vt, pp["lam_vecs"], pp["sub_col"], lam_init, L)
    y = _ssm(ub, *pp["ssm"], L)
    x1, x1b = _mix(x2d, attn, y, u, pp["ssm_d"], pp["w_glu"], pp["w_out"], pp["ln1_g"], pp["ln1_b"], alpha)
    h = _ffn_up(x1b, pp["w_gate"], pp["w_up"])
    out = _ffn_down(h, pp["w_down"], x1, pp["ln2_g"], pp["ln2_b"], alpha)
    return out.reshape(B, L, D)


def kernel(x_prompt, x_sample, w_in, lambda_q1, lambda_k1, lambda_q2, lambda_k2, subln_w, ssm_a_re, ssm_a_im, ssm_log_dt, ssm_b_re, ssm_b_im, ssm_c_re, ssm_c_im, ssm_d, w_glu, w_out, ln1_g, ln1_b, w_gate, w_up, w_down, ln2_g, ln2_b):
    params = (w_in, lambda_q1, lambda_k1, lambda_q2, lambda_k2, subln_w,
              ssm_a_re, ssm_a_im, ssm_log_dt, ssm_b_re, ssm_b_im, ssm_c_re, ssm_c_im, ssm_d, w_glu,
              w_out, ln1_g, ln1_b, w_gate, w_up, w_down, ln2_g, ln2_b)
    depth = w_in.shape[0]
    y_prompt, y_sample = x_prompt, x_sample
    for layer in range(depth):
        lp = _prepare_layer(tuple(t[layer] for t in params))
        y_prompt = _encoder_layer(y_prompt, layer, depth, lp)
        y_sample = _encoder_layer(y_sample, layer, depth, lp)
    return (y_prompt, y_sample)
```
